```python
import math
import jax, jax.numpy as jnp
from jax import lax
import numpy as np

D_MODEL = 2048
BATCH = 2
SEQ = 8192
DEPTH = 2

N_EVEN = (DEPTH + 1) // 2
N_ODD = DEPTH // 2
EPS = 1e-6

CONV_WIDTH = D_MODEL // 2
CONV_K = 3
HG_HEAD_DIM = 128
HG_HEADS = (D_MODEL // 2) // HG_HEAD_DIM
HG_WIDTH = HG_HEADS * HG_HEAD_DIM
HG_CHUNK = 64
IN_EVEN = 3 * CONV_WIDTH + 4 * HG_WIDTH
MIX_EVEN = CONV_WIDTH + HG_WIDTH
S5_WIDTH = D_MODEL
S5_GROUP = 16
S5_GROUPS = S5_WIDTH // S5_GROUP
S5_STATE = 64
S5_CHUNK_MAX = 1024
DT_MIN = 1e-3
DT_MAX = 1e-1
MOE_GROUPS = 4
MOE_EXPERTS_PER_GROUP = 8
MOE_EXPERTS = MOE_GROUPS * MOE_EXPERTS_PER_GROUP
MOE_TOPK = 2
MOE_FF = D_MODEL // 2
MOE_ROWS = 256

kernel_name = "hybrid_conv_hgrn2_s5_hmoe"


def rmsnorm(x, g):
    xf = x.astype(jnp.float32)
    y = xf * lax.rsqrt(jnp.mean(xf * xf, axis=-1, keepdims=True) + EPS)
    return (y * g.astype(jnp.float32)).astype(x.dtype)


def short_conv(u, w):
    c = u.shape[-1]
    return lax.conv_general_dilated(
        u, w.astype(u.dtype)[:, None, :], window_strides=(1,),
        padding=[(CONV_K - 1, 0)], dimension_numbers=('NWC', 'WIO', 'NWC'),
        feature_group_count=c)


def hgrn2_mix(q, f_logit, v, g, lb, norm_g):
    bsz, s, _ = q.shape
    n = s // HG_CHUNK
    f32 = jnp.float32
    f = lb + (1.0 - lb) * jax.nn.sigmoid(f_logit.astype(f32))
    log_f = jnp.log(f)
    k = 1.0 - f

    def to_chunks(t):
        return t.astype(f32).reshape(bsz, n, HG_CHUNK, HG_HEADS, HG_HEAD_DIM).transpose(1, 0, 3, 2, 4)

    qc, kc, vc, lc = to_chunks(q), to_chunks(k), to_chunks(v), to_chunks(log_f)
    causal = jnp.tril(jnp.ones((HG_CHUNK, HG_CHUNK), dtype=bool))
    mid = HG_CHUNK // 2

    def step(state, inp):
        qb, kb, vb, lgb = inp
        b = jnp.cumsum(lgb, axis=2)
        ref = b[:, :, mid - 1:mid]
        b_last = b[:, :, -1:]
        o_inter = jnp.einsum('bhck,bhkv->bhcv', qb * jnp.exp(b), state)
        scores = jnp.einsum('bhck,bhsk->bhcs', qb * jnp.exp(b - ref), kb * jnp.exp(ref - b))
        scores = jnp.where(causal, scores, 0.0)
        o_intra = jnp.einsum('bhcs,bhsv->bhcv', scores, vb)
        new_state = (jnp.exp(b_last[:, :, 0])[..., None] * state
                     + jnp.einsum('bhsk,bhsv->bhkv', kb * jnp.exp(b_last - b), vb))
        return new_state, o_inter + o_intra

    s0 = jnp.zeros((bsz, HG_HEADS, HG_HEAD_DIM, HG_HEAD_DIM), f32)
    _, o = lax.scan(step, s0, (qc, kc, vc, lc))
    o = o.transpose(1, 0, 3, 2, 4).reshape(bsz, s, HG_HEADS, HG_HEAD_DIM)
    o = o * lax.rsqrt(jnp.mean(o * o, axis=-1, keepdims=True) + EPS) * norm_g.astype(f32)
    o = o.reshape(bsz, s, HG_WIDTH) * jax.nn.silu(g.astype(f32))
    return o.astype(q.dtype)


def _complex_affine_combine(e1, e2):
    a1r, a1i, b1r, b1i = e1
    a2r, a2i, b2r, b2i = e2
    return (a2r * a1r - a2i * a1i,
            a2r * a1i + a2i * a1r,
            a2r * b1r - a2i * b1i + b2r,
            a2r * b1i + a2i * b1r + b2i)


def s5_mix(u, lam_re, lam_im, log_dt, b_re, b_im, c_re, c_im, d_skip):
    bsz, s, _ = u.shape
    f32 = jnp.float32
    L = math.gcd(s, S5_CHUNK_MAX)
    n = s // L
    dt = jnp.exp(log_dt.astype(f32))[:, None]
    lr = lam_re.astype(f32)
    li = lam_im.astype(f32)
    mag = jnp.exp(lr * dt)
    ar = mag * jnp.cos(li * dt)
    ai = mag * jnp.sin(li * dt)
    den = lr * lr + li * li
    cr = ((ar - 1.0) * lr + ai * li) / den
    ci = (ai * lr - (ar - 1.0) * li) / den
    bb_re = cr[..., None] * b_re.astype(f32) - ci[..., None] * b_im.astype(f32)
    bb_im = cr[..., None] * b_im.astype(f32) + ci[..., None] * b_re.astype(f32)
    cre = c_re.astype(f32)
    cim = c_im.astype(f32)
    a_r = jnp.broadcast_to(ar, (bsz, L, S5_GROUPS, S5_STATE))
    a_i = jnp.broadcast_to(ai, (bsz, L, S5_GROUPS, S5_STATE))
    uc = u.astype(f32).reshape(bsz, n, L, S5_GROUPS, S5_GROUP).transpose(1, 0, 2, 3, 4)

    def step(carry, ub):
        hr0, hi0 = carry
        xr = jnp.einsum('blgc,gpc->blgp', ub, bb_re)
        xi = jnp.einsum('blgc,gpc->blgp', ub, bb_im)
        pr, pi, hr, hi = lax.associative_scan(_complex_affine_combine, (a_r, a_i, xr, xi), axis=1)
        hr = hr + pr * hr0[:, None] - pi * hi0[:, None]
        hi = hi + pr * hi0[:, None] + pi * hr0[:, None]
        y = jnp.einsum('blgp,gcp->blgc', hr, cre) - jnp.einsum('blgp,gcp->blgc', hi, cim)
        return (hr[:, -1], hi[:, -1]), y

    h0 = jnp.zeros((bsz, S5_GROUPS, S5_STATE), f32)
    _, y = lax.scan(step, (h0, h0), uc)
    y = y.transpose(1, 0, 2, 3, 4).reshape(bsz, s, S5_WIDTH)
    y = y + d_skip.astype(f32) * u.astype(f32)
    return y.astype(u.dtype)


def hier_moe(h, w_coarse, b_coarse, w_fine, b_fine, w_gate, w_up, w_down):
    bsz, s, d = h.shape
    x = h.reshape(-1, d)
    t = x.shape[0]
    f32 = jnp.float32
    p_coarse = jax.nn.softmax((x @ w_coarse + b_coarse).astype(f32), axis=-1)
    p_top, grp = lax.top_k(p_coarse, 1)
    fine = (x @ w_fine + b_fine).astype(f32).reshape(t, MOE_GROUPS, MOE_EXPERTS_PER_GROUP)
    fine_sel = fine[jnp.arange(t), grp[:, 0]]
    top_val, top_idx = lax.top_k(fine_sel, MOE_TOPK)
    gates = p_top * jax.nn.softmax(top_val, axis=-1)
    experts = grp * MOE_EXPERTS_PER_GROUP + top_idx
    n_assign = t * MOE_TOPK
    e_flat = experts.reshape(-1)
    tok = jnp.repeat(jnp.arange(t, dtype=jnp.int32), MOE_TOPK)
    w_flat = gates.reshape(-1)
    order = jnp.argsort(e_flat)
    e_s = e_flat[order]
    tok_s = tok[order]
    w_s = w_flat[order]
    counts = jnp.bincount(e_flat, length=MOE_EXPERTS)
    starts = jnp.cumsum(counts) - counts
    padded = (counts + MOE_ROWS - 1) // MOE_ROWS * MOE_ROWS
    pend = jnp.cumsum(padded)
    pstart = pend - padded
    dest = pstart[e_s] + jnp.arange(n_assign) - starts[e_s]
    n_groups = -(-n_assign // MOE_ROWS) + MOE_EXPERTS
    total = n_groups * MOE_ROWS
    buf_tok = jnp.full((total,), t, dtype=jnp.int32).at[dest].set(tok_s)
    buf_w = jnp.zeros((total,), f32).at[dest].set(w_s)
    grp_expert = jnp.minimum(
        jnp.searchsorted(pend, jnp.arange(n_groups) * MOE_ROWS, side='right'), MOE_EXPERTS - 1)
    x_pad = jnp.concatenate([x, jnp.zeros((1, d), x.dtype)], axis=0)
    xb = x_pad[buf_tok].reshape(n_groups, MOE_ROWS, d)

    def expert_rows(args):
        xm, e = args
        hid = jax.nn.silu(xm @ w_gate[e]) * (xm @ w_up[e])
        return hid @ w_down[e]

    yb = lax.map(expert_rows, (xb, grp_expert)).reshape(total, d)
    y = jnp.zeros((t + 1, d), f32).at[buf_tok].add(yb.astype(f32) * buf_w[:, None])[:t]
    return y.astype(h.dtype).reshape(bsz, s, d)


def setup_inputs(seed: int = 0) -> dict:
    key = jax.random.key(seed)
    ks = jax.random.split(key, 26)
    nrm = jax.random.normal
    f32 = jnp.float32
    D = D_MODEL
    x = nrm(ks[0], (BATCH, SEQ, D), f32)
    g_mix = 1.0 + 0.01 * nrm(ks[1], (DEPTH, D), f32)
    g_ffn = 1.0 + 0.01 * nrm(ks[2], (DEPTH, D), f32)
    g_final = 1.0 + 0.01 * nrm(ks[3], (D,), f32)
    w_in_even = nrm(ks[4], (N_EVEN, D, IN_EVEN), f32) * D ** -0.5
    conv_w = nrm(ks[5], (N_EVEN, CONV_K, CONV_WIDTH), f32) * CONV_K ** -0.5
    hg_lower_bound = 0.1 * nrm(ks[6], (N_EVEN + 1, HG_WIDTH), f32)
    hg_norm_g = 1.0 + 0.01 * nrm(ks[7], (N_EVEN, HG_HEAD_DIM), f32)
    w_out_even = nrm(ks[8], (N_EVEN, MIX_EVEN, D), f32) * MIX_EVEN ** -0.5
    s5_lambda_re = -0.5 + 0.01 * nrm(ks[9], (N_ODD, S5_GROUPS, S5_STATE), f32)
    s5_lambda_im = (jnp.pi * jnp.arange(S5_STATE, dtype=f32)
                    + 0.01 * nrm(ks[10], (N_ODD, S5_GROUPS, S5_STATE), f32))
    s5_log_dt = jax.random.uniform(ks[11], (N_ODD, S5_GROUPS), f32,
                                   minval=math.log(DT_MIN), maxval=math.log(DT_MAX))
    s5_b_re = nrm(ks[12], (N_ODD, S5_GROUPS, S5_STATE, S5_GROUP), f32) * (2 * S5_GROUP) ** -0.5
    s5_b_im = nrm(ks[13], (N_ODD, S5_GROUPS, S5_STATE, S5_GROUP), f32) * (2 * S5_GROUP) ** -0.5
    s5_c_re = nrm(ks[14], (N_ODD, S5_GROUPS, S5_GROUP, S5_STATE), f32) * S5_STATE ** -0.5
    s5_c_im = nrm(ks[15], (N_ODD, S5_GROUPS, S5_GROUP, S5_STATE), f32) * S5_STATE ** -0.5
    s5_d = nrm(ks[16], (N_ODD, S5_WIDTH), f32)
    w_glu = nrm(ks[17], (N_ODD, S5_WIDTH, 2 * D), f32) * S5_WIDTH ** -0.5
    moe_w_coarse = nrm(ks[18], (DEPTH, D, MOE_GROUPS), f32) * D ** -0.5
    moe_b_coarse = 0.01 * nrm(ks[19], (DEPTH, MOE_GROUPS), f32)
    moe_w_fine = nrm(ks[20], (DEPTH, D, MOE_EXPERTS), f32) * D ** -0.5
    moe_b_fine = 0.01 * nrm(ks[21], (DEPTH, MOE_EXPERTS), f32)
    moe_w_gate = nrm(ks[22], (DEPTH, MOE_EXPERTS, D, MOE_FF), f32) * D ** -0.5
    moe_w_up = nrm(ks[23], (DEPTH, MOE_EXPERTS, D, MOE_FF), f32) * D ** -0.5
    moe_w_down = nrm(ks[24], (DEPTH, MOE_EXPERTS, MOE_FF, D), f32) * MOE_FF ** -0.5
    return {"x": x, "g_mix": g_mix, "g_ffn": g_ffn, "g_final": g_final,
            "w_in_even": w_in_even, "conv_w": conv_w, "hg_lower_bound": hg_lower_bound,
            "hg_norm_g": hg_norm_g, "w_out_even": w_out_even,
            "s5_lambda_re": s5_lambda_re, "s5_lambda_im": s5_lambda_im, "s5_log_dt": s5_log_dt,
            "s5_b_re": s5_b_re, "s5_b_im": s5_b_im, "s5_c_re": s5_c_re, "s5_c_im": s5_c_im,
            "s5_d": s5_d, "w_glu": w_glu,
            "moe_w_coarse": moe_w_coarse, "moe_b_coarse": moe_b_coarse,
            "moe_w_fine": moe_w_fine, "moe_b_fine": moe_b_fine,
            "moe_w_gate": moe_w_gate, "moe_w_up": moe_w_up, "moe_w_down": moe_w_down}


def reference(x, g_mix, g_ffn, g_final, w_in_even, conv_w, hg_lower_bound, hg_norm_g,
              w_out_even, s5_lambda_re, s5_lambda_im, s5_log_dt, s5_b_re, s5_b_im,
              s5_c_re, s5_c_im, s5_d, w_glu, moe_w_coarse, moe_b_coarse, moe_w_fine,
              moe_b_fine, moe_w_gate, moe_w_up, moe_w_down):
    lb_all = jnp.cumsum(jax.nn.softmax(hg_lower_bound.astype(jnp.float32), axis=0), axis=0)
    for layer in range(DEPTH):
        j = layer // 2
        h = rmsnorm(x, g_mix[layer])
        if layer % 2 == 0:
            z = h @ w_in_even[j]
            a_b, a_c, a_h, q, f_logit, v, g = jnp.split(z, 7, axis=-1)
            y_a = a_b * short_conv(a_c * a_h, conv_w[j])
            y_b = hgrn2_mix(q, f_logit, v, g, lb_all[j], hg_norm_g[j])
            mix = jnp.concatenate([y_a, y_b], axis=-1) @ w_out_even[j]
        else:
            y = s5_mix(h, s5_lambda_re[j], s5_lambda_im[j], s5_log_dt[j], s5_b_re[j],
                       s5_b_im[j], s5_c_re[j], s5_c_im[j], s5_d[j])
            zg = jax.nn.gelu(y) @ w_glu[j]
            mix = zg[..., :D_MODEL] * jax.nn.sigmoid(zg[..., D_MODEL:])
        x = x + mix
        x = x + hier_moe(rmsnorm(x, g_ffn[layer]), moe_w_coarse[layer], moe_b_coarse[layer],
                         moe_w_fine[layer], moe_b_fine[layer], moe_w_gate[layer],
                         moe_w_up[layer], moe_w_down[layer])
    return rmsnorm(x, g_final)
```

```python
import functools
import math

import jax
import jax.numpy as jnp
from jax import lax
from jax.experimental import pallas as pl
from jax.experimental.pallas import tpu as pltpu

F32 = jnp.float32
BF16 = jnp.bfloat16
EPS = 1e-6

LANES = 128
HG_HEAD_DIM = 128
HG_CHUNK = 64
CONV_K = 3
S5_GROUP = 16
S5_STATE = 64
S5_CHUNK = 64
MOE_GROUPS = 4
MOE_EPG = 8
MOE_EXPERTS = MOE_GROUPS * MOE_EPG
MOE_ROWS = 256
ROUTER_ROWS = 40
VMEM_LIMIT = 56 * 1024 * 1024


def _cparams(sem, vmem=VMEM_LIMIT):
    return pltpu.CompilerParams(dimension_semantics=sem, vmem_limit_bytes=vmem)


def _rms(x, g):
    ms = jnp.mean(x * x, axis=-1, keepdims=True)
    return x * lax.rsqrt(ms + EPS) * g


def _norm_kernel(x_ref, g_ref, o_ref):
    o_ref[...] = _rms(x_ref[...], g_ref[...]).astype(o_ref.dtype)


def rmsnorm_cast(x, g, tm=512):
    t, d = x.shape
    return pl.pallas_call(
        _norm_kernel,
        grid=(t // tm,),
        in_specs=[pl.BlockSpec((tm, d), lambda i: (i, 0)),
                  pl.BlockSpec((1, d), lambda i: (0, 0))],
        out_specs=pl.BlockSpec((tm, d), lambda i: (i, 0)),
        out_shape=jax.ShapeDtypeStruct((t, d), BF16),
        compiler_params=_cparams(("parallel",)),
        name="rmsnorm_cast",
    )(x, g.reshape(1, d))


def _norm_matmul_kernel(x_ref, g_ref, w_ref, o_ref, h_ref):
    @pl.when(pl.program_id(1) == 0)
    def _():
        h_ref[...] = _rms(x_ref[...], g_ref[...]).astype(BF16)

    o_ref[...] = jnp.dot(h_ref[...], w_ref[...],
                         preferred_element_type=F32).astype(o_ref.dtype)


def norm_matmul(x, g, w, tm=512, tn=1024):
    t, d = x.shape
    n = w.shape[1]
    return pl.pallas_call(
        _norm_matmul_kernel,
        grid=(t // tm, n // tn),
        in_specs=[pl.BlockSpec((tm, d), lambda i, j: (i, 0)),
                  pl.BlockSpec((1, d), lambda i, j: (0, 0)),
                  pl.BlockSpec((d, tn), lambda i, j: (0, j))],
        out_specs=pl.BlockSpec((tm, tn), lambda i, j: (i, j)),
        out_shape=jax.ShapeDtypeStruct((t, n), BF16),
        scratch_shapes=[pltpu.VMEM((tm, d), BF16)],
        compiler_params=_cparams(("parallel", "arbitrary")),
        name="norm_matmul",
    )(x, g.reshape(1, d), w)


def _split3(x):
    hi = x.astype(BF16)
    r1 = x - hi.astype(F32)
    mid = r1.astype(BF16)
    lo = (r1 - mid.astype(F32)).astype(BF16)
    return hi, mid, lo


def _hgrn2_kernel(q_ref, f_ref, v_ref, g_ref, lb_ref, ng_ref, o_ref, st_ref):
    c = HG_CHUNK
    n_chunks = q_ref.shape[0] // c
    lb = lb_ref[...]
    ng = ng_ref[...]
    row = lax.broadcasted_iota(jnp.int32, (c, c), 0)
    col = lax.broadcasted_iota(jnp.int32, (c, c), 1)
    causal = row >= col
    tril = causal.astype(BF16)
    mid = c // 2
    st_ref[...] = jnp.zeros_like(st_ref)

    def step(n, carry):
        rows = pl.ds(pl.multiple_of(n * c, c), c)
        q = q_ref[rows, :].astype(F32)
        fl = f_ref[rows, :].astype(F32)
        v = v_ref[rows, :]
        gate = g_ref[rows, :].astype(F32)
        f = lb + (1.0 - lb) * jax.nn.sigmoid(fl)
        lg = jnp.log(f)
        k = 1.0 - f
        b = sum(jnp.dot(tril, part, preferred_element_type=F32) for part in _split3(lg))
        ref = b[mid - 1:mid, :]
        b_last = b[c - 1:c, :]
        st = st_ref[...]
        qe = (q * jnp.exp(b)).astype(BF16)
        o_inter = lax.dot_general(qe, st.astype(BF16), (((1,), (1,)), ((), ())),
                                  preferred_element_type=F32)
        qs = (q * jnp.exp(b - ref)).astype(BF16)
        ks = (k * jnp.exp(ref - b)).astype(BF16)
        scores = lax.dot_general(qs, ks, (((1,), (1,)), ((), ())),
                                 preferred_element_type=F32)
        scores = jnp.where(causal, scores, 0.0).astype(BF16)
        o = o_inter + jnp.dot(scores, v, preferred_element_type=F32)
        kd = (k * jnp.exp(b_last - b)).astype(BF16)
        upd = lax.dot_general(v, kd, (((0,), (0,)), ((), ())),
                              preferred_element_type=F32)
        st_ref[...] = st * jnp.exp(b_last) + upd
        o = o * lax.rsqrt(jnp.mean(o * o, axis=-1, keepdims=True) + EPS) * ng
        o_ref[rows, :] = (o * (gate * jax.nn.sigmoid(gate))).astype(o_ref.dtype)
        return carry

    lax.fori_loop(0, n_chunks, step, 0)


def hgrn2(z, lb, norm_g, bsz, seq, width, col0):
    hd = HG_HEAD_DIM
    heads = width // hd
    cb = col0 // hd
    wb = width // hd

    def zspec(k):
        return pl.BlockSpec((seq, hd), lambda b, h: (b, cb + k * wb + h))

    return pl.pallas_call(
        _hgrn2_kernel,
        grid=(bsz, heads),
        in_specs=[zspec(0), zspec(1), zspec(2), zspec(3),
                  pl.BlockSpec((1, hd), lambda b, h: (0, h)),
                  pl.BlockSpec((1, hd), lambda b, h: (0, 0))],
        out_specs=pl.BlockSpec((seq, hd), lambda b, h: (b, h)),
        out_shape=jax.ShapeDtypeStruct((bsz * seq, width), BF16),
        scratch_shapes=[pltpu.VMEM((hd, hd), F32)],
        compiler_params=_cparams(("parallel", "parallel")),
        name="hgrn2",
    )(z, z, z, z, lb.reshape(1, width), norm_g.reshape(1, hd))


def _outproj_kernel(seq_tiles, ab_ref, ac_ref, ah_ref, hc_ref, hh_ref, cw_ref,
                    ob_ref, x_ref, wa_ref, wb_ref, o_ref):
    i = pl.program_id(0)
    u = ac_ref[...].astype(F32) * ah_ref[...].astype(F32)
    halo = hc_ref[...].astype(F32) * hh_ref[...].astype(F32)
    halo = jnp.where(i % seq_tiles == 0, 0.0, halo)
    hr = halo.shape[0]
    row = lax.broadcasted_iota(jnp.int32, u.shape, 0)
    u1 = jnp.where(row == 0, halo[hr - 1:hr, :], pltpu.roll(u, 1, axis=0))
    u2 = jnp.where(row == 0, halo[hr - 2:hr - 1, :],
                   jnp.where(row == 1, halo[hr - 1:hr, :], pltpu.roll(u, 2, axis=0)))
    cw = cw_ref[...]
    conv = cw[2:3, :] * u + cw[1:2, :] * u1 + cw[0:1, :] * u2
    ya = (ab_ref[...].astype(F32) * conv).astype(BF16)
    mix = jnp.dot(ya, wa_ref[...], preferred_element_type=F32)
    mix = mix + jnp.dot(ob_ref[...], wb_ref[...], preferred_element_type=F32)
    o_ref[...] = x_ref[...] + mix


def outproj_even(z, conv_w, o_b, x, w_out, seq, tm=256, halo=16):
    t, d = x.shape
    cw = conv_w.shape[1]
    hw = o_b.shape[1]
    hb = tm // halo
    kern = functools.partial(_outproj_kernel, seq // tm)
    return pl.pallas_call(
        kern,
        grid=(t // tm,),
        in_specs=[pl.BlockSpec((tm, cw), lambda i: (i, 0)),
                  pl.BlockSpec((tm, cw), lambda i: (i, 1)),
                  pl.BlockSpec((tm, cw), lambda i: (i, 2)),
                  pl.BlockSpec((halo, cw), lambda i: (jnp.maximum(i * hb - 1, 0), 1)),
                  pl.BlockSpec((halo, cw), lambda i: (jnp.maximum(i * hb - 1, 0), 2)),
                  pl.BlockSpec((CONV_K, cw), lambda i: (0, 0)),
                  pl.BlockSpec((tm, hw), lambda i: (i, 0)),
                  pl.BlockSpec((tm, d), lambda i: (i, 0)),
                  pl.BlockSpec((cw, d), lambda i: (0, 0)),
                  pl.BlockSpec((hw, d), lambda i: (1, 0))],
        out_specs=pl.BlockSpec((tm, d), lambda i: (i, 0)),
        out_shape=jax.ShapeDtypeStruct((t, d), F32),
        compiler_params=_cparams(("parallel",)),
        name="outproj_even",
    )(z, z, z, z, z, conv_w, o_b, x, w_out, w_out)


def _rep_rows(x, n):
    r, l = x.shape
    return jnp.broadcast_to(x[:, None, :], (r, n, l)).reshape(r * n, l)


def _tile_rows(x, n):
    r, l = x.shape
    return jnp.broadcast_to(x[None, :, :], (n, r, l)).reshape(n * r, l)


def _s5_kernel(n_chunks, u_ref, p_ref, y_ref, m_ref):
    lc = S5_CHUNK
    gc = S5_GROUP
    half = S5_STATE
    prm = p_ref[0]
    tab = [prm[i * gc:(i + 1) * gc, :] for i in range(8)]
    crd, cid, bqa, bqb, bwa, bwb, cga, cgb = tab
    alpha = prm[8 * gc:8 * gc + 1, :]
    beta = prm[8 * gc + 1:8 * gc + 2, :]

    def cpow(tt):
        mag = jnp.exp(alpha * tt)
        ang = beta * tt
        return mag * jnp.cos(ang), mag * jnp.sin(ang)

    t_col = lax.broadcasted_iota(jnp.int32, (lc, LANES), 0).astype(F32)
    lane = lax.broadcasted_iota(jnp.int32, (lc, LANES), 1)
    p0r, p0i = cpow(t_col)
    p1r, p1i = cpow(t_col + 1.0)
    prr, pri = cpow(float(lc - 1) - t_col)

    qcat = _rep_rows(crd, gc) * _tile_rows(bqa, gc) + _rep_rows(cid, gc) * _tile_rows(bqb, gc)
    p0cat = jnp.where(lane < half, p0r, p0i)
    p0cat2 = jnp.concatenate([p0cat, p0cat], axis=0)
    kc2 = lax.dot_general(qcat, p0cat2, (((1,), (1,)), ((), ())),
                          precision=lax.Precision.HIGHEST, preferred_element_type=F32)
    lane16 = lax.broadcasted_iota(jnp.int32, (gc, LANES), 1)
    k_row = jnp.concatenate(
        [jnp.where(lane16 < half, kc2[2 * j * gc:(2 * j + 1) * gc, :],
                   kc2[(2 * j + 1) * gc:(2 * j + 2) * gc, :]) for j in range(gc // 2)],
        axis=1)
    seg_pos = lax.broadcasted_iota(jnp.int32, k_row.shape, 1) % lc
    for s in range(lc):
        shifted = k_row if s == 0 else pltpu.roll(k_row, s, axis=1)
        m_ref[s * gc:(s + 1) * gc, :] = jnp.where(seg_pos >= s, shifted, 0.0).astype(BF16)

    u = u_ref[0]
    wcat = (_rep_rows(prr, gc) * _tile_rows(bwa, lc)
            + _rep_rows(pri, gc) * _tile_rows(bwb, lc)).astype(BF16)
    st = jnp.dot(u, wcat, preferred_element_type=F32)

    rows = st.shape[0]
    rpos = lax.broadcasted_iota(jnp.int32, (rows, LANES), 0) % n_chunks
    rlane = lax.broadcasted_iota(jnp.int32, (rows, LANES), 1)
    ar, ai = cpow(jnp.full((1, LANES), float(lc), F32))
    d = 1
    while d < n_chunks:
        sh = jnp.where(rpos >= d, pltpu.roll(st, d, axis=0), 0.0)
        a2 = jnp.where(rlane < half, -ai, ai)
        st = st + ar * sh + a2 * pltpu.roll(sh, half, axis=1)
        ar, ai = ar * ar - ai * ai, 2.0 * ar * ai
        d *= 2
    h0 = jnp.where(rpos >= 1, pltpu.roll(st, 1, axis=0), 0.0).astype(BF16)

    gcat = (_rep_rows(cga, lc) * _tile_rows(p1r, gc)
            + _rep_rows(cgb, lc) * _tile_rows(p1i, gc)).astype(BF16)
    y = jnp.dot(u, m_ref[...], preferred_element_type=F32)
    y = y + lax.dot_general(h0, gcat, (((1,), (1,)), ((), ())), preferred_element_type=F32)
    y_ref[0] = y.astype(y_ref.dtype)


def _s5_tables(lam_re, lam_im, log_dt, b_re, b_im, c_re, c_im):
    dt = jnp.exp(log_dt.astype(F32))[:, None]
    lr = lam_re.astype(F32)
    li = lam_im.astype(F32)
    mag = jnp.exp(lr * dt)
    ar = mag * jnp.cos(li * dt)
    ai = mag * jnp.sin(li * dt)
    den = lr * lr + li * li
    cr = ((ar - 1.0) * lr + ai * li) / den
    ci = (ai * lr - (ar - 1.0) * li) / den
    bbr = (cr[..., None] * b_re - ci[..., None] * b_im).transpose(0, 2, 1)
    bbi = (cr[..., None] * b_im + ci[..., None] * b_re).transpose(0, 2, 1)
    cre = c_re.astype(F32)
    cim = c_im.astype(F32)
    cat = lambda a, b: jnp.concatenate([a, b], axis=-1)
    alpha = (lr * dt)[:, None, :]
    beta = (li * dt)[:, None, :]
    pad = jnp.zeros((lr.shape[0], 6, 2 * lr.shape[1]), F32)
    return jnp.concatenate([
        cat(cre, cre), cat(cim, cim),
        cat(bbr, -bbi), cat(-bbi, -bbr),
        cat(bbr, bbi), cat(-bbi, bbr),
        cat(cre, -cim), cat(-cim, -cre),
        cat(alpha, alpha), cat(beta, beta), pad], axis=1)


def s5_mix(h, tables, bsz, seq):
    t, w = h.shape
    lc, gc = S5_CHUNK, S5_GROUP
    groups = w // gc
    n = seq // lc
    rows = bsz * n
    u = h.reshape(bsz, n, lc, groups, gc).transpose(3, 0, 1, 2, 4).reshape(groups, rows, lc * gc)
    y = pl.pallas_call(
        functools.partial(_s5_kernel, n),
        grid=(groups,),
        in_specs=[pl.BlockSpec((1, rows, lc * gc), lambda g: (g, 0, 0)),
                  pl.BlockSpec((1,) + tables.shape[1:], lambda g: (g, 0, 0))],
        out_specs=pl.BlockSpec((1, rows, lc * gc), lambda g: (g, 0, 0)),
        out_shape=jax.ShapeDtypeStruct((groups, rows, lc * gc), BF16),
        scratch_shapes=[pltpu.VMEM((lc * gc, lc * gc), BF16)],
        compiler_params=_cparams(("parallel",)),
        name="s5_mix",
    )(u, tables)
    return y.reshape(groups, bsz, n, gc, lc).transpose(1, 2, 4, 0, 3).reshape(t, w)


def _glu_kernel(y_ref, h_ref, d_ref, x_ref, wa_ref, wb_ref, o_ref, act_ref):
    @pl.when(pl.program_id(1) == 0)
    def _():
        y = y_ref[...].astype(F32) + d_ref[...] * h_ref[...].astype(F32)
        cdf = 0.5 * (1.0 + jnp.tanh(math.sqrt(2.0 / math.pi) * (y + 0.044715 * (y * y * y))))
        act_ref[...] = (y * cdf).astype(BF16)

    act = act_ref[...]
    za = jnp.dot(act, wa_ref[...], preferred_element_type=F32)
    zb = jnp.dot(act, wb_ref[...], preferred_element_type=F32)
    o_ref[...] = x_ref[...] + za * jax.nn.sigmoid(zb)


def glu_residual(y, h, d_skip, x, w_glu, tm=512, tn=512):
    t, d = x.shape
    w = y.shape[1]
    nb = d // tn
    return pl.pallas_call(
        _glu_kernel,
        grid=(t // tm, nb),
        in_specs=[pl.BlockSpec((tm, w), lambda i, j: (i, 0)),
                  pl.BlockSpec((tm, w), lambda i, j: (i, 0)),
                  pl.BlockSpec((1, w), lambda i, j: (0, 0)),
                  pl.BlockSpec((tm, tn), lambda i, j: (i, j)),
                  pl.BlockSpec((w, tn), lambda i, j: (0, j)),
                  pl.BlockSpec((w, tn), lambda i, j: (0, j + nb))],
        out_specs=pl.BlockSpec((tm, tn), lambda i, j: (i, j)),
        out_shape=jax.ShapeDtypeStruct((t, d), F32),
        scratch_shapes=[pltpu.VMEM((tm, w), BF16)],
        compiler_params=_cparams(("parallel", "arbitrary")),
        name="glu_residual",
    )(y, h, d_skip.reshape(1, w), x, w_glu, w_glu)


def _router_kernel(x_ref, g_ref, w_ref, b_ref, h_ref, e_ref, p_ref):
    h = _rms(x_ref[...], g_ref[...])
    h_ref[...] = h
    logits = lax.dot_general(w_ref[...], h, (((1,), (1,)), ((), ())),
                             precision=lax.Precision.HIGHEST,
                             preferred_element_type=F32) + b_ref[...]
    lc = [logits[i:i + 1, :] for i in range(MOE_GROUPS)]
    m = functools.reduce(jnp.maximum, lc)
    grp = jnp.full(m.shape, MOE_GROUPS - 1, jnp.int32)
    for i in range(MOE_GROUPS - 2, -1, -1):
        grp = jnp.where(lc[i] == m, i, grp)
    den = functools.reduce(lambda a, b: a + b, [jnp.exp(l - m) for l in lc])
    p_top = 1.0 / den
    fine = [logits[MOE_GROUPS + j:MOE_GROUPS + j + 1, :] for j in range(MOE_EXPERTS)]
    sel = []
    for e in range(MOE_EPG):
        v = fine[(MOE_GROUPS - 1) * MOE_EPG + e]
        for i in range(MOE_GROUPS - 2, -1, -1):
            v = jnp.where(grp == i, fine[i * MOE_EPG + e], v)
        sel.append(v)

    def top1(vals):
        best = functools.reduce(jnp.maximum, vals)
        idx = jnp.full(best.shape, MOE_EPG - 1, jnp.int32)
        for e in range(MOE_EPG - 2, -1, -1):
            idx = jnp.where(vals[e] == best, e, idx)
        return best, idx

    v1, i1 = top1(sel)
    v2, i2 = top1([jnp.where(i1 == e, -jnp.inf, sel[e]) for e in range(MOE_EPG)])
    ex = jnp.exp(v2 - v1)
    s = 1.0 + ex
    e_ref[0:1, :] = grp * MOE_EPG + i1
    e_ref[1:2, :] = grp * MOE_EPG + i2
    p_ref[0:1, :] = p_top * (1.0 / s)
    p_ref[1:2, :] = p_top * (ex / s)


def moe_router(x, g, w_coarse, b_coarse, w_fine, b_fine, tm=512):
    t, d = x.shape
    pad = ROUTER_ROWS - MOE_GROUPS - MOE_EXPERTS
    w = jnp.concatenate([w_coarse.T, w_fine.T, jnp.zeros((pad, d), F32)], axis=0)
    b = jnp.concatenate([b_coarse, b_fine, jnp.zeros((pad,), F32)]).reshape(ROUTER_ROWS, 1)
    return pl.pallas_call(
        _router_kernel,
        grid=(t // tm,),
        in_specs=[pl.BlockSpec((tm, d), lambda i: (i, 0)),
                  pl.BlockSpec((1, d), lambda i: (0, 0)),
                  pl.BlockSpec((ROUTER_ROWS, d), lambda i: (0, 0)),
                  pl.BlockSpec((ROUTER_ROWS, 1), lambda i: (0, 0))],
        out_specs=[pl.BlockSpec((tm, d), lambda i: (i, 0)),
                   pl.BlockSpec((2, tm), lambda i: (0, i)),
                   pl.BlockSpec((2, tm), lambda i: (0, i))],
        out_shape=[jax.ShapeDtypeStruct((t, d), F32),
                   jax.ShapeDtypeStruct((2, t), jnp.int32),
                   jax.ShapeDtypeStruct((2, t), F32)],
        compiler_params=_cparams(("parallel",)),
        name="moe_router",
    )(x, g.reshape(1, d), w, b)


def _dispatch_plan(experts, gates):
    k, t = experts.shape
    n_assign = k * t
    e_flat = experts.reshape(-1)
    tok = jnp.arange(n_assign, dtype=jnp.int32) % t
    onehot = (e_flat[:, None] == jnp.arange(MOE_EXPERTS, dtype=jnp.int32)).astype(jnp.int32)
    csum = jnp.cumsum(onehot, axis=0)
    rank = jnp.sum(csum * onehot, axis=1) - 1
    counts = csum[-1]
    padded = (counts + MOE_ROWS - 1) // MOE_ROWS * MOE_ROWS
    pend = jnp.cumsum(padded)
    pstart = pend - padded
    dest = (pstart[e_flat] + rank).astype(jnp.int32)
    n_groups = -(-n_assign // MOE_ROWS) + MOE_EXPERTS
    total = n_groups * MOE_ROWS
    buf_tok = jnp.zeros((total,), jnp.int32).at[dest].set(tok)
    buf_w = jnp.zeros((total,), F32).at[dest].set(gates.reshape(-1))
    grp_expert = jnp.minimum(
        jnp.searchsorted(pend, jnp.arange(n_groups, dtype=jnp.int32) * MOE_ROWS, side='right'),
        MOE_EXPERTS - 1).astype(jnp.int32)
    n_active = (pend[-1] // MOE_ROWS).astype(jnp.int32).reshape(1)
    return buf_tok, buf_w.reshape(total, 1), grp_expert, n_active, dest


def _row_gather_copy(src_hbm, idx_ref, base, r, dst, sem):
    tok = idx_ref[base + r]
    return pltpu.make_async_copy(src_hbm.at[pl.ds(tok, 1)], dst.at[pl.ds(r, 1)], sem)


def _experts_kernel(ge_ref, na_ref, tok_ref, h_hbm, bw_ref, wg_ref, wu_ref, wd_ref,
                    o_ref, xbuf, sem):
    g = pl.program_id(0)
    na = na_ref[0]
    rows = xbuf.shape[1]

    def issue(grp, slot):
        def body(r, c):
            _row_gather_copy(h_hbm, tok_ref, grp * rows, r, xbuf.at[slot], sem.at[slot]).start()
            return c
        lax.fori_loop(0, rows, body, 0, unroll=8)

    @pl.when(g == 0)
    def _():
        issue(0, 0)

    @pl.when(g + 1 < na)
    def _():
        issue(g + 1, (g + 1) % 2)

    @pl.when(g < na)
    def _():
        slot = g % 2
        pltpu.make_async_copy(h_hbm.at[pl.ds(0, rows)], xbuf.at[slot], sem.at[slot]).wait()
        xb = xbuf[slot].astype(BF16)
        a = jnp.dot(xb, wg_ref[0], preferred_element_type=F32)
        b = jnp.dot(xb, wu_ref[0], preferred_element_type=F32)
        hid = (a * jax.nn.sigmoid(a) * b).astype(BF16)
        y = jnp.dot(hid, wd_ref[0], preferred_element_type=F32)
        o_ref[...] = y * bw_ref[...]

    @pl.when(g >= na)
    def _():
        o_ref[...] = jnp.zeros_like(o_ref)


def moe_experts(h, buf_tok, buf_w, grp_expert, n_active, w_gate, w_up, w_down):
    t, d = h.shape
    ff = w_gate.shape[2]
    total = buf_tok.shape[0]
    n_groups = total // MOE_ROWS

    def blk(g, ge, na, tok):
        return jnp.minimum(g, na[0] - 1)

    grid_spec = pltpu.PrefetchScalarGridSpec(
        num_scalar_prefetch=3,
        grid=(n_groups,),
        in_specs=[pl.BlockSpec(memory_space=pl.ANY),
                  pl.BlockSpec((MOE_ROWS, 1), lambda g, ge, na, tok: (blk(g, ge, na, tok), 0)),
                  pl.BlockSpec((1, d, ff), lambda g, ge, na, tok: (ge[blk(g, ge, na, tok)], 0, 0)),
                  pl.BlockSpec((1, d, ff), lambda g, ge, na, tok: (ge[blk(g, ge, na, tok)], 0, 0)),
                  pl.BlockSpec((1, ff, d), lambda g, ge, na, tok: (ge[blk(g, ge, na, tok)], 0, 0))],
        out_specs=pl.BlockSpec((MOE_ROWS, d), lambda g, ge, na, tok: (g, 0)),
        scratch_shapes=[pltpu.VMEM((2, MOE_ROWS, d), F32),
                        pltpu.SemaphoreType.DMA((2,))],
    )
    return pl.pallas_call(
        _experts_kernel,
        grid_spec=grid_spec,
        out_shape=jax.ShapeDtypeStruct((total, d), F32),
        compiler_params=_cparams(("arbitrary",)),
        name="moe_experts",
    )(grp_expert, n_active, buf_tok, h, buf_w, w_gate, w_up, w_down)


def _combine_kernel(final_norm, pos_ref, yb_hbm, x_ref, g_ref, o_ref, ybuf, sem):
    i = pl.program_id(0)
    n = pl.num_programs(0)
    tm = x_ref.shape[0]
    t = n * tm

    def issue(tile, slot):
        def body(r, c):
            for k in range(2):
                _row_gather_copy(yb_hbm, pos_ref, k * t + tile * tm, r,
                                 ybuf.at[slot, k], sem.at[slot]).start()
            return c
        lax.fori_loop(0, tm, body, 0, unroll=8)

    @pl.when(i == 0)
    def _():
        issue(0, 0)

    @pl.when(i + 1 < n)
    def _():
        issue(i + 1, (i + 1) % 2)

    slot = i % 2
    for k in range(2):
        pltpu.make_async_copy(yb_hbm.at[pl.ds(0, tm)], ybuf.at[slot, k], sem.at[slot]).wait()
    out = x_ref[...] + (ybuf[slot, 0] + ybuf[slot, 1])
    if final_norm:
        out = _rms(out, g_ref[...])
    o_ref[...] = out


def moe_combine(yb, dest, x, g_final=None, tm=256):
    t, d = x.shape
    final_norm = g_final is not None
    g = (g_final if final_norm else jnp.ones((d,), F32)).reshape(1, d)
    grid_spec = pltpu.PrefetchScalarGridSpec(
        num_scalar_prefetch=1,
        grid=(t // tm,),
        in_specs=[pl.BlockSpec(memory_space=pl.ANY),
                  pl.BlockSpec((tm, d), lambda i, pos: (i, 0)),
                  pl.BlockSpec((1, d), lambda i, pos: (0, 0))],
        out_specs=pl.BlockSpec((tm, d), lambda i, pos: (i, 0)),
        scratch_shapes=[pltpu.VMEM((2, 2, tm, d), F32),
                        pltpu.SemaphoreType.DMA((2,))],
    )
    return pl.pallas_call(
        functools.partial(_combine_kernel, final_norm),
        grid_spec=grid_spec,
        out_shape=jax.ShapeDtypeStruct((t, d), F32),
        compiler_params=_cparams(("arbitrary",)),
        name="moe_combine",
    )(dest, yb, x, g)


def hier_moe_residual(x, g_ffn, w_coarse, b_coarse, w_fine, b_fine, w_gate, w_up, w_down,
                      g_final=None):
    h, experts, gates = moe_router(x, g_ffn, w_coarse, b_coarse, w_fine, b_fine)
    buf_tok, buf_w, grp_expert, n_active, dest = _dispatch_plan(experts, gates)
    yb = moe_experts(h, buf_tok, buf_w, grp_expert, n_active,
                     w_gate.astype(BF16), w_up.astype(BF16), w_down.astype(BF16))
    return moe_combine(yb, dest, x, g_final)


def kernel(x, g_mix, g_ffn, g_final, w_in_even, conv_w, hg_lower_bound, hg_norm_g, w_out_even, s5_lambda_re, s5_lambda_im, s5_log_dt, s5_b_re, s5_b_im, s5_c_re, s5_c_im, s5_d, w_glu, moe_w_coarse, moe_b_coarse, moe_w_fine, moe_b_fine, moe_w_gate, moe_w_up, moe_w_down):
    bsz, seq, d = x.shape
    depth = g_mix.shape[0]
    conv_width = conv_w.shape[2]
    hg_width = hg_lower_bound.shape[1]
    lb_all = jnp.cumsum(jax.nn.softmax(hg_lower_bound.astype(F32), axis=0), axis=0)
    xf = x.reshape(bsz * seq, d)
    for layer in range(depth):
        j = layer // 2
        if layer % 2 == 0:
            z = norm_matmul(xf, g_mix[layer], w_in_even[j].astype(BF16))
            o_b = hgrn2(z, lb_all[j], hg_norm_g[j], bsz, seq, hg_width, 3 * conv_width)
            xf = outproj_even(z, conv_w[j], o_b, xf, w_out_even[j].astype(BF16), seq)
        else:
            h = rmsnorm_cast(xf, g_mix[layer])
            tables = _s5_tables(s5_lambda_re[j], s5_lambda_im[j], s5_log_dt[j],
                                s5_b_re[j], s5_b_im[j], s5_c_re[j], s5_c_im[j])
            y = s5_mix(h, tables, bsz, seq)
            xf = glu_residual(y, h, s5_d[j], xf, w_glu[j].astype(BF16))
        xf = hier_moe_residual(
            xf, g_ffn[layer], moe_w_coarse[layer], moe_b_coarse[layer], moe_w_fine[layer],
            moe_b_fine[layer], moe_w_gate[layer], moe_w_up[layer], moe_w_down[layer],
            g_final if layer == depth - 1 else None)
    return xf.reshape(bsz, seq, d)
```

```python
import functools
import math

import jax
import jax.numpy as jnp
from jax import lax
from jax.experimental import pallas as pl
from jax.experimental.pallas import tpu as pltpu

F32 = jnp.float32
BF16 = jnp.bfloat16
EPS = 1e-6

LANES = 128
HG_HEAD_DIM = 128
HG_CHUNK = 64
CONV_K = 3
S5_GROUP = 16
S5_STATE = 64
S5_CHUNK = 64
MOE_GROUPS = 4
MOE_EPG = 8
MOE_EXPERTS = MOE_GROUPS * MOE_EPG
MOE_ROWS = 256
ROUTER_ROWS = 40
VMEM_LIMIT = 56 * 1024 * 1024


def _cparams(sem, vmem=VMEM_LIMIT):
    return pltpu.CompilerParams(dimension_semantics=sem, vmem_limit_bytes=vmem)


def _rms(x, g):
    ms = jnp.mean(x * x, axis=-1, keepdims=True)
    return x * lax.rsqrt(ms + EPS) * g


def _norm_kernel(x_ref, g_ref, o_ref):
    o_ref[...] = _rms(x_ref[...], g_ref[...]).astype(o_ref.dtype)


def rmsnorm_cast(x, g, tm=512):
    t, d = x.shape
    return pl.pallas_call(
        _norm_kernel,
        grid=(t // tm,),
        in_specs=[pl.BlockSpec((tm, d), lambda i: (i, 0)),
                  pl.BlockSpec((1, d), lambda i: (0, 0))],
        out_specs=pl.BlockSpec((tm, d), lambda i: (i, 0)),
        out_shape=jax.ShapeDtypeStruct((t, d), BF16),
        compiler_params=_cparams(("parallel",)),
        name="rmsnorm_cast",
    )(x, g.reshape(1, d))


def _norm_matmul_kernel(x_ref, g_ref, w_ref, o_ref, h_ref):
    @pl.when(pl.program_id(1) == 0)
    def _():
        h_ref[...] = _rms(x_ref[...], g_ref[...]).astype(BF16)

    o_ref[...] = jnp.dot(h_ref[...], w_ref[...],
                         preferred_element_type=F32).astype(o_ref.dtype)


def norm_matmul(x, g, w, tm=512, tn=1024):
    t, d = x.shape
    n = w.shape[1]
    return pl.pallas_call(
        _norm_matmul_kernel,
        grid=(t // tm, n // tn),
        in_specs=[pl.BlockSpec((tm, d), lambda i, j: (i, 0)),
                  pl.BlockSpec((1, d), lambda i, j: (0, 0)),
                  pl.BlockSpec((d, tn), lambda i, j: (0, j))],
        out_specs=pl.BlockSpec((tm, tn), lambda i, j: (i, j)),
        out_shape=jax.ShapeDtypeStruct((t, n), BF16),
        scratch_shapes=[pltpu.VMEM((tm, d), BF16)],
        compiler_params=_cparams(("parallel", "arbitrary")),
        name="norm_matmul",
    )(x, g.reshape(1, d), w)


def _split3(x):
    hi = x.astype(BF16)
    r1 = x - hi.astype(F32)
    mid = r1.astype(BF16)
    lo = (r1 - mid.astype(F32)).astype(BF16)
    return hi, mid, lo


def _hgrn2_kernel(q_ref, f_ref, v_ref, g_ref, lb_ref, ng_ref, o_ref, st_ref):
    c = HG_CHUNK
    hd = HG_HEAD_DIM
    heads = st_ref.shape[0]
    n_chunks = q_ref.shape[0] // c
    ng = ng_ref[...]
    row = lax.broadcasted_iota(jnp.int32, (c, c), 0)
    col = lax.broadcasted_iota(jnp.int32, (c, c), 1)
    causal = row >= col
    tril = causal.astype(BF16)
    mid = c // 2

    @pl.when(pl.program_id(1) == 0)
    def _():
        st_ref[...] = jnp.zeros_like(st_ref)

    nt_dims = (((1,), (1,)), ((), ()))
    hs = range(heads)
    cols = [slice(h * hd, (h + 1) * hd) for h in hs]

    def step(n, carry):
        rows = pl.ds(pl.multiple_of(n * c, c), c)
        q = [q_ref[rows, cl].astype(F32) for cl in cols]
        f = [lb_ref[:, cl] + (1.0 - lb_ref[:, cl]) * jax.nn.sigmoid(f_ref[rows, cl].astype(F32))
             for cl in cols]
        k = [1.0 - fh for fh in f]
        parts = [_split3(jnp.log(fh)) for fh in f]
        b = [sum(jnp.dot(tril, p, preferred_element_type=F32) for p in ph) for ph in parts]
        ref = [bh[mid - 1:mid, :] for bh in b]
        b_last = [bh[c - 1:c, :] for bh in b]
        st = [st_ref[h] for h in hs]
        qs = [(q[h] * jnp.exp(b[h] - ref[h])).astype(BF16) for h in hs]
        ks = [(k[h] * jnp.exp(ref[h] - b[h])).astype(BF16) for h in hs]
        scores = [lax.dot_general(qs[h], ks[h], nt_dims, preferred_element_type=F32) for h in hs]
        qe = [(q[h] * jnp.exp(b[h])).astype(BF16) for h in hs]
        o_inter = [lax.dot_general(qe[h], st[h].astype(BF16), nt_dims, preferred_element_type=F32)
                   for h in hs]
        kd = [(k[h] * jnp.exp(b_last[h] - b[h])).astype(BF16) for h in hs]
        upd = [lax.dot_general(v_ref[rows, cols[h]], kd[h], (((0,), (0,)), ((), ())),
                               preferred_element_type=F32) for h in hs]
        for h in hs:
            st_ref[h] = st[h] * jnp.exp(b_last[h]) + upd[h]
        sc = [jnp.where(causal, s, 0.0).astype(BF16) for s in scores]
        o = [o_inter[h] + jnp.dot(sc[h], v_ref[rows, cols[h]], preferred_element_type=F32)
             for h in hs]
        for h in hs:
            oh = o[h] * lax.rsqrt(jnp.mean(o[h] * o[h], axis=-1, keepdims=True) + EPS) * ng
            gate = g_ref[rows, cols[h]].astype(F32)
            o_ref[rows, cols[h]] = (oh * (gate * jax.nn.sigmoid(gate))).astype(o_ref.dtype)
        return carry

    lax.fori_loop(0, n_chunks, step, 0)


def hgrn2(z, lb, norm_g, bsz, seq, width, col0, ts=1024):
    hd = HG_HEAD_DIM
    heads = width // hd
    cb = col0 // width
    ts = min(ts, seq)
    nt = seq // ts

    def zspec(k):
        return pl.BlockSpec((ts, width), lambda b, s: (b * nt + s, cb + k))

    return pl.pallas_call(
        _hgrn2_kernel,
        grid=(bsz, nt),
        in_specs=[zspec(0), zspec(1), zspec(2), zspec(3),
                  pl.BlockSpec((1, width), lambda b, s: (0, 0)),
                  pl.BlockSpec((1, hd), lambda b, s: (0, 0))],
        out_specs=pl.BlockSpec((ts, width), lambda b, s: (b * nt + s, 0)),
        out_shape=jax.ShapeDtypeStruct((bsz * seq, width), BF16),
        scratch_shapes=[pltpu.VMEM((heads, hd, hd), F32)],
        compiler_params=_cparams(("parallel", "arbitrary")),
        name="hgrn2",
    )(z, z, z, z, lb.reshape(1, width), norm_g.reshape(1, hd))


def _outproj_kernel(seq_tiles, ab_ref, ac_ref, ah_ref, hc_ref, hh_ref, cw_ref,
                    ob_ref, x_ref, wa_ref, wb_ref, o_ref):
    i = pl.program_id(0)
    u = ac_ref[...].astype(F32) * ah_ref[...].astype(F32)
    halo = hc_ref[...].astype(F32) * hh_ref[...].astype(F32)
    halo = jnp.where(i % seq_tiles == 0, 0.0, halo)
    hr = halo.shape[0]
    row = lax.broadcasted_iota(jnp.int32, u.shape, 0)
    u1 = jnp.where(row == 0, halo[hr - 1:hr, :], pltpu.roll(u, 1, axis=0))
    u2 = jnp.where(row == 0, halo[hr - 2:hr - 1, :],
                   jnp.where(row == 1, halo[hr - 1:hr, :], pltpu.roll(u, 2, axis=0)))
    cw = cw_ref[...]
    conv = cw[2:3, :] * u + cw[1:2, :] * u1 + cw[0:1, :] * u2
    ya = (ab_ref[...].astype(F32) * conv).astype(BF16)
    mix = jnp.dot(ya, wa_ref[...], preferred_element_type=F32)
    mix = mix + jnp.dot(ob_ref[...], wb_ref[...], preferred_element_type=F32)
    o_ref[...] = x_ref[...] + mix


def outproj_even(z, conv_w, o_b, x, w_out, seq, tm=256, halo=16):
    t, d = x.shape
    cw = conv_w.shape[1]
    hw = o_b.shape[1]
    hb = tm // halo
    kern = functools.partial(_outproj_kernel, seq // tm)
    return pl.pallas_call(
        kern,
        grid=(t // tm,),
        in_specs=[pl.BlockSpec((tm, cw), lambda i: (i, 0)),
                  pl.BlockSpec((tm, cw), lambda i: (i, 1)),
                  pl.BlockSpec((tm, cw), lambda i: (i, 2)),
                  pl.BlockSpec((halo, cw), lambda i: (jnp.maximum(i * hb - 1, 0), 1)),
                  pl.BlockSpec((halo, cw), lambda i: (jnp.maximum(i * hb - 1, 0), 2)),
                  pl.BlockSpec((CONV_K, cw), lambda i: (0, 0)),
                  pl.BlockSpec((tm, hw), lambda i: (i, 0)),
                  pl.BlockSpec((tm, d), lambda i: (i, 0)),
                  pl.BlockSpec((cw, d), lambda i: (0, 0)),
                  pl.BlockSpec((hw, d), lambda i: (1, 0))],
        out_specs=pl.BlockSpec((tm, d), lambda i: (i, 0)),
        out_shape=jax.ShapeDtypeStruct((t, d), F32),
        compiler_params=_cparams(("parallel",)),
        name="outproj_even",
    )(z, z, z, z, z, conv_w, o_b, x, w_out, w_out)


def _rep_rows(x, n):
    r, l = x.shape
    return jnp.broadcast_to(x[:, None, :], (r, n, l)).reshape(r * n, l)


def _tile_rows(x, n):
    r, l = x.shape
    return jnp.broadcast_to(x[None, :, :], (n, r, l)).reshape(n * r, l)


def _s5_kernel(n_chunks, u_ref, p_ref, y_ref, m_ref):
    lc = S5_CHUNK
    gc = S5_GROUP
    half = S5_STATE
    prm = p_ref[0]
    tab = [prm[i * gc:(i + 1) * gc, :] for i in range(8)]
    crd, cid, bqa, bqb, bwa, bwb, cga, cgb = tab
    alpha = prm[8 * gc:8 * gc + 1, :]
    beta = prm[8 * gc + 1:8 * gc + 2, :]

    def cpow(tt):
        mag = jnp.exp(alpha * tt)
        ang = beta * tt
        return mag * jnp.cos(ang), mag * jnp.sin(ang)

    t_col = lax.broadcasted_iota(jnp.int32, (lc, LANES), 0).astype(F32)
    lane = lax.broadcasted_iota(jnp.int32, (lc, LANES), 1)
    p0r, p0i = cpow(t_col)
    p1r, p1i = cpow(t_col + 1.0)
    prr, pri = cpow(float(lc - 1) - t_col)

    qcat = _rep_rows(crd, gc) * _tile_rows(bqa, gc) + _rep_rows(cid, gc) * _tile_rows(bqb, gc)
    p0cat = jnp.where(lane < half, p0r, p0i)
    p0cat2 = jnp.concatenate([p0cat, p0cat], axis=0)
    kc2 = lax.dot_general(qcat, p0cat2, (((1,), (1,)), ((), ())),
                          precision=lax.Precision.HIGHEST, preferred_element_type=F32)
    lane16 = lax.broadcasted_iota(jnp.int32, (gc, LANES), 1)
    k_row = jnp.concatenate(
        [jnp.where(lane16 < half, kc2[2 * j * gc:(2 * j + 1) * gc, :],
                   kc2[(2 * j + 1) * gc:(2 * j + 2) * gc, :]) for j in range(gc // 2)],
        axis=1)
    seg_pos = lax.broadcasted_iota(jnp.int32, k_row.shape, 1) % lc
    for s in range(lc):
        shifted = k_row if s == 0 else pltpu.roll(k_row, s, axis=1)
        m_ref[s * gc:(s + 1) * gc, :] = jnp.where(seg_pos >= s, shifted, 0.0).astype(BF16)

    u = u_ref[0]
    wcat = (_rep_rows(prr, gc) * _tile_rows(bwa, lc)
            + _rep_rows(pri, gc) * _tile_rows(bwb, lc)).astype(BF16)
    st = jnp.dot(u, wcat, preferred_element_type=F32)

    rows = st.shape[0]
    rpos = lax.broadcasted_iota(jnp.int32, (rows, LANES), 0) % n_chunks
    rlane = lax.broadcasted_iota(jnp.int32, (rows, LANES), 1)
    ar, ai = cpow(jnp.full((1, LANES), float(lc), F32))
    d = 1
    while d < n_chunks:
        sh = jnp.where(rpos >= d, pltpu.roll(st, d, axis=0), 0.0)
        a2 = jnp.where(rlane < half, -ai, ai)
        st = st + ar * sh + a2 * pltpu.roll(sh, half, axis=1)
        ar, ai = ar * ar - ai * ai, 2.0 * ar * ai
        d *= 2
    h0 = jnp.where(rpos >= 1, pltpu.roll(st, 1, axis=0), 0.0).astype(BF16)

    gcat = (_rep_rows(cga, lc) * _tile_rows(p1r, gc)
            + _rep_rows(cgb, lc) * _tile_rows(p1i, gc)).astype(BF16)
    y = jnp.dot(u, m_ref[...], preferred_element_type=F32)
    y = y + lax.dot_general(h0, gcat, (((1,), (1,)), ((), ())), preferred_element_type=F32)
    y_ref[0] = y.astype(y_ref.dtype)


def _s5_tables(lam_re, lam_im, log_dt, b_re, b_im, c_re, c_im):
    dt = jnp.exp(log_dt.astype(F32))[:, None]
    lr = lam_re.astype(F32)
    li = lam_im.astype(F32)
    mag = jnp.exp(lr * dt)
    ar = mag * jnp.cos(li * dt)
    ai = mag * jnp.sin(li * dt)
    den = lr * lr + li * li
    cr = ((ar - 1.0) * lr + ai * li) / den
    ci = (ai * lr - (ar - 1.0) * li) / den
    bbr = (cr[..., None] * b_re - ci[..., None] * b_im).transpose(0, 2, 1)
    bbi = (cr[..., None] * b_im + ci[..., None] * b_re).transpose(0, 2, 1)
    cre = c_re.astype(F32)
    cim = c_im.astype(F32)
    cat = lambda a, b: jnp.concatenate([a, b], axis=-1)
    alpha = (lr * dt)[:, None, :]
    beta = (li * dt)[:, None, :]
    pad = jnp.zeros((lr.shape[0], 6, 2 * lr.shape[1]), F32)
    return jnp.concatenate([
        cat(cre, cre), cat(cim, cim),
        cat(bbr, -bbi), cat(-bbi, -bbr),
        cat(bbr, bbi), cat(-bbi, bbr),
        cat(cre, -cim), cat(-cim, -cre),
        cat(alpha, alpha), cat(beta, beta), pad], axis=1)


def s5_mix(h, tables, bsz, seq):
    t, w = h.shape
    lc, gc = S5_CHUNK, S5_GROUP
    groups = w // gc
    n = seq // lc
    rows = bsz * n
    u = h.reshape(bsz, n, lc, groups, gc).transpose(3, 0, 1, 2, 4).reshape(groups, rows, lc * gc)
    y = pl.pallas_call(
        functools.partial(_s5_kernel, n),
        grid=(groups,),
        in_specs=[pl.BlockSpec((1, rows, lc * gc), lambda g: (g, 0, 0)),
                  pl.BlockSpec((1,) + tables.shape[1:], lambda g: (g, 0, 0))],
        out_specs=pl.BlockSpec((1, rows, lc * gc), lambda g: (g, 0, 0)),
        out_shape=jax.ShapeDtypeStruct((groups, rows, lc * gc), BF16),
        scratch_shapes=[pltpu.VMEM((lc * gc, lc * gc), BF16)],
        compiler_params=_cparams(("parallel",)),
        name="s5_mix",
    )(u, tables)
    return y.reshape(groups, bsz, n, gc, lc).transpose(1, 2, 4, 0, 3).reshape(t, w)


def _glu_kernel(y_ref, h_ref, d_ref, x_ref, wa_ref, wb_ref, o_ref, act_ref):
    @pl.when(pl.program_id(1) == 0)
    def _():
        y = y_ref[...].astype(F32) + d_ref[...] * h_ref[...].astype(F32)
        cdf = 0.5 * (1.0 + jnp.tanh(math.sqrt(2.0 / math.pi) * (y + 0.044715 * (y * y * y))))
        act_ref[...] = (y * cdf).astype(BF16)

    act = act_ref[...]
    za = jnp.dot(act, wa_ref[...], preferred_element_type=F32)
    zb = jnp.dot(act, wb_ref[...], preferred_element_type=F32)
    o_ref[...] = x_ref[...] + za * jax.nn.sigmoid(zb)


def glu_residual(y, h, d_skip, x, w_glu, tm=512, tn=512):
    t, d = x.shape
    w = y.shape[1]
    nb = d // tn
    return pl.pallas_call(
        _glu_kernel,
        grid=(t // tm, nb),
        in_specs=[pl.BlockSpec((tm, w), lambda i, j: (i, 0)),
                  pl.BlockSpec((tm, w), lambda i, j: (i, 0)),
                  pl.BlockSpec((1, w), lambda i, j: (0, 0)),
                  pl.BlockSpec((tm, tn), lambda i, j: (i, j)),
                  pl.BlockSpec((w, tn), lambda i, j: (0, j)),
                  pl.BlockSpec((w, tn), lambda i, j: (0, j + nb))],
        out_specs=pl.BlockSpec((tm, tn), lambda i, j: (i, j)),
        out_shape=jax.ShapeDtypeStruct((t, d), F32),
        scratch_shapes=[pltpu.VMEM((tm, w), BF16)],
        compiler_params=_cparams(("parallel", "arbitrary")),
        name="glu_residual",
    )(y, h, d_skip.reshape(1, w), x, w_glu, w_glu)


def _router_kernel(x_ref, g_ref, w_ref, b_ref, e_ref, p_ref, r_ref, c_ref, cnt_ref):
    tm = x_ref.shape[0]

    @pl.when(pl.program_id(0) == 0)
    def _():
        cnt_ref[...] = jnp.zeros_like(cnt_ref)

    h = _rms(x_ref[...], g_ref[...])
    logits = lax.dot_general(w_ref[...], h, (((1,), (1,)), ((), ())),
                             precision=lax.Precision.HIGHEST,
                             preferred_element_type=F32) + b_ref[...]
    lc = [logits[i:i + 1, :] for i in range(MOE_GROUPS)]
    m = functools.reduce(jnp.maximum, lc)
    grp = jnp.full(m.shape, MOE_GROUPS - 1, jnp.int32)
    for i in range(MOE_GROUPS - 2, -1, -1):
        grp = jnp.where(lc[i] == m, i, grp)
    den = functools.reduce(lambda a, b: a + b, [jnp.exp(l - m) for l in lc])
    p_top = 1.0 / den
    fine = [logits[MOE_GROUPS + j:MOE_GROUPS + j + 1, :] for j in range(MOE_EXPERTS)]
    sel = []
    for e in range(MOE_EPG):
        v = fine[(MOE_GROUPS - 1) * MOE_EPG + e]
        for i in range(MOE_GROUPS - 2, -1, -1):
            v = jnp.where(grp == i, fine[i * MOE_EPG + e], v)
        sel.append(v)

    def top1(vals):
        best = functools.reduce(jnp.maximum, vals)
        idx = jnp.full(best.shape, MOE_EPG - 1, jnp.int32)
        for e in range(MOE_EPG - 2, -1, -1):
            idx = jnp.where(vals[e] == best, e, idx)
        return best, idx

    v1, i1 = top1(sel)
    v2, i2 = top1([jnp.where(i1 == e, -jnp.inf, sel[e]) for e in range(MOE_EPG)])
    ex = jnp.exp(v2 - v1)
    s = 1.0 + ex
    e0 = grp * MOE_EPG + i1
    e1 = grp * MOE_EPG + i2
    e_ref[0:1, :] = e0
    e_ref[1:2, :] = e1
    p_ref[0:1, :] = p_top * (1.0 / s)
    p_ref[1:2, :] = p_top * (ex / s)

    eid = lax.broadcasted_iota(jnp.int32, (MOE_EXPERTS, tm), 0)
    tri = (lax.broadcasted_iota(jnp.int32, (tm, tm), 0)
           <= lax.broadcasted_iota(jnp.int32, (tm, tm), 1)).astype(BF16)
    base = cnt_ref[...]
    for k, ek in enumerate((e0, e1)):
        hot = eid == ek
        hot_f = hot.astype(F32)
        csum = jnp.dot(hot.astype(BF16), tri, preferred_element_type=F32)
        rank = jnp.sum(hot_f * (base + csum - 1.0), axis=0, keepdims=True)
        r_ref[k:k + 1, :] = rank.astype(jnp.int32)
        base = base + jnp.sum(hot_f, axis=1, keepdims=True)
    cnt_ref[...] = base
    c_ref[...] = base.astype(jnp.int32)


def moe_router(x, g, w_coarse, b_coarse, w_fine, b_fine, tm=512):
    t, d = x.shape
    pad = ROUTER_ROWS - MOE_GROUPS - MOE_EXPERTS
    w = jnp.concatenate([w_coarse.T, w_fine.T, jnp.zeros((pad, d), F32)], axis=0)
    b = jnp.concatenate([b_coarse, b_fine, jnp.zeros((pad,), F32)]).reshape(ROUTER_ROWS, 1)
    tok_spec = pl.BlockSpec((2, tm), lambda i: (0, i))
    return pl.pallas_call(
        _router_kernel,
        grid=(t // tm,),
        in_specs=[pl.BlockSpec((tm, d), lambda i: (i, 0)),
                  pl.BlockSpec((1, d), lambda i: (0, 0)),
                  pl.BlockSpec((ROUTER_ROWS, d), lambda i: (0, 0)),
                  pl.BlockSpec((ROUTER_ROWS, 1), lambda i: (0, 0))],
        out_specs=[tok_spec, tok_spec, tok_spec,
                   pl.BlockSpec((MOE_EXPERTS, 1), lambda i: (0, 0))],
        out_shape=[jax.ShapeDtypeStruct((2, t), jnp.int32),
                   jax.ShapeDtypeStruct((2, t), F32),
                   jax.ShapeDtypeStruct((2, t), jnp.int32),
                   jax.ShapeDtypeStruct((MOE_EXPERTS, 1), jnp.int32)],
        scratch_shapes=[pltpu.VMEM((MOE_EXPERTS, 1), F32)],
        compiler_params=_cparams(("arbitrary",)),
        name="moe_router",
    )(x, g.reshape(1, d), w, b)


def _dispatch_plan(experts, rank, counts):
    n_assign = experts.size
    counts = counts.reshape(-1)
    padded = (counts + MOE_ROWS - 1) // MOE_ROWS * MOE_ROWS
    pend = jnp.cumsum(padded)
    pstart = pend - padded
    eids = jnp.arange(MOE_EXPERTS, dtype=jnp.int32)
    dest = rank + jnp.sum(jnp.where(experts[..., None] == eids, pstart, 0), axis=-1)
    n_groups = -(-n_assign // MOE_ROWS) + MOE_EXPERTS
    g0 = jnp.arange(n_groups, dtype=jnp.int32) * MOE_ROWS
    grp_expert = jnp.minimum(jnp.sum(pend[None, :] <= g0[:, None], axis=1), MOE_EXPERTS - 1)
    n_active = (pend[-1] // MOE_ROWS).reshape(1)
    i32 = lambda a: a.astype(jnp.int32)
    return (i32(dest).reshape(-1), i32(grp_expert), i32(n_active),
            i32(pstart + counts), i32(padded - counts))


def _dispatch_kernel(dest_ref, ps_ref, pn_ref, na_ref, x_ref, g_ref, xs_hbm, hbuf, zbuf, sem,
                     zsem):
    i = pl.program_id(0)
    n = pl.num_programs(0)
    tm = x_ref.shape[0]
    t = n * tm
    slot = i % 2

    def wait_slot(s):
        for _ in range(2):
            pltpu.make_async_copy(hbuf.at[s], xs_hbm.at[pl.ds(0, tm)], sem.at[s]).wait()

    def zero_row_copy(row):
        return pltpu.make_async_copy(zbuf.at[pl.ds(0, 1)], xs_hbm.at[pl.ds(row, 1)], zsem.at[0])

    def for_each_pad_row(fn):
        def per_expert(e, c):
            def per_row(r, c2):
                fn(ps_ref[e] + r)
                return c2
            return lax.fori_loop(0, pn_ref[e], per_row, c)
        lax.fori_loop(0, ps_ref.shape[0], per_expert, 0)

    def tail_group_copy(grp):
        rows = pl.ds(pl.multiple_of(grp * tm, tm), tm)
        return pltpu.make_async_copy(hbuf.at[1], xs_hbm.at[rows], sem.at[1])

    def for_each_tail_group(fn):
        def body(grp, c):
            fn(grp)
            return c
        lax.fori_loop(na_ref[0], xs_hbm.shape[0] // tm, body, 0)

    @pl.when(i == 0)
    def _():
        zbuf[...] = jnp.zeros_like(zbuf)
        for_each_pad_row(lambda row: zero_row_copy(row).start())
        hbuf[1] = jnp.zeros(hbuf.shape[1:], hbuf.dtype)
        for_each_tail_group(lambda grp: tail_group_copy(grp).start())
        for_each_tail_group(lambda grp: tail_group_copy(grp).wait())

    @pl.when(i >= 2)
    def _():
        wait_slot(slot)

    hbuf[slot] = _rms(x_ref[...], g_ref[...])

    def body(r, c):
        for k in range(2):
            row = dest_ref[k * t + i * tm + r]
            pltpu.make_async_copy(hbuf.at[slot, pl.ds(r, 1)], xs_hbm.at[pl.ds(row, 1)],
                                  sem.at[slot]).start()
        return c
    lax.fori_loop(0, tm, body, 0, unroll=8)

    @pl.when(i == n - 1)
    def _():
        wait_slot(slot)

        @pl.when(n >= 2)
        def _():
            wait_slot(1 - slot)

        for_each_pad_row(lambda row: zero_row_copy(row).wait())


def moe_dispatch(x, g, dest, pad_start, pad_len, n_active, total):
    t, d = x.shape
    tm = MOE_ROWS
    grid_spec = pltpu.PrefetchScalarGridSpec(
        num_scalar_prefetch=4,
        grid=(t // tm,),
        in_specs=[pl.BlockSpec((tm, d), lambda i, *_: (i, 0)),
                  pl.BlockSpec((1, d), lambda i, *_: (0, 0))],
        out_specs=pl.BlockSpec(memory_space=pl.ANY),
        scratch_shapes=[pltpu.VMEM((2, tm, d), F32),
                        pltpu.VMEM((8, d), F32),
                        pltpu.SemaphoreType.DMA((2,)),
                        pltpu.SemaphoreType.DMA((1,))],
    )
    return pl.pallas_call(
        _dispatch_kernel,
        grid_spec=grid_spec,
        out_shape=jax.ShapeDtypeStruct((total, d), F32),
        compiler_params=_cparams(("arbitrary",)),
        name="moe_dispatch",
    )(dest, pad_start, pad_len, n_active, x, g.reshape(1, d))


def _experts_kernel(ge_ref, na_ref, xs_ref, wg_ref, wu_ref, wd_ref, o_ref):
    g = pl.program_id(0)

    @pl.when(g < na_ref[0])
    def _():
        xb = xs_ref[...].astype(BF16)
        a = jnp.dot(xb, wg_ref[...], preferred_element_type=F32)
        b = jnp.dot(xb, wu_ref[...], preferred_element_type=F32)
        hid = (a * jax.nn.sigmoid(a) * b).astype(BF16)
        o_ref[...] = jnp.dot(hid, wd_ref[...], preferred_element_type=F32)

    @pl.when(g >= na_ref[0])
    def _():
        o_ref[...] = jnp.zeros_like(o_ref)


def moe_experts(xs, grp_expert, n_active, w_gate, w_up, w_down, layer):
    total, d = xs.shape
    ff = w_gate.shape[3]
    n_groups = total // MOE_ROWS

    def blk(g, na):
        return jnp.minimum(g, na[0] - 1)

    def wspec(r, c):
        return pl.BlockSpec((None, None, r, c),
                            lambda g, ge, na: (layer, ge[blk(g, na)], 0, 0))

    grid_spec = pltpu.PrefetchScalarGridSpec(
        num_scalar_prefetch=2,
        grid=(n_groups,),
        in_specs=[pl.BlockSpec((MOE_ROWS, d), lambda g, ge, na: (blk(g, na), 0)),
                  wspec(d, ff), wspec(d, ff), wspec(ff, d)],
        out_specs=pl.BlockSpec((MOE_ROWS, d), lambda g, ge, na: (g, 0)),
    )
    return pl.pallas_call(
        _experts_kernel,
        grid_spec=grid_spec,
        out_shape=jax.ShapeDtypeStruct((total, d), F32),
        compiler_params=_cparams(("arbitrary",)),
        name="moe_experts",
    )(grp_expert, n_active, xs, w_gate, w_up, w_down)


def _combine_kernel(final_norm, pos_ref, yb_hbm, x_ref, p_ref, g_ref, o_ref, ybuf, sem):
    i = pl.program_id(0)
    n = pl.num_programs(0)
    tm = x_ref.shape[0]
    t = n * tm

    def issue(tile, slot):
        def body(r, c):
            for k in range(2):
                row = pos_ref[k * t + tile * tm + r]
                pltpu.make_async_copy(yb_hbm.at[pl.ds(row, 1)], ybuf.at[slot, k, pl.ds(r, 1)],
                                      sem.at[slot]).start()
            return c
        lax.fori_loop(0, tm, body, 0, unroll=8)

    @pl.when(i == 0)
    def _():
        issue(0, 0)

    @pl.when(i + 1 < n)
    def _():
        issue(i + 1, (i + 1) % 2)

    slot = i % 2
    for k in range(2):
        pltpu.make_async_copy(yb_hbm.at[pl.ds(0, tm)], ybuf.at[slot, k], sem.at[slot]).wait()
    p = p_ref[...]
    out = x_ref[...] + (p[:, 0:1] * ybuf[slot, 0] + p[:, 1:2] * ybuf[slot, 1])
    if final_norm:
        out = _rms(out, g_ref[...])
    o_ref[...] = out


def moe_combine(yb, dest, gates_t, x, g_final=None, tm=256):
    t, d = x.shape
    final_norm = g_final is not None
    g = (g_final if final_norm else jnp.ones((d,), F32)).reshape(1, d)
    grid_spec = pltpu.PrefetchScalarGridSpec(
        num_scalar_prefetch=1,
        grid=(t // tm,),
        in_specs=[pl.BlockSpec(memory_space=pl.ANY),
                  pl.BlockSpec((tm, d), lambda i, pos: (i, 0)),
                  pl.BlockSpec((tm, 2), lambda i, pos: (i, 0)),
                  pl.BlockSpec((1, d), lambda i, pos: (0, 0))],
        out_specs=pl.BlockSpec((tm, d), lambda i, pos: (i, 0)),
        scratch_shapes=[pltpu.VMEM((2, 2, tm, d), F32),
                        pltpu.SemaphoreType.DMA((2,))],
    )
    return pl.pallas_call(
        functools.partial(_combine_kernel, final_norm),
        grid_spec=grid_spec,
        out_shape=jax.ShapeDtypeStruct((t, d), F32),
        compiler_params=_cparams(("arbitrary",)),
        name="moe_combine",
    )(dest, yb, x, gates_t, g)


def hier_moe_residual(x, g_ffn, w_coarse, b_coarse, w_fine, b_fine, w_gate, w_up, w_down,
                      layer, g_final=None):
    experts, gates, rank, counts = moe_router(x, g_ffn, w_coarse, b_coarse, w_fine, b_fine)
    dest, grp_expert, n_active, pad_start, pad_len = _dispatch_plan(experts, rank, counts)
    total = grp_expert.shape[0] * MOE_ROWS
    xs = moe_dispatch(x, g_ffn, dest, pad_start, pad_len, n_active, total)
    yb = moe_experts(xs, grp_expert, n_active, w_gate, w_up, w_down, layer)
    return moe_combine(yb, dest, gates.T, x, g_final)


def kernel(x, g_mix, g_ffn, g_final, w_in_even, conv_w, hg_lower_bound, hg_norm_g, w_out_even, s5_lambda_re, s5_lambda_im, s5_log_dt, s5_b_re, s5_b_im, s5_c_re, s5_c_im, s5_d, w_glu, moe_w_coarse, moe_b_coarse, moe_w_fine, moe_b_fine, moe_w_gate, moe_w_up, moe_w_down):
    bsz, seq, d = x.shape
    depth = g_mix.shape[0]
    conv_width = conv_w.shape[2]
    hg_width = hg_lower_bound.shape[1]
    lb_all = jnp.cumsum(jax.nn.softmax(hg_lower_bound.astype(F32), axis=0), axis=0)
    xf = x.reshape(bsz * seq, d)
    wg_all, wu_all, wd_all = (w.astype(BF16) for w in (moe_w_gate, moe_w_up, moe_w_down))
    for layer in range(depth):
        j = layer // 2
        if layer % 2 == 0:
            z = norm_matmul(xf, g_mix[layer], w_in_even[j].astype(BF16))
            o_b = hgrn2(z, lb_all[j], hg_norm_g[j], bsz, seq, hg_width, 3 * conv_width)
            xf = outproj_even(z, conv_w[j], o_b, xf, w_out_even[j].astype(BF16), seq)
        else:
            h = rmsnorm_cast(xf, g_mix[layer])
            tables = _s5_tables(s5_lambda_re[j], s5_lambda_im[j], s5_log_dt[j],
                                s5_b_re[j], s5_b_im[j], s5_c_re[j], s5_c_im[j])
            y = s5_mix(h, tables, bsz, seq)
            xf = glu_residual(y, h, s5_d[j], xf, w_glu[j].astype(BF16))
        xf = hier_moe_residual(
            xf, g_ffn[layer], moe_w_coarse[layer], moe_b_coarse[layer], moe_w_fine[layer],
            moe_b_fine[layer], wg_all, wu_all, wd_all, layer,
            g_final if layer == depth - 1 else None)
    return xf.reshape(bsz, seq, d)
```

```python
import functools
import math

import jax
import jax.numpy as jnp
from jax import lax
from jax.experimental import pallas as pl
from jax.experimental.pallas import tpu as pltpu

F32 = jnp.float32
BF16 = jnp.bfloat16
EPS = 1e-6

LANES = 128
HG_HEAD_DIM = 128
HG_CHUNK = 64
CONV_K = 3
S5_GROUP = 16
S5_STATE = 64
S5_CHUNK = 64
MOE_GROUPS = 4
MOE_EPG = 8
MOE_EXPERTS = MOE_GROUPS * MOE_EPG
MOE_ROWS = 512
MOE_FF_TILE = 512
ROUTER_ROWS = 40
VMEM_LIMIT = 56 * 1024 * 1024


def _cparams(sem, vmem=VMEM_LIMIT):
    return pltpu.CompilerParams(dimension_semantics=sem, vmem_limit_bytes=vmem)


def _rms(x, g):
    ms = jnp.mean(x * x, axis=-1, keepdims=True)
    return x * lax.rsqrt(ms + EPS) * g


def _norm_kernel(x_ref, g_ref, o_ref):
    o_ref[...] = _rms(x_ref[...], g_ref[...]).astype(o_ref.dtype)


def rmsnorm_cast(x, g, tm=512):
    t, d = x.shape
    return pl.pallas_call(
        _norm_kernel,
        grid=(t // tm,),
        in_specs=[pl.BlockSpec((tm, d), lambda i: (i, 0)),
                  pl.BlockSpec((1, d), lambda i: (0, 0))],
        out_specs=pl.BlockSpec((tm, d), lambda i: (i, 0)),
        out_shape=jax.ShapeDtypeStruct((t, d), BF16),
        compiler_params=_cparams(("parallel",)),
        name="rmsnorm_cast",
    )(x, g.reshape(1, d))


def _norm_matmul_kernel(x_ref, g_ref, w_ref, o_ref, h_ref):
    @pl.when(pl.program_id(1) == 0)
    def _():
        h_ref[...] = _rms(x_ref[...], g_ref[...]).astype(BF16)

    o_ref[...] = jnp.dot(h_ref[...], w_ref[...],
                         preferred_element_type=F32).astype(o_ref.dtype)


def norm_matmul(x, g, w, tm=1024, tn=1024):
    t, d = x.shape
    n = w.shape[1]
    tm = min(tm, t)
    return pl.pallas_call(
        _norm_matmul_kernel,
        grid=(t // tm, n // tn),
        in_specs=[pl.BlockSpec((tm, d), lambda i, j: (i, 0)),
                  pl.BlockSpec((1, d), lambda i, j: (0, 0)),
                  pl.BlockSpec((d, tn), lambda i, j: (0, j))],
        out_specs=pl.BlockSpec((tm, tn), lambda i, j: (i, j)),
        out_shape=jax.ShapeDtypeStruct((t, n), BF16),
        scratch_shapes=[pltpu.VMEM((tm, d), BF16)],
        compiler_params=_cparams(("parallel", "arbitrary")),
        name="norm_matmul",
    )(x, g.reshape(1, d), w)


def _split2(x):
    hi = x.astype(BF16)
    return hi, (x - hi.astype(F32)).astype(BF16)


def _split3(x):
    hi = x.astype(BF16)
    r1 = x - hi.astype(F32)
    mid = r1.astype(BF16)
    lo = (r1 - mid.astype(F32)).astype(BF16)
    return hi, mid, lo


def _hgrn2_kernel(q_ref, f_ref, v_ref, g_ref, lb_ref, ng_ref, o_ref, st_ref):
    c = HG_CHUNK
    hd = HG_HEAD_DIM
    heads = st_ref.shape[0]
    n_chunks = q_ref.shape[0] // c
    ng = ng_ref[...]
    row = lax.broadcasted_iota(jnp.int32, (c, c), 0)
    col = lax.broadcasted_iota(jnp.int32, (c, c), 1)
    causal = row >= col
    tril = causal.astype(BF16)
    mid = c // 2

    @pl.when(pl.program_id(1) == 0)
    def _():
        st_ref[...] = jnp.zeros_like(st_ref)

    nt_dims = (((1,), (1,)), ((), ()))
    hs = range(heads)
    cols = [slice(h * hd, (h + 1) * hd) for h in hs]

    def step(n, carry):
        rows = pl.ds(pl.multiple_of(n * c, c), c)
        q = [q_ref[rows, cl].astype(F32) for cl in cols]
        f = [lb_ref[:, cl] + (1.0 - lb_ref[:, cl]) * jax.nn.sigmoid(f_ref[rows, cl].astype(F32))
             for cl in cols]
        k = [1.0 - fh for fh in f]
        parts = [_split3(jnp.log(fh)) for fh in f]
        b = [sum(jnp.dot(tril, p, preferred_element_type=F32) for p in ph) for ph in parts]
        ref = [bh[mid - 1:mid, :] for bh in b]
        b_last = [bh[c - 1:c, :] for bh in b]
        st = [st_ref[h] for h in hs]
        qs = [(q[h] * jnp.exp(b[h] - ref[h])).astype(BF16) for h in hs]
        ks = [(k[h] * jnp.exp(ref[h] - b[h])).astype(BF16) for h in hs]
        scores = [lax.dot_general(qs[h], ks[h], nt_dims, preferred_element_type=F32) for h in hs]
        qe = [(q[h] * jnp.exp(b[h])).astype(BF16) for h in hs]
        o_inter = [lax.dot_general(qe[h], st[h].astype(BF16), nt_dims, preferred_element_type=F32)
                   for h in hs]
        kd = [(k[h] * jnp.exp(b_last[h] - b[h])).astype(BF16) for h in hs]
        upd = [lax.dot_general(v_ref[rows, cols[h]], kd[h], (((0,), (0,)), ((), ())),
                               preferred_element_type=F32) for h in hs]
        for h in hs:
            st_ref[h] = st[h] * jnp.exp(b_last[h]) + upd[h]
        sc = [jnp.where(causal, s, 0.0).astype(BF16) for s in scores]
        o = [o_inter[h] + jnp.dot(sc[h], v_ref[rows, cols[h]], preferred_element_type=F32)
             for h in hs]
        for h in hs:
            oh = o[h] * lax.rsqrt(jnp.mean(o[h] * o[h], axis=-1, keepdims=True) + EPS) * ng
            gate = g_ref[rows, cols[h]].astype(F32)
            o_ref[rows, cols[h]] = (oh * (gate * jax.nn.sigmoid(gate))).astype(o_ref.dtype)
        return carry

    lax.fori_loop(0, n_chunks, step, 0)


def hgrn2(z, lb, norm_g, bsz, seq, width, col0, ts=1024):
    hd = HG_HEAD_DIM
    heads = width // hd
    cb = col0 // width
    ts = min(ts, seq)
    nt = seq // ts

    def zspec(k):
        return pl.BlockSpec((ts, width), lambda b, s: (b * nt + s, cb + k))

    return pl.pallas_call(
        _hgrn2_kernel,
        grid=(bsz, nt),
        in_specs=[zspec(0), zspec(1), zspec(2), zspec(3),
                  pl.BlockSpec((1, width), lambda b, s: (0, 0)),
                  pl.BlockSpec((1, hd), lambda b, s: (0, 0))],
        out_specs=pl.BlockSpec((ts, width), lambda b, s: (b * nt + s, 0)),
        out_shape=jax.ShapeDtypeStruct((bsz * seq, width), BF16),
        scratch_shapes=[pltpu.VMEM((heads, hd, hd), F32)],
        compiler_params=_cparams(("parallel", "arbitrary")),
        name="hgrn2",
    )(z, z, z, z, lb.reshape(1, width), norm_g.reshape(1, hd))


def _outproj_kernel(seq_tiles, ab_ref, ac_ref, ah_ref, hc_ref, hh_ref, cw_ref,
                    ob_ref, x_ref, wa_ref, wb_ref, o_ref):
    i = pl.program_id(0)
    u = ac_ref[...].astype(F32) * ah_ref[...].astype(F32)
    halo = hc_ref[...].astype(F32) * hh_ref[...].astype(F32)
    halo = jnp.where(i % seq_tiles == 0, 0.0, halo)
    hr = halo.shape[0]
    row = lax.broadcasted_iota(jnp.int32, u.shape, 0)
    u1 = jnp.where(row == 0, halo[hr - 1:hr, :], pltpu.roll(u, 1, axis=0))
    u2 = jnp.where(row == 0, halo[hr - 2:hr - 1, :],
                   jnp.where(row == 1, halo[hr - 1:hr, :], pltpu.roll(u, 2, axis=0)))
    cw = cw_ref[...]
    conv = cw[2:3, :] * u + cw[1:2, :] * u1 + cw[0:1, :] * u2
    ya = (ab_ref[...].astype(F32) * conv).astype(BF16)
    mix = jnp.dot(ya, wa_ref[...], preferred_element_type=F32)
    mix = mix + jnp.dot(ob_ref[...], wb_ref[...], preferred_element_type=F32)
    o_ref[...] = x_ref[...] + mix


def outproj_even(z, conv_w, o_b, x, w_out, seq, tm=256, halo=16):
    t, d = x.shape
    cw = conv_w.shape[1]
    hw = o_b.shape[1]
    hb = tm // halo
    kern = functools.partial(_outproj_kernel, seq // tm)
    return pl.pallas_call(
        kern,
        grid=(t // tm,),
        in_specs=[pl.BlockSpec((tm, cw), lambda i: (i, 0)),
                  pl.BlockSpec((tm, cw), lambda i: (i, 1)),
                  pl.BlockSpec((tm, cw), lambda i: (i, 2)),
                  pl.BlockSpec((halo, cw), lambda i: (jnp.maximum(i * hb - 1, 0), 1)),
                  pl.BlockSpec((halo, cw), lambda i: (jnp.maximum(i * hb - 1, 0), 2)),
                  pl.BlockSpec((CONV_K, cw), lambda i: (0, 0)),
                  pl.BlockSpec((tm, hw), lambda i: (i, 0)),
                  pl.BlockSpec((tm, d), lambda i: (i, 0)),
                  pl.BlockSpec((cw, d), lambda i: (0, 0)),
                  pl.BlockSpec((hw, d), lambda i: (1, 0))],
        out_specs=pl.BlockSpec((tm, d), lambda i: (i, 0)),
        out_shape=jax.ShapeDtypeStruct((t, d), F32),
        compiler_params=_cparams(("parallel",)),
        name="outproj_even",
    )(z, z, z, z, z, conv_w, o_b, x, w_out, w_out)


def _rep_rows(x, n):
    r, l = x.shape
    return jnp.broadcast_to(x[:, None, :], (r, n, l)).reshape(r * n, l)


def _tile_rows(x, n):
    r, l = x.shape
    return jnp.broadcast_to(x[None, :, :], (n, r, l)).reshape(n * r, l)


def _s5_kernel(n_chunks, u_ref, p_ref, y_ref, m_ref):
    lc = S5_CHUNK
    gc = S5_GROUP
    half = S5_STATE
    prm = p_ref[0]
    tab = [prm[i * gc:(i + 1) * gc, :] for i in range(8)]
    crd, cid, bqa, bqb, bwa, bwb, cga, cgb = tab
    alpha = prm[8 * gc:8 * gc + 1, :]
    beta = prm[8 * gc + 1:8 * gc + 2, :]

    def cpow(tt):
        mag = jnp.exp(alpha * tt)
        ang = beta * tt
        return mag * jnp.cos(ang), mag * jnp.sin(ang)

    t_col = lax.broadcasted_iota(jnp.int32, (lc, LANES), 0).astype(F32)
    lane = lax.broadcasted_iota(jnp.int32, (lc, LANES), 1)
    p0r, p0i = cpow(t_col)
    p1r, p1i = cpow(t_col + 1.0)
    prr, pri = cpow(float(lc - 1) - t_col)

    qcat = _rep_rows(crd, gc) * _tile_rows(bqa, gc) + _rep_rows(cid, gc) * _tile_rows(bqb, gc)
    p0cat = jnp.where(lane < half, p0r, p0i)
    p0cat2 = jnp.concatenate([p0cat, p0cat], axis=0)
    kc2 = lax.dot_general(qcat, p0cat2, (((1,), (1,)), ((), ())),
                          precision=lax.Precision.HIGHEST, preferred_element_type=F32)
    lane16 = lax.broadcasted_iota(jnp.int32, (gc, LANES), 1)
    k_row = jnp.concatenate(
        [jnp.where(lane16 < half, kc2[2 * j * gc:(2 * j + 1) * gc, :],
                   kc2[(2 * j + 1) * gc:(2 * j + 2) * gc, :]) for j in range(gc // 2)],
        axis=1)
    seg_pos = lax.broadcasted_iota(jnp.int32, k_row.shape, 1) % lc
    for s in range(lc):
        shifted = k_row if s == 0 else pltpu.roll(k_row, s, axis=1)
        m_ref[s * gc:(s + 1) * gc, :] = jnp.where(seg_pos >= s, shifted, 0.0).astype(BF16)

    u = u_ref[0]
    wcat = (_rep_rows(prr, gc) * _tile_rows(bwa, lc)
            + _rep_rows(pri, gc) * _tile_rows(bwb, lc)).astype(BF16)
    st = jnp.dot(u, wcat, preferred_element_type=F32)

    rows = st.shape[0]
    rpos = lax.broadcasted_iota(jnp.int32, (rows, LANES), 0) % n_chunks
    rlane = lax.broadcasted_iota(jnp.int32, (rows, LANES), 1)
    ar, ai = cpow(jnp.full((1, LANES), float(lc), F32))
    d = 1
    while d < n_chunks:
        sh = jnp.where(rpos >= d, pltpu.roll(st, d, axis=0), 0.0)
        a2 = jnp.where(rlane < half, -ai, ai)
        st = st + ar * sh + a2 * pltpu.roll(sh, half, axis=1)
        ar, ai = ar * ar - ai * ai, 2.0 * ar * ai
        d *= 2
    h0 = jnp.where(rpos >= 1, pltpu.roll(st, 1, axis=0), 0.0).astype(BF16)

    gcat = (_rep_rows(cga, lc) * _tile_rows(p1r, gc)
            + _rep_rows(cgb, lc) * _tile_rows(p1i, gc)).astype(BF16)
    y = jnp.dot(u, m_ref[...], preferred_element_type=F32)
    y = y + lax.dot_general(h0, gcat, (((1,), (1,)), ((), ())), preferred_element_type=F32)
    y_ref[0] = y.astype(y_ref.dtype)


def _s5_tables(lam_re, lam_im, log_dt, b_re, b_im, c_re, c_im):
    dt = jnp.exp(log_dt.astype(F32))[:, None]
    lr = lam_re.astype(F32)
    li = lam_im.astype(F32)
    mag = jnp.exp(lr * dt)
    ar = mag * jnp.cos(li * dt)
    ai = mag * jnp.sin(li * dt)
    den = lr * lr + li * li
    cr = ((ar - 1.0) * lr + ai * li) / den
    ci = (ai * lr - (ar - 1.0) * li) / den
    bbr = (cr[..., None] * b_re - ci[..., None] * b_im).transpose(0, 2, 1)
    bbi = (cr[..., None] * b_im + ci[..., None] * b_re).transpose(0, 2, 1)
    cre = c_re.astype(F32)
    cim = c_im.astype(F32)
    cat = lambda a, b: jnp.concatenate([a, b], axis=-1)
    alpha = (lr * dt)[:, None, :]
    beta = (li * dt)[:, None, :]
    pad = jnp.zeros((lr.shape[0], 6, 2 * lr.shape[1]), F32)
    return jnp.concatenate([
        cat(cre, cre), cat(cim, cim),
        cat(bbr, -bbi), cat(-bbi, -bbr),
        cat(bbr, bbi), cat(-bbi, bbr),
        cat(cre, -cim), cat(-cim, -cre),
        cat(alpha, alpha), cat(beta, beta), pad], axis=1)


def s5_mix(h, tables, bsz, seq):
    t, w = h.shape
    lc, gc = S5_CHUNK, S5_GROUP
    groups = w // gc
    n = seq // lc
    rows = bsz * n
    u = h.reshape(bsz, n, lc, groups, gc).transpose(3, 0, 1, 2, 4).reshape(groups, rows, lc * gc)
    y = pl.pallas_call(
        functools.partial(_s5_kernel, n),
        grid=(groups,),
        in_specs=[pl.BlockSpec((1, rows, lc * gc), lambda g: (g, 0, 0)),
                  pl.BlockSpec((1,) + tables.shape[1:], lambda g: (g, 0, 0))],
        out_specs=pl.BlockSpec((1, rows, lc * gc), lambda g: (g, 0, 0)),
        out_shape=jax.ShapeDtypeStruct((groups, rows, lc * gc), BF16),
        scratch_shapes=[pltpu.VMEM((lc * gc, lc * gc), BF16)],
        compiler_params=_cparams(("parallel",)),
        name="s5_mix",
    )(u, tables)
    return y.reshape(groups, bsz, n, gc, lc).transpose(1, 2, 4, 0, 3).reshape(t, w)


def _glu_kernel(y_ref, h_ref, d_ref, x_ref, wa_ref, wb_ref, o_ref, act_ref):
    @pl.when(pl.program_id(1) == 0)
    def _():
        y = y_ref[...].astype(F32) + d_ref[...] * h_ref[...].astype(F32)
        cdf = 0.5 * (1.0 + jnp.tanh(math.sqrt(2.0 / math.pi) * (y + 0.044715 * (y * y * y))))
        act_ref[...] = (y * cdf).astype(BF16)

    act = act_ref[...]
    za = jnp.dot(act, wa_ref[...], preferred_element_type=F32)
    zb = jnp.dot(act, wb_ref[...], preferred_element_type=F32)
    o_ref[...] = x_ref[...] + za * jax.nn.sigmoid(zb)


def glu_residual(y, h, d_skip, x, w_glu, tm=1024, tn=512):
    t, d = x.shape
    w = y.shape[1]
    tm = min(tm, t)
    nb = d // tn
    return pl.pallas_call(
        _glu_kernel,
        grid=(t // tm, nb),
        in_specs=[pl.BlockSpec((tm, w), lambda i, j: (i, 0)),
                  pl.BlockSpec((tm, w), lambda i, j: (i, 0)),
                  pl.BlockSpec((1, w), lambda i, j: (0, 0)),
                  pl.BlockSpec((tm, tn), lambda i, j: (i, j)),
                  pl.BlockSpec((w, tn), lambda i, j: (0, j)),
                  pl.BlockSpec((w, tn), lambda i, j: (0, j + nb))],
        out_specs=pl.BlockSpec((tm, tn), lambda i, j: (i, j)),
        out_shape=jax.ShapeDtypeStruct((t, d), F32),
        scratch_shapes=[pltpu.VMEM((tm, w), BF16)],
        compiler_params=_cparams(("parallel", "arbitrary")),
        name="glu_residual",
    )(y, h, d_skip.reshape(1, w), x, w_glu, w_glu)


def _router_kernel(x_ref, g_ref, w_ref, b_ref, e_ref, p_ref, r_ref, c_ref, cnt_ref):
    tm = x_ref.shape[0]

    @pl.when(pl.program_id(0) == 0)
    def _():
        cnt_ref[...] = jnp.zeros_like(cnt_ref)

    h = _rms(x_ref[...], g_ref[...])
    nt_dot = lambda a, b: lax.dot_general(a, b, (((1,), (1,)), ((), ())),
                                          preferred_element_type=F32)
    w_hi, w_lo = _split2(w_ref[...])
    h_hi, h_lo = _split2(h)
    logits = (nt_dot(w_hi, h_hi) + nt_dot(w_hi, h_lo) + nt_dot(w_lo, h_hi)
              + b_ref[...])
    lc = [logits[i:i + 1, :] for i in range(MOE_GROUPS)]
    m = functools.reduce(jnp.maximum, lc)
    grp = jnp.full(m.shape, MOE_GROUPS - 1, jnp.int32)
    for i in range(MOE_GROUPS - 2, -1, -1):
        grp = jnp.where(lc[i] == m, i, grp)
    den = functools.reduce(lambda a, b: a + b, [jnp.exp(l - m) for l in lc])
    p_top = 1.0 / den
    fine = [logits[MOE_GROUPS + j:MOE_GROUPS + j + 1, :] for j in range(MOE_EXPERTS)]
    sel = []
    for e in range(MOE_EPG):
        v = fine[(MOE_GROUPS - 1) * MOE_EPG + e]
        for i in range(MOE_GROUPS - 2, -1, -1):
            v = jnp.where(grp == i, fine[i * MOE_EPG + e], v)
        sel.append(v)

    def top1(vals):
        best = functools.reduce(jnp.maximum, vals)
        idx = jnp.full(best.shape, MOE_EPG - 1, jnp.int32)
        for e in range(MOE_EPG - 2, -1, -1):
            idx = jnp.where(vals[e] == best, e, idx)
        return best, idx

    v1, i1 = top1(sel)
    v2, i2 = top1([jnp.where(i1 == e, -jnp.inf, sel[e]) for e in range(MOE_EPG)])
    ex = jnp.exp(v2 - v1)
    s = 1.0 + ex
    e0 = grp * MOE_EPG + i1
    e1 = grp * MOE_EPG + i2
    e_ref[0:1, :] = e0
    e_ref[1:2, :] = e1
    p_ref[0:1, :] = p_top * (1.0 / s)
    p_ref[1:2, :] = p_top * (ex / s)

    eid = lax.broadcasted_iota(jnp.int32, (MOE_EXPERTS, tm), 0)
    tri = (lax.broadcasted_iota(jnp.int32, (tm, tm), 0)
           <= lax.broadcasted_iota(jnp.int32, (tm, tm), 1)).astype(BF16)
    base = cnt_ref[...]
    for k, ek in enumerate((e0, e1)):
        hot = eid == ek
        hot_f = hot.astype(F32)
        csum = jnp.dot(hot.astype(BF16), tri, preferred_element_type=F32)
        rank = jnp.sum(hot_f * (base + csum - 1.0), axis=0, keepdims=True)
        r_ref[k:k + 1, :] = rank.astype(jnp.int32)
        base = base + jnp.sum(hot_f, axis=1, keepdims=True)
    cnt_ref[...] = base
    c_ref[...] = base.astype(jnp.int32)


def moe_router(x, g, w_coarse, b_coarse, w_fine, b_fine, tm=512):
    t, d = x.shape
    pad = ROUTER_ROWS - MOE_GROUPS - MOE_EXPERTS
    w = jnp.concatenate([w_coarse.T, w_fine.T, jnp.zeros((pad, d), F32)], axis=0)
    b = jnp.concatenate([b_coarse, b_fine, jnp.zeros((pad,), F32)]).reshape(ROUTER_ROWS, 1)
    tok_spec = pl.BlockSpec((2, tm), lambda i: (0, i))
    return pl.pallas_call(
        _router_kernel,
        grid=(t // tm,),
        in_specs=[pl.BlockSpec((tm, d), lambda i: (i, 0)),
                  pl.BlockSpec((1, d), lambda i: (0, 0)),
                  pl.BlockSpec((ROUTER_ROWS, d), lambda i: (0, 0)),
                  pl.BlockSpec((ROUTER_ROWS, 1), lambda i: (0, 0))],
        out_specs=[tok_spec, tok_spec, tok_spec,
                   pl.BlockSpec((MOE_EXPERTS, 1), lambda i: (0, 0))],
        out_shape=[jax.ShapeDtypeStruct((2, t), jnp.int32),
                   jax.ShapeDtypeStruct((2, t), F32),
                   jax.ShapeDtypeStruct((2, t), jnp.int32),
                   jax.ShapeDtypeStruct((MOE_EXPERTS, 1), jnp.int32)],
        scratch_shapes=[pltpu.VMEM((MOE_EXPERTS, 1), F32)],
        compiler_params=_cparams(("arbitrary",)),
        name="moe_router",
    )(x, g.reshape(1, d), w, b)


def _dispatch_plan(experts, rank, counts):
    n_assign = experts.size
    counts = counts.reshape(-1)
    padded = (counts + MOE_ROWS - 1) // MOE_ROWS * MOE_ROWS
    pend = jnp.cumsum(padded)
    pstart = pend - padded
    eids = jnp.arange(MOE_EXPERTS, dtype=jnp.int32)
    dest = rank + jnp.sum(jnp.where(experts[..., None] == eids, pstart, 0), axis=-1)
    n_groups = -(-n_assign // MOE_ROWS) + MOE_EXPERTS
    g0 = jnp.arange(n_groups, dtype=jnp.int32) * MOE_ROWS
    grp_expert = jnp.minimum(jnp.sum(pend[None, :] <= g0[:, None], axis=1), MOE_EXPERTS - 1)
    n_active = (pend[-1] // MOE_ROWS).reshape(1)
    i32 = lambda a: a.astype(jnp.int32)
    return (i32(dest).reshape(-1), i32(grp_expert), i32(n_active),
            i32(pstart + counts), i32(padded - counts))


def _dispatch_kernel(dest_ref, ps_ref, pn_ref, na_ref, x_ref, g_ref, xs_hbm, hbuf, zbuf, sem,
                     zsem):
    i = pl.program_id(0)
    n = pl.num_programs(0)
    tm = x_ref.shape[0]
    t = n * tm
    slot = i % 2

    def wait_slot(s):
        for _ in range(2):
            pltpu.make_async_copy(hbuf.at[s], xs_hbm.at[pl.ds(0, tm)], sem.at[s]).wait()

    def zero_row_copy(row):
        return pltpu.make_async_copy(zbuf.at[pl.ds(0, 1)], xs_hbm.at[pl.ds(row, 1)], zsem.at[0])

    def for_each_pad_row(fn):
        def per_expert(e, c):
            def per_row(r, c2):
                fn(ps_ref[e] + r)
                return c2
            return lax.fori_loop(0, pn_ref[e], per_row, c)
        lax.fori_loop(0, ps_ref.shape[0], per_expert, 0)

    def tail_group_copy(grp):
        rows = pl.ds(pl.multiple_of(grp * tm, tm), tm)
        return pltpu.make_async_copy(hbuf.at[1], xs_hbm.at[rows], sem.at[1])

    def for_each_tail_group(fn):
        def body(grp, c):
            fn(grp)
            return c
        lax.fori_loop(na_ref[0], xs_hbm.shape[0] // tm, body, 0)

    @pl.when(i == 0)
    def _():
        zbuf[...] = jnp.zeros_like(zbuf)
        for_each_pad_row(lambda row: zero_row_copy(row).start())
        hbuf[1] = jnp.zeros(hbuf.shape[1:], hbuf.dtype)
        for_each_tail_group(lambda grp: tail_group_copy(grp).start())
        for_each_tail_group(lambda grp: tail_group_copy(grp).wait())

    @pl.when(i >= 2)
    def _():
        wait_slot(slot)

    hbuf[slot] = _rms(x_ref[...], g_ref[...])

    def body(r, c):
        for k in range(2):
            row = dest_ref[k * t + i * tm + r]
            pltpu.make_async_copy(hbuf.at[slot, pl.ds(r, 1)], xs_hbm.at[pl.ds(row, 1)],
                                  sem.at[slot]).start()
        return c
    lax.fori_loop(0, tm, body, 0, unroll=8)

    @pl.when(i == n - 1)
    def _():
        wait_slot(slot)

        @pl.when(n >= 2)
        def _():
            wait_slot(1 - slot)

        for_each_pad_row(lambda row: zero_row_copy(row).wait())


def moe_dispatch(x, g, dest, pad_start, pad_len, n_active, total):
    t, d = x.shape
    tm = MOE_ROWS
    grid_spec = pltpu.PrefetchScalarGridSpec(
        num_scalar_prefetch=4,
        grid=(t // tm,),
        in_specs=[pl.BlockSpec((tm, d), lambda i, *_: (i, 0)),
                  pl.BlockSpec((1, d), lambda i, *_: (0, 0))],
        out_specs=pl.BlockSpec(memory_space=pl.ANY),
        scratch_shapes=[pltpu.VMEM((2, tm, d), F32),
                        pltpu.VMEM((8, d), F32),
                        pltpu.SemaphoreType.DMA((2,)),
                        pltpu.SemaphoreType.DMA((1,))],
    )
    return pl.pallas_call(
        _dispatch_kernel,
        grid_spec=grid_spec,
        out_shape=jax.ShapeDtypeStruct((total, d), F32),
        compiler_params=_cparams(("arbitrary",)),
        name="moe_dispatch",
    )(dest, pad_start, pad_len, n_active, x, g.reshape(1, d))


def _experts_kernel(ge_ref, na_ref, xs_ref, wg_ref, wu_ref, wd_ref, o_ref, xb_ref):
    g = pl.program_id(0)
    c = pl.program_id(1)

    @pl.when(g < na_ref[0])
    def _():
        @pl.when(c == 0)
        def _():
            xb_ref[...] = xs_ref[...].astype(BF16)

        xb = xb_ref[...]
        a = jnp.dot(xb, wg_ref[...].astype(BF16), preferred_element_type=F32)
        b = jnp.dot(xb, wu_ref[...].astype(BF16), preferred_element_type=F32)
        hid = (a * jax.nn.sigmoid(a) * b).astype(BF16)
        y = jnp.dot(hid, wd_ref[...].astype(BF16), preferred_element_type=F32)

        @pl.when(c == 0)
        def _():
            o_ref[...] = y

        @pl.when(c > 0)
        def _():
            o_ref[...] += y

    @pl.when(jnp.logical_and(g >= na_ref[0], c == 0))
    def _():
        o_ref[...] = jnp.zeros_like(o_ref)


def moe_experts(xs, grp_expert, n_active, w_gate, w_up, w_down, layer):
    total, d = xs.shape
    ff = w_gate.shape[3]
    tf = MOE_FF_TILE
    nc = ff // tf
    n_groups = total // MOE_ROWS

    def blk(g, na):
        return jnp.maximum(jnp.minimum(g, na[0] - 1), 0)

    def chunk(g, c, na):
        return jnp.where(g < na[0], c, nc - 1)

    grid_spec = pltpu.PrefetchScalarGridSpec(
        num_scalar_prefetch=2,
        grid=(n_groups, nc),
        in_specs=[pl.BlockSpec((MOE_ROWS, d), lambda g, c, ge, na: (blk(g, na), 0)),
                  pl.BlockSpec((None, None, d, tf),
                               lambda g, c, ge, na: (layer, ge[blk(g, na)], 0, chunk(g, c, na))),
                  pl.BlockSpec((None, None, d, tf),
                               lambda g, c, ge, na: (layer, ge[blk(g, na)], 0, chunk(g, c, na))),
                  pl.BlockSpec((None, None, tf, d),
                               lambda g, c, ge, na: (layer, ge[blk(g, na)], chunk(g, c, na), 0))],
        out_specs=pl.BlockSpec((MOE_ROWS, d), lambda g, c, ge, na: (g, 0)),
        scratch_shapes=[pltpu.VMEM((MOE_ROWS, d), BF16)],
    )
    return pl.pallas_call(
        _experts_kernel,
        grid_spec=grid_spec,
        out_shape=jax.ShapeDtypeStruct((total, d), F32),
        compiler_params=_cparams(("arbitrary", "arbitrary")),
        name="moe_experts",
    )(grp_expert, n_active, xs, w_gate, w_up, w_down)


def _combine_kernel(final_norm, pos_ref, yb_hbm, x_ref, p_ref, g_ref, o_ref, ybuf, sem):
    i = pl.program_id(0)
    n = pl.num_programs(0)
    tm = x_ref.shape[0]
    t = n * tm

    def issue(tile, slot):
        def body(r, c):
            for k in range(2):
                row = pos_ref[k * t + tile * tm + r]
                pltpu.make_async_copy(yb_hbm.at[pl.ds(row, 1)], ybuf.at[slot, k, pl.ds(r, 1)],
                                      sem.at[slot]).start()
            return c
        lax.fori_loop(0, tm, body, 0, unroll=8)

    @pl.when(i == 0)
    def _():
        issue(0, 0)

    @pl.when(i + 1 < n)
    def _():
        issue(i + 1, (i + 1) % 2)

    slot = i % 2
    for k in range(2):
        pltpu.make_async_copy(yb_hbm.at[pl.ds(0, tm)], ybuf.at[slot, k], sem.at[slot]).wait()
    p = p_ref[...]
    out = x_ref[...] + (p[:, 0:1] * ybuf[slot, 0] + p[:, 1:2] * ybuf[slot, 1])
    if final_norm:
        out = _rms(out, g_ref[...])
    o_ref[...] = out


def moe_combine(yb, dest, gates_t, x, g_final=None, tm=256):
    t, d = x.shape
    final_norm = g_final is not None
    g = (g_final if final_norm else jnp.ones((d,), F32)).reshape(1, d)
    grid_spec = pltpu.PrefetchScalarGridSpec(
        num_scalar_prefetch=1,
        grid=(t // tm,),
        in_specs=[pl.BlockSpec(memory_space=pl.ANY),
                  pl.BlockSpec((tm, d), lambda i, pos: (i, 0)),
                  pl.BlockSpec((tm, 2), lambda i, pos: (i, 0)),
                  pl.BlockSpec((1, d), lambda i, pos: (0, 0))],
        out_specs=pl.BlockSpec((tm, d), lambda i, pos: (i, 0)),
        scratch_shapes=[pltpu.VMEM((2, 2, tm, d), F32),
                        pltpu.SemaphoreType.DMA((2,))],
    )
    return pl.pallas_call(
        functools.partial(_combine_kernel, final_norm),
        grid_spec=grid_spec,
        out_shape=jax.ShapeDtypeStruct((t, d), F32),
        compiler_params=_cparams(("arbitrary",)),
        name="moe_combine",
    )(dest, yb, x, gates_t, g)


def hier_moe_residual(x, g_ffn, w_coarse, b_coarse, w_fine, b_fine, w_gate, w_up, w_down,
                      layer, g_final=None):
    experts, gates, rank, counts = moe_router(x, g_ffn, w_coarse, b_coarse, w_fine, b_fine)
    dest, grp_expert, n_active, pad_start, pad_len = _dispatch_plan(experts, rank, counts)
    total = grp_expert.shape[0] * MOE_ROWS
    xs = moe_dispatch(x, g_ffn, dest, pad_start, pad_len, n_active, total)
    yb = moe_experts(xs, grp_expert, n_active, w_gate, w_up, w_down, layer)
    return moe_combine(yb, dest, gates.T, x, g_final)


def kernel(x, g_mix, g_ffn, g_final, w_in_even, conv_w, hg_lower_bound, hg_norm_g, w_out_even, s5_lambda_re, s5_lambda_im, s5_log_dt, s5_b_re, s5_b_im, s5_c_re, s5_c_im, s5_d, w_glu, moe_w_coarse, moe_b_coarse, moe_w_fine, moe_b_fine, moe_w_gate, moe_w_up, moe_w_down):
    bsz, seq, d = x.shape
    depth = g_mix.shape[0]
    conv_width = conv_w.shape[2]
    hg_width = hg_lower_bound.shape[1]
    lb_all = jnp.cumsum(jax.nn.softmax(hg_lower_bound.astype(F32), axis=0), axis=0)
    xf = x.reshape(bsz * seq, d)
    for layer in range(depth):
        j = layer // 2
        if layer % 2 == 0:
            z = norm_matmul(xf, g_mix[layer], w_in_even[j].astype(BF16))
            o_b = hgrn2(z, lb_all[j], hg_norm_g[j], bsz, seq, hg_width, 3 * conv_width)
            xf = outproj_even(z, conv_w[j], o_b, xf, w_out_even[j].astype(BF16), seq)
        else:
            h = rmsnorm_cast(xf, g_mix[layer])
            tables = _s5_tables(s5_lambda_re[j], s5_lambda_im[j], s5_log_dt[j],
                                s5_b_re[j], s5_b_im[j], s5_c_re[j], s5_c_im[j])
            y = s5_mix(h, tables, bsz, seq)
            xf = glu_residual(y, h, s5_d[j], xf, w_glu[j].astype(BF16))
        xf = hier_moe_residual(
            xf, g_ffn[layer], moe_w_coarse[layer], moe_b_coarse[layer], moe_w_fine[layer],
            moe_b_fine[layer], moe_w_gate, moe_w_up, moe_w_down, layer,
            g_final if layer == depth - 1 else None)
    return xf.reshape(bsz, seq, d)
```

```python
import functools
import math

import jax
import jax.numpy as jnp
from jax import lax
from jax.experimental import pallas as pl
from jax.experimental.pallas import tpu as pltpu

F32 = jnp.float32
BF16 = jnp.bfloat16
EPS = 1e-6

LANES = 128
HG_HEAD_DIM = 128
HG_CHUNK = 64
CONV_K = 3
S5_GROUP = 16
S5_STATE = 64
S5_CHUNK = 64
MOE_GROUPS = 4
MOE_EPG = 8
MOE_EXPERTS = MOE_GROUPS * MOE_EPG
MOE_ROWS = 256
ROUTER_ROWS = 40
VMEM_LIMIT = 56 * 1024 * 1024


def _cparams(sem, vmem=VMEM_LIMIT):
    return pltpu.CompilerParams(dimension_semantics=sem, vmem_limit_bytes=vmem)


def _rms(x, g):
    ms = jnp.mean(x * x, axis=-1, keepdims=True)
    return x * lax.rsqrt(ms + EPS) * g


def _norm_kernel(x_ref, g_ref, o_ref):
    o_ref[...] = _rms(x_ref[...], g_ref[...]).astype(o_ref.dtype)


def rmsnorm_cast(x, g, tm=512):
    t, d = x.shape
    return pl.pallas_call(
        _norm_kernel,
        grid=(t // tm,),
        in_specs=[pl.BlockSpec((tm, d), lambda i: (i, 0)),
                  pl.BlockSpec((1, d), lambda i: (0, 0))],
        out_specs=pl.BlockSpec((tm, d), lambda i: (i, 0)),
        out_shape=jax.ShapeDtypeStruct((t, d), BF16),
        compiler_params=_cparams(("parallel",)),
        name="rmsnorm_cast",
    )(x, g.reshape(1, d))


def _norm_matmul_kernel(x_ref, g_ref, w_ref, o_ref, h_ref):
    @pl.when(pl.program_id(1) == 0)
    def _():
        h_ref[...] = _rms(x_ref[...], g_ref[...]).astype(BF16)

    o_ref[...] = jnp.dot(h_ref[...], w_ref[...],
                         preferred_element_type=F32).astype(o_ref.dtype)


def norm_matmul(x, g, w, tm=1024, tn=1024):
    t, d = x.shape
    n = w.shape[1]
    tm = min(tm, t)
    return pl.pallas_call(
        _norm_matmul_kernel,
        grid=(t // tm, n // tn),
        in_specs=[pl.BlockSpec((tm, d), lambda i, j: (i, 0)),
                  pl.BlockSpec((1, d), lambda i, j: (0, 0)),
                  pl.BlockSpec((d, tn), lambda i, j: (0, j))],
        out_specs=pl.BlockSpec((tm, tn), lambda i, j: (i, j)),
        out_shape=jax.ShapeDtypeStruct((t, n), BF16),
        scratch_shapes=[pltpu.VMEM((tm, d), BF16)],
        compiler_params=_cparams(("parallel", "arbitrary")),
        name="norm_matmul",
    )(x, g.reshape(1, d), w)


def _split2(x):
    hi = x.astype(BF16)
    return hi, (x - hi.astype(F32)).astype(BF16)


def _split3(x):
    hi = x.astype(BF16)
    r1 = x - hi.astype(F32)
    mid = r1.astype(BF16)
    lo = (r1 - mid.astype(F32)).astype(BF16)
    return hi, mid, lo


def _hgrn2_kernel(q_ref, f_ref, v_ref, g_ref, lb_ref, ng_ref, o_ref, st_ref):
    c = HG_CHUNK
    hd = HG_HEAD_DIM
    heads = st_ref.shape[0]
    n_chunks = q_ref.shape[0] // c
    ng = ng_ref[...]
    row = lax.broadcasted_iota(jnp.int32, (c, c), 0)
    col = lax.broadcasted_iota(jnp.int32, (c, c), 1)
    causal = row >= col
    tril = causal.astype(BF16)
    mid = c // 2

    @pl.when(pl.program_id(1) == 0)
    def _():
        st_ref[...] = jnp.zeros_like(st_ref)

    nt_dims = (((1,), (1,)), ((), ()))
    hs = range(heads)
    cols = [slice(h * hd, (h + 1) * hd) for h in hs]

    def step(n, carry):
        rows = pl.ds(pl.multiple_of(n * c, c), c)
        q = [q_ref[rows, cl].astype(F32) for cl in cols]
        f = [lb_ref[:, cl] + (1.0 - lb_ref[:, cl]) * jax.nn.sigmoid(f_ref[rows, cl].astype(F32))
             for cl in cols]
        k = [1.0 - fh for fh in f]
        parts = [_split3(jnp.log(fh)) for fh in f]
        b = [sum(jnp.dot(tril, p, preferred_element_type=F32) for p in ph) for ph in parts]
        ref = [bh[mid - 1:mid, :] for bh in b]
        b_last = [bh[c - 1:c, :] for bh in b]
        st = [st_ref[h] for h in hs]
        qs = [(q[h] * jnp.exp(b[h] - ref[h])).astype(BF16) for h in hs]
        ks = [(k[h] * jnp.exp(ref[h] - b[h])).astype(BF16) for h in hs]
        scores = [lax.dot_general(qs[h], ks[h], nt_dims, preferred_element_type=F32) for h in hs]
        qe = [(q[h] * jnp.exp(b[h])).astype(BF16) for h in hs]
        o_inter = [lax.dot_general(qe[h], st[h].astype(BF16), nt_dims, preferred_element_type=F32)
                   for h in hs]
        kd = [(k[h] * jnp.exp(b_last[h] - b[h])).astype(BF16) for h in hs]
        upd = [lax.dot_general(v_ref[rows, cols[h]], kd[h], (((0,), (0,)), ((), ())),
                               preferred_element_type=F32) for h in hs]
        for h in hs:
            st_ref[h] = st[h] * jnp.exp(b_last[h]) + upd[h]
        sc = [jnp.where(causal, s, 0.0).astype(BF16) for s in scores]
        o = [o_inter[h] + jnp.dot(sc[h], v_ref[rows, cols[h]], preferred_element_type=F32)
             for h in hs]
        for h in hs:
            oh = o[h] * lax.rsqrt(jnp.mean(o[h] * o[h], axis=-1, keepdims=True) + EPS) * ng
            gate = g_ref[rows, cols[h]].astype(F32)
            o_ref[rows, cols[h]] = (oh * (gate * jax.nn.sigmoid(gate))).astype(o_ref.dtype)
        return carry

    lax.fori_loop(0, n_chunks, step, 0)


def hgrn2(z, lb, norm_g, bsz, seq, width, col0, ts=1024):
    hd = HG_HEAD_DIM
    heads = width // hd
    cb = col0 // width
    ts = min(ts, seq)
    nt = seq // ts

    def zspec(k):
        return pl.BlockSpec((ts, width), lambda b, s: (b * nt + s, cb + k))

    return pl.pallas_call(
        _hgrn2_kernel,
        grid=(bsz, nt),
        in_specs=[zspec(0), zspec(1), zspec(2), zspec(3),
                  pl.BlockSpec((1, width), lambda b, s: (0, 0)),
                  pl.BlockSpec((1, hd), lambda b, s: (0, 0))],
        out_specs=pl.BlockSpec((ts, width), lambda b, s: (b * nt + s, 0)),
        out_shape=jax.ShapeDtypeStruct((bsz * seq, width), BF16),
        scratch_shapes=[pltpu.VMEM((heads, hd, hd), F32)],
        compiler_params=_cparams(("parallel", "arbitrary")),
        name="hgrn2",
    )(z, z, z, z, lb.reshape(1, width), norm_g.reshape(1, hd))


def _outproj_kernel(seq_tiles, ab_ref, ac_ref, ah_ref, hc_ref, hh_ref, cw_ref,
                    ob_ref, x_ref, wa_ref, wb_ref, o_ref):
    i = pl.program_id(0)
    u = ac_ref[...].astype(F32) * ah_ref[...].astype(F32)
    halo = hc_ref[...].astype(F32) * hh_ref[...].astype(F32)
    halo = jnp.where(i % seq_tiles == 0, 0.0, halo)
    hr = halo.shape[0]
    row = lax.broadcasted_iota(jnp.int32, u.shape, 0)
    u1 = jnp.where(row == 0, halo[hr - 1:hr, :], pltpu.roll(u, 1, axis=0))
    u2 = jnp.where(row == 0, halo[hr - 2:hr - 1, :],
                   jnp.where(row == 1, halo[hr - 1:hr, :], pltpu.roll(u, 2, axis=0)))
    cw = cw_ref[...]
    conv = cw[2:3, :] * u + cw[1:2, :] * u1 + cw[0:1, :] * u2
    ya = (ab_ref[...].astype(F32) * conv).astype(BF16)
    mix = jnp.dot(ya, wa_ref[...], preferred_element_type=F32)
    mix = mix + jnp.dot(ob_ref[...], wb_ref[...], preferred_element_type=F32)
    o_ref[...] = x_ref[...] + mix


def outproj_even(z, conv_w, o_b, x, w_out, seq, tm=256, halo=16):
    t, d = x.shape
    cw = conv_w.shape[1]
    hw = o_b.shape[1]
    hb = tm // halo
    kern = functools.partial(_outproj_kernel, seq // tm)
    return pl.pallas_call(
        kern,
        grid=(t // tm,),
        in_specs=[pl.BlockSpec((tm, cw), lambda i: (i, 0)),
                  pl.BlockSpec((tm, cw), lambda i: (i, 1)),
                  pl.BlockSpec((tm, cw), lambda i: (i, 2)),
                  pl.BlockSpec((halo, cw), lambda i: (jnp.maximum(i * hb - 1, 0), 1)),
                  pl.BlockSpec((halo, cw), lambda i: (jnp.maximum(i * hb - 1, 0), 2)),
                  pl.BlockSpec((CONV_K, cw), lambda i: (0, 0)),
                  pl.BlockSpec((tm, hw), lambda i: (i, 0)),
                  pl.BlockSpec((tm, d), lambda i: (i, 0)),
                  pl.BlockSpec((cw, d), lambda i: (0, 0)),
                  pl.BlockSpec((hw, d), lambda i: (1, 0))],
        out_specs=pl.BlockSpec((tm, d), lambda i: (i, 0)),
        out_shape=jax.ShapeDtypeStruct((t, d), F32),
        compiler_params=_cparams(("parallel",)),
        name="outproj_even",
    )(z, z, z, z, z, conv_w, o_b, x, w_out, w_out)


def _rep_rows(x, n):
    r, l = x.shape
    return jnp.broadcast_to(x[:, None, :], (r, n, l)).reshape(r * n, l)


def _tile_rows(x, n):
    r, l = x.shape
    return jnp.broadcast_to(x[None, :, :], (n, r, l)).reshape(n * r, l)


def _s5_kernel(n_chunks, u_ref, p_ref, y_ref, m_ref):
    lc = S5_CHUNK
    gc = S5_GROUP
    half = S5_STATE
    prm = p_ref[0]
    tab = [prm[i * gc:(i + 1) * gc, :] for i in range(8)]
    crd, cid, bqa, bqb, bwa, bwb, cga, cgb = tab
    alpha = prm[8 * gc:8 * gc + 1, :]
    beta = prm[8 * gc + 1:8 * gc + 2, :]

    def cpow(tt):
        mag = jnp.exp(alpha * tt)
        ang = beta * tt
        return mag * jnp.cos(ang), mag * jnp.sin(ang)

    t_col = lax.broadcasted_iota(jnp.int32, (lc, LANES), 0).astype(F32)
    lane = lax.broadcasted_iota(jnp.int32, (lc, LANES), 1)
    p0r, p0i = cpow(t_col)
    p1r, p1i = cpow(t_col + 1.0)
    prr, pri = cpow(float(lc - 1) - t_col)

    qcat = _rep_rows(crd, gc) * _tile_rows(bqa, gc) + _rep_rows(cid, gc) * _tile_rows(bqb, gc)
    p0cat = jnp.where(lane < half, p0r, p0i)
    p0cat2 = jnp.concatenate([p0cat, p0cat], axis=0)
    kc2 = lax.dot_general(qcat, p0cat2, (((1,), (1,)), ((), ())),
                          precision=lax.Precision.HIGHEST, preferred_element_type=F32)
    lane16 = lax.broadcasted_iota(jnp.int32, (gc, LANES), 1)
    k_row = jnp.concatenate(
        [jnp.where(lane16 < half, kc2[2 * j * gc:(2 * j + 1) * gc, :],
                   kc2[(2 * j + 1) * gc:(2 * j + 2) * gc, :]) for j in range(gc // 2)],
        axis=1)
    seg_pos = lax.broadcasted_iota(jnp.int32, k_row.shape, 1) % lc
    for s in range(lc):
        shifted = k_row if s == 0 else pltpu.roll(k_row, s, axis=1)
        m_ref[s * gc:(s + 1) * gc, :] = jnp.where(seg_pos >= s, shifted, 0.0).astype(BF16)

    u = u_ref[0]
    wcat = (_rep_rows(prr, gc) * _tile_rows(bwa, lc)
            + _rep_rows(pri, gc) * _tile_rows(bwb, lc)).astype(BF16)
    st = jnp.dot(u, wcat, preferred_element_type=F32)

    rows = st.shape[0]
    rpos = lax.broadcasted_iota(jnp.int32, (rows, LANES), 0) % n_chunks
    rlane = lax.broadcasted_iota(jnp.int32, (rows, LANES), 1)
    ar, ai = cpow(jnp.full((1, LANES), float(lc), F32))
    d = 1
    while d < n_chunks:
        sh = jnp.where(rpos >= d, pltpu.roll(st, d, axis=0), 0.0)
        a2 = jnp.where(rlane < half, -ai, ai)
        st = st + ar * sh + a2 * pltpu.roll(sh, half, axis=1)
        ar, ai = ar * ar - ai * ai, 2.0 * ar * ai
        d *= 2
    h0 = jnp.where(rpos >= 1, pltpu.roll(st, 1, axis=0), 0.0).astype(BF16)

    gcat = (_rep_rows(cga, lc) * _tile_rows(p1r, gc)
            + _rep_rows(cgb, lc) * _tile_rows(p1i, gc)).astype(BF16)
    y = jnp.dot(u, m_ref[...], preferred_element_type=F32)
    y = y + lax.dot_general(h0, gcat, (((1,), (1,)), ((), ())), preferred_element_type=F32)
    y_ref[0] = y.astype(y_ref.dtype)


def _s5_tables(lam_re, lam_im, log_dt, b_re, b_im, c_re, c_im):
    dt = jnp.exp(log_dt.astype(F32))[:, None]
    lr = lam_re.astype(F32)
    li = lam_im.astype(F32)
    mag = jnp.exp(lr * dt)
    ar = mag * jnp.cos(li * dt)
    ai = mag * jnp.sin(li * dt)
    den = lr * lr + li * li
    cr = ((ar - 1.0) * lr + ai * li) / den
    ci = (ai * lr - (ar - 1.0) * li) / den
    bbr = (cr[..., None] * b_re - ci[..., None] * b_im).transpose(0, 2, 1)
    bbi = (cr[..., None] * b_im + ci[..., None] * b_re).transpose(0, 2, 1)
    cre = c_re.astype(F32)
    cim = c_im.astype(F32)
    cat = lambda a, b: jnp.concatenate([a, b], axis=-1)
    alpha = (lr * dt)[:, None, :]
    beta = (li * dt)[:, None, :]
    pad = jnp.zeros((lr.shape[0], 6, 2 * lr.shape[1]), F32)
    return jnp.concatenate([
        cat(cre, cre), cat(cim, cim),
        cat(bbr, -bbi), cat(-bbi, -bbr),
        cat(bbr, bbi), cat(-bbi, bbr),
        cat(cre, -cim), cat(-cim, -cre),
        cat(alpha, alpha), cat(beta, beta), pad], axis=1)


def s5_mix(h, tables, bsz, seq):
    t, w = h.shape
    lc, gc = S5_CHUNK, S5_GROUP
    groups = w // gc
    n = seq // lc
    rows = bsz * n
    u = h.reshape(bsz, n, lc, groups, gc).transpose(3, 0, 1, 2, 4).reshape(groups, rows, lc * gc)
    y = pl.pallas_call(
        functools.partial(_s5_kernel, n),
        grid=(groups,),
        in_specs=[pl.BlockSpec((1, rows, lc * gc), lambda g: (g, 0, 0)),
                  pl.BlockSpec((1,) + tables.shape[1:], lambda g: (g, 0, 0))],
        out_specs=pl.BlockSpec((1, rows, lc * gc), lambda g: (g, 0, 0)),
        out_shape=jax.ShapeDtypeStruct((groups, rows, lc * gc), BF16),
        scratch_shapes=[pltpu.VMEM((lc * gc, lc * gc), BF16)],
        compiler_params=_cparams(("parallel",)),
        name="s5_mix",
    )(u, tables)
    return y.reshape(groups, bsz, n, gc, lc).transpose(1, 2, 4, 0, 3).reshape(t, w)


def _glu_kernel(y_ref, h_ref, d_ref, x_ref, wa_ref, wb_ref, o_ref, act_ref):
    @pl.when(pl.program_id(1) == 0)
    def _():
        y = y_ref[...].astype(F32) + d_ref[...] * h_ref[...].astype(F32)
        cdf = 0.5 * (1.0 + jnp.tanh(math.sqrt(2.0 / math.pi) * (y + 0.044715 * (y * y * y))))
        act_ref[...] = (y * cdf).astype(BF16)

    act = act_ref[...]
    za = jnp.dot(act, wa_ref[...], preferred_element_type=F32)
    zb = jnp.dot(act, wb_ref[...], preferred_element_type=F32)
    o_ref[...] = x_ref[...] + za * jax.nn.sigmoid(zb)


def glu_residual(y, h, d_skip, x, w_glu, tm=1024, tn=512):
    t, d = x.shape
    w = y.shape[1]
    tm = min(tm, t)
    nb = d // tn
    return pl.pallas_call(
        _glu_kernel,
        grid=(t // tm, nb),
        in_specs=[pl.BlockSpec((tm, w), lambda i, j: (i, 0)),
                  pl.BlockSpec((tm, w), lambda i, j: (i, 0)),
                  pl.BlockSpec((1, w), lambda i, j: (0, 0)),
                  pl.BlockSpec((tm, tn), lambda i, j: (i, j)),
                  pl.BlockSpec((w, tn), lambda i, j: (0, j)),
                  pl.BlockSpec((w, tn), lambda i, j: (0, j + nb))],
        out_specs=pl.BlockSpec((tm, tn), lambda i, j: (i, j)),
        out_shape=jax.ShapeDtypeStruct((t, d), F32),
        scratch_shapes=[pltpu.VMEM((tm, w), BF16)],
        compiler_params=_cparams(("parallel", "arbitrary")),
        name="glu_residual",
    )(y, h, d_skip.reshape(1, w), x, w_glu, w_glu)


def _router_kernel(x_ref, g_ref, w_ref, b_ref, e_ref, p_ref, r_ref, c_ref, cnt_ref):
    tm = x_ref.shape[0]

    @pl.when(pl.program_id(0) == 0)
    def _():
        cnt_ref[...] = jnp.zeros_like(cnt_ref)

    h = _rms(x_ref[...], g_ref[...])
    nt_dot = lambda a, b: lax.dot_general(a, b, (((1,), (1,)), ((), ())),
                                          preferred_element_type=F32)
    w_hi, w_lo = _split2(w_ref[...])
    h_hi, h_lo = _split2(h)
    logits = (nt_dot(w_hi, h_hi) + nt_dot(w_hi, h_lo) + nt_dot(w_lo, h_hi)
              + b_ref[...])
    lc = [logits[i:i + 1, :] for i in range(MOE_GROUPS)]
    m = functools.reduce(jnp.maximum, lc)
    grp = jnp.full(m.shape, MOE_GROUPS - 1, jnp.int32)
    for i in range(MOE_GROUPS - 2, -1, -1):
        grp = jnp.where(lc[i] == m, i, grp)
    den = functools.reduce(lambda a, b: a + b, [jnp.exp(l - m) for l in lc])
    p_top = 1.0 / den
    fine = [logits[MOE_GROUPS + j:MOE_GROUPS + j + 1, :] for j in range(MOE_EXPERTS)]
    sel = []
    for e in range(MOE_EPG):
        v = fine[(MOE_GROUPS - 1) * MOE_EPG + e]
        for i in range(MOE_GROUPS - 2, -1, -1):
            v = jnp.where(grp == i, fine[i * MOE_EPG + e], v)
        sel.append(v)

    def top1(vals):
        best = functools.reduce(jnp.maximum, vals)
        idx = jnp.full(best.shape, MOE_EPG - 1, jnp.int32)
        for e in range(MOE_EPG - 2, -1, -1):
            idx = jnp.where(vals[e] == best, e, idx)
        return best, idx

    v1, i1 = top1(sel)
    v2, i2 = top1([jnp.where(i1 == e, -jnp.inf, sel[e]) for e in range(MOE_EPG)])
    ex = jnp.exp(v2 - v1)
    s = 1.0 + ex
    e0 = grp * MOE_EPG + i1
    e1 = grp * MOE_EPG + i2
    e_ref[0:1, :] = e0
    e_ref[1:2, :] = e1
    p_ref[0:1, :] = p_top * (1.0 / s)
    p_ref[1:2, :] = p_top * (ex / s)

    eid = lax.broadcasted_iota(jnp.int32, (MOE_EXPERTS, tm), 0)
    tri = (lax.broadcasted_iota(jnp.int32, (tm, tm), 0)
           <= lax.broadcasted_iota(jnp.int32, (tm, tm), 1)).astype(BF16)
    base = cnt_ref[...]
    for k, ek in enumerate((e0, e1)):
        hot = eid == ek
        hot_f = hot.astype(F32)
        csum = jnp.dot(hot.astype(BF16), tri, preferred_element_type=F32)
        rank = jnp.sum(hot_f * (base + csum - 1.0), axis=0, keepdims=True)
        r_ref[k:k + 1, :] = rank.astype(jnp.int32)
        base = base + jnp.sum(hot_f, axis=1, keepdims=True)
    cnt_ref[...] = base
    c_ref[...] = base.astype(jnp.int32)


def moe_router(x, g, w_coarse, b_coarse, w_fine, b_fine, tm=512):
    t, d = x.shape
    pad = ROUTER_ROWS - MOE_GROUPS - MOE_EXPERTS
    w = jnp.concatenate([w_coarse.T, w_fine.T, jnp.zeros((pad, d), F32)], axis=0)
    b = jnp.concatenate([b_coarse, b_fine, jnp.zeros((pad,), F32)]).reshape(ROUTER_ROWS, 1)
    tok_spec = pl.BlockSpec((2, tm), lambda i: (0, i))
    return pl.pallas_call(
        _router_kernel,
        grid=(t // tm,),
        in_specs=[pl.BlockSpec((tm, d), lambda i: (i, 0)),
                  pl.BlockSpec((1, d), lambda i: (0, 0)),
                  pl.BlockSpec((ROUTER_ROWS, d), lambda i: (0, 0)),
                  pl.BlockSpec((ROUTER_ROWS, 1), lambda i: (0, 0))],
        out_specs=[tok_spec, tok_spec, tok_spec,
                   pl.BlockSpec((MOE_EXPERTS, 1), lambda i: (0, 0))],
        out_shape=[jax.ShapeDtypeStruct((2, t), jnp.int32),
                   jax.ShapeDtypeStruct((2, t), F32),
                   jax.ShapeDtypeStruct((2, t), jnp.int32),
                   jax.ShapeDtypeStruct((MOE_EXPERTS, 1), jnp.int32)],
        scratch_shapes=[pltpu.VMEM((MOE_EXPERTS, 1), F32)],
        compiler_params=_cparams(("arbitrary",)),
        name="moe_router",
    )(x, g.reshape(1, d), w, b)


def _dispatch_plan(experts, rank, counts):
    n_assign = experts.size
    counts = counts.reshape(-1)
    padded = (counts + MOE_ROWS - 1) // MOE_ROWS * MOE_ROWS
    pend = jnp.cumsum(padded)
    pstart = pend - padded
    eids = jnp.arange(MOE_EXPERTS, dtype=jnp.int32)
    dest = rank + jnp.sum(jnp.where(experts[..., None] == eids, pstart, 0), axis=-1)
    n_groups = -(-n_assign // MOE_ROWS) + MOE_EXPERTS
    g0 = jnp.arange(n_groups, dtype=jnp.int32) * MOE_ROWS
    grp_expert = jnp.minimum(jnp.sum(pend[None, :] <= g0[:, None], axis=1), MOE_EXPERTS - 1)
    n_active = (pend[-1] // MOE_ROWS).reshape(1)
    grp_first = g0 == jnp.sum(jnp.where(grp_expert[:, None] == eids, pstart, 0), axis=1)
    used = counts > 0
    e_slot = (jnp.cumsum(used) - 1) % 2
    later_used = jnp.logical_and(used[None, :], eids[None, :] > eids[:, None])
    e_next = jnp.min(jnp.where(later_used, eids[None, :], MOE_EXPERTS), axis=1)
    e_next = jnp.where(e_next == MOE_EXPERTS, -1, e_next)
    i32 = lambda a: a.astype(jnp.int32)
    runs = (i32(grp_expert), i32(n_active), i32(grp_first), i32(e_slot), i32(e_next))
    return i32(dest).reshape(-1), runs, i32(pstart + counts), i32(padded - counts)


def _dispatch_kernel(dest_ref, ps_ref, pn_ref, na_ref, x_ref, g_ref, xs_hbm, hbuf, zbuf, sem,
                     zsem):
    i = pl.program_id(0)
    n = pl.num_programs(0)
    tm = x_ref.shape[0]
    t = n * tm
    slot = i % 2

    def wait_slot(s):
        for _ in range(2):
            pltpu.make_async_copy(hbuf.at[s], xs_hbm.at[pl.ds(0, tm)], sem.at[s]).wait()

    def zero_row_copy(row):
        return pltpu.make_async_copy(zbuf.at[pl.ds(0, 1)], xs_hbm.at[pl.ds(row, 1)], zsem.at[0])

    def for_each_pad_row(fn):
        def per_expert(e, c):
            def per_row(r, c2):
                fn(ps_ref[e] + r)
                return c2
            return lax.fori_loop(0, pn_ref[e], per_row, c)
        lax.fori_loop(0, ps_ref.shape[0], per_expert, 0)

    def tail_group_copy(grp):
        rows = pl.ds(pl.multiple_of(grp * tm, tm), tm)
        return pltpu.make_async_copy(hbuf.at[1], xs_hbm.at[rows], sem.at[1])

    def for_each_tail_group(fn):
        def body(grp, c):
            fn(grp)
            return c
        lax.fori_loop(na_ref[0], xs_hbm.shape[0] // tm, body, 0)

    @pl.when(i == 0)
    def _():
        zbuf[...] = jnp.zeros_like(zbuf)
        for_each_pad_row(lambda row: zero_row_copy(row).start())
        hbuf[1] = jnp.zeros(hbuf.shape[1:], hbuf.dtype)
        for_each_tail_group(lambda grp: tail_group_copy(grp).start())
        for_each_tail_group(lambda grp: tail_group_copy(grp).wait())

    @pl.when(i >= 2)
    def _():
        wait_slot(slot)

    hbuf[slot] = _rms(x_ref[...], g_ref[...])

    def body(r, c):
        for k in range(2):
            row = dest_ref[k * t + i * tm + r]
            pltpu.make_async_copy(hbuf.at[slot, pl.ds(r, 1)], xs_hbm.at[pl.ds(row, 1)],
                                  sem.at[slot]).start()
        return c
    lax.fori_loop(0, tm, body, 0, unroll=8)

    @pl.when(i == n - 1)
    def _():
        wait_slot(slot)

        @pl.when(n >= 2)
        def _():
            wait_slot(1 - slot)

        for_each_pad_row(lambda row: zero_row_copy(row).wait())


def moe_dispatch(x, g, dest, pad_start, pad_len, n_active, total):
    t, d = x.shape
    tm = MOE_ROWS
    grid_spec = pltpu.PrefetchScalarGridSpec(
        num_scalar_prefetch=4,
        grid=(t // tm,),
        in_specs=[pl.BlockSpec((tm, d), lambda i, *_: (i, 0)),
                  pl.BlockSpec((1, d), lambda i, *_: (0, 0))],
        out_specs=pl.BlockSpec(memory_space=pl.ANY),
        scratch_shapes=[pltpu.VMEM((2, tm, d), F32),
                        pltpu.VMEM((8, d), F32),
                        pltpu.SemaphoreType.DMA((2,)),
                        pltpu.SemaphoreType.DMA((1,))],
    )
    return pl.pallas_call(
        _dispatch_kernel,
        grid_spec=grid_spec,
        out_shape=jax.ShapeDtypeStruct((total, d), F32),
        compiler_params=_cparams(("arbitrary",)),
        name="moe_dispatch",
    )(dest, pad_start, pad_len, n_active, x, g.reshape(1, d))


def _expert_weights_step(layer, n_mats, ge_ref, na_ref, first_ref, slot_ref, next_ref,
                         w_hbm, wbuf, wb16, sem):
    g = pl.program_id(0)
    e = ge_ref[g]

    def copies(expert, slot):
        return [pltpu.make_async_copy(w_hbm[m].at[layer, expert], wbuf.at[slot, m],
                                      sem.at[slot, m]) for m in range(n_mats)]

    @pl.when(jnp.logical_and(g < na_ref[0], first_ref[g] == 1))
    def _():
        slot = slot_ref[e]

        @pl.when(g == 0)
        def _():
            for c in copies(e, slot):
                c.start()

        for c in copies(e, slot):
            c.wait()
        nxt = next_ref[e]

        @pl.when(nxt >= 0)
        def _():
            for c in copies(nxt, 1 - slot):
                c.start()

        rows = wbuf.shape[2]
        step = 256
        for m in range(n_mats):
            for r in range(0, rows, step):
                wb16[m, r:r + step, :] = wbuf[slot, m, r:r + step, :].astype(BF16)


def _experts_up_kernel(layer, ge_ref, na_ref, first_ref, slot_ref, next_ref,
                       xs_ref, wg_hbm, wu_hbm, h_ref, wbuf, wb16, sem):
    _expert_weights_step(layer, 2, ge_ref, na_ref, first_ref, slot_ref, next_ref,
                         (wg_hbm, wu_hbm), wbuf, wb16, sem)
    g = pl.program_id(0)

    @pl.when(g < na_ref[0])
    def _():
        xb = xs_ref[...].astype(BF16)
        a = jnp.dot(xb, wb16[0], preferred_element_type=F32)
        b = jnp.dot(xb, wb16[1], preferred_element_type=F32)
        h_ref[...] = (a * jax.nn.sigmoid(a) * b).astype(h_ref.dtype)

    @pl.when(g >= na_ref[0])
    def _():
        h_ref[...] = jnp.zeros_like(h_ref)


def _experts_down_kernel(layer, ge_ref, na_ref, first_ref, slot_ref, next_ref,
                         h_ref, wd_hbm, o_ref, wbuf, wb16, sem):
    _expert_weights_step(layer, 1, ge_ref, na_ref, first_ref, slot_ref, next_ref,
                         (wd_hbm,), wbuf, wb16, sem)
    g = pl.program_id(0)

    @pl.when(g < na_ref[0])
    def _():
        o_ref[...] = jnp.dot(h_ref[...], wb16[0], preferred_element_type=F32)

    @pl.when(g >= na_ref[0])
    def _():
        o_ref[...] = jnp.zeros_like(o_ref)


def moe_experts(xs, runs, w_gate, w_up, w_down, layer):
    total, d = xs.shape
    ff = w_gate.shape[3]
    n_groups = total // MOE_ROWS
    n_pref = len(runs)

    def blk(g, ge, na, *_):
        return (jnp.maximum(jnp.minimum(g, na[0] - 1), 0), 0)

    def call(body, n_w, in_cols, w_rows, w_cols, out_dtype, name):
        return pl.pallas_call(
            functools.partial(body, layer),
            grid_spec=pltpu.PrefetchScalarGridSpec(
                num_scalar_prefetch=n_pref, grid=(n_groups,),
                in_specs=[pl.BlockSpec((MOE_ROWS, in_cols), blk)]
                + [pl.BlockSpec(memory_space=pl.ANY)] * n_w,
                out_specs=pl.BlockSpec((MOE_ROWS, w_cols), lambda g, *_: (g, 0)),
                scratch_shapes=[pltpu.VMEM((2, n_w, w_rows, w_cols), F32),
                                pltpu.VMEM((n_w, w_rows, w_cols), BF16),
                                pltpu.SemaphoreType.DMA((2, n_w))]),
            out_shape=jax.ShapeDtypeStruct((total, w_cols), out_dtype),
            compiler_params=_cparams(("arbitrary",)),
            name=name)

    hid = call(_experts_up_kernel, 2, d, d, ff, BF16, "moe_experts_up")(*runs, xs, w_gate, w_up)
    return call(_experts_down_kernel, 1, ff, ff, d, F32, "moe_experts_down")(*runs, hid, w_down)


def _combine_kernel(final_norm, pos_ref, yb_hbm, x_ref, p_ref, g_ref, o_ref, ybuf, sem):
    i = pl.program_id(0)
    n = pl.num_programs(0)
    tm = x_ref.shape[0]
    t = n * tm

    def issue(tile, slot):
        def body(r, c):
            for k in range(2):
                row = pos_ref[k * t + tile * tm + r]
                pltpu.make_async_copy(yb_hbm.at[pl.ds(row, 1)], ybuf.at[slot, k, pl.ds(r, 1)],
                                      sem.at[slot]).start()
            return c
        lax.fori_loop(0, tm, body, 0, unroll=8)

    @pl.when(i == 0)
    def _():
        issue(0, 0)

    @pl.when(i + 1 < n)
    def _():
        issue(i + 1, (i + 1) % 2)

    slot = i % 2
    for k in range(2):
        pltpu.make_async_copy(yb_hbm.at[pl.ds(0, tm)], ybuf.at[slot, k], sem.at[slot]).wait()
    p = p_ref[...]
    out = x_ref[...] + (p[:, 0:1] * ybuf[slot, 0] + p[:, 1:2] * ybuf[slot, 1])
    if final_norm:
        out = _rms(out, g_ref[...])
    o_ref[...] = out


def moe_combine(yb, dest, gates_t, x, g_final=None, tm=256):
    t, d = x.shape
    final_norm = g_final is not None
    g = (g_final if final_norm else jnp.ones((d,), F32)).reshape(1, d)
    grid_spec = pltpu.PrefetchScalarGridSpec(
        num_scalar_prefetch=1,
        grid=(t // tm,),
        in_specs=[pl.BlockSpec(memory_space=pl.ANY),
                  pl.BlockSpec((tm, d), lambda i, pos: (i, 0)),
                  pl.BlockSpec((tm, 2), lambda i, pos: (i, 0)),
                  pl.BlockSpec((1, d), lambda i, pos: (0, 0))],
        out_specs=pl.BlockSpec((tm, d), lambda i, pos: (i, 0)),
        scratch_shapes=[pltpu.VMEM((2, 2, tm, d), F32),
                        pltpu.SemaphoreType.DMA((2,))],
    )
    return pl.pallas_call(
        functools.partial(_combine_kernel, final_norm),
        grid_spec=grid_spec,
        out_shape=jax.ShapeDtypeStruct((t, d), F32),
        compiler_params=_cparams(("arbitrary",)),
        name="moe_combine",
    )(dest, yb, x, gates_t, g)


def hier_moe_residual(x, g_ffn, w_coarse, b_coarse, w_fine, b_fine, w_gate, w_up, w_down,
                      layer, g_final=None):
    experts, gates, rank, counts = moe_router(x, g_ffn, w_coarse, b_coarse, w_fine, b_fine)
    dest, runs, pad_start, pad_len = _dispatch_plan(experts, rank, counts)
    total = runs[0].shape[0] * MOE_ROWS
    xs = moe_dispatch(x, g_ffn, dest, pad_start, pad_len, runs[1], total)
    yb = moe_experts(xs, runs, w_gate, w_up, w_down, layer)
    return moe_combine(yb, dest, gates.T, x, g_final)


def kernel(x, g_mix, g_ffn, g_final, w_in_even, conv_w, hg_lower_bound, hg_norm_g, w_out_even, s5_lambda_re, s5_lambda_im, s5_log_dt, s5_b_re, s5_b_im, s5_c_re, s5_c_im, s5_d, w_glu, moe_w_coarse, moe_b_coarse, moe_w_fine, moe_b_fine, moe_w_gate, moe_w_up, moe_w_down):
    bsz, seq, d = x.shape
    depth = g_mix.shape[0]
    conv_width = conv_w.shape[2]
    hg_width = hg_lower_bound.shape[1]
    lb_all = jnp.cumsum(jax.nn.softmax(hg_lower_bound.astype(F32), axis=0), axis=0)
    xf = x.reshape(bsz * seq, d)
    for layer in range(depth):
        j = layer // 2
        if layer % 2 == 0:
            z = norm_matmul(xf, g_mix[layer], w_in_even[j].astype(BF16))
            o_b = hgrn2(z, lb_all[j], hg_norm_g[j], bsz, seq, hg_width, 3 * conv_width)
            xf = outproj_even(z, conv_w[j], o_b, xf, w_out_even[j].astype(BF16), seq)
        else:
            h = rmsnorm_cast(xf, g_mix[layer])
            tables = _s5_tables(s5_lambda_re[j], s5_lambda_im[j], s5_log_dt[j],
                                s5_b_re[j], s5_b_im[j], s5_c_re[j], s5_c_im[j])
            y = s5_mix(h, tables, bsz, seq)
            xf = glu_residual(y, h, s5_d[j], xf, w_glu[j].astype(BF16))
        xf = hier_moe_residual(
            xf, g_ffn[layer], moe_w_coarse[layer], moe_b_coarse[layer], moe_w_fine[layer],
            moe_b_fine[layer], moe_w_gate, moe_w_up, moe_w_down, layer,
            g_final if layer == depth - 1 else None)
    return xf.reshape(bsz, seq, d)
```

```python
import functools
import math

import jax
import jax.numpy as jnp
from jax import lax
from jax.experimental import pallas as pl
from jax.experimental.pallas import tpu as pltpu

F32 = jnp.float32
BF16 = jnp.bfloat16
EPS = 1e-6

LANES = 128
HG_HEAD_DIM = 128
HG_CHUNK = 64
CONV_K = 3
S5_GROUP = 16
S5_STATE = 64
S5_CHUNK = 64
MOE_GROUPS = 4
MOE_EPG = 8
MOE_EXPERTS = MOE_GROUPS * MOE_EPG
MOE_ROWS = 256
ROUTER_ROWS = 40
VMEM_LIMIT = 56 * 1024 * 1024


def _cparams(sem, vmem=VMEM_LIMIT):
    return pltpu.CompilerParams(dimension_semantics=sem, vmem_limit_bytes=vmem)


def _rms(x, g):
    ms = jnp.mean(x * x, axis=-1, keepdims=True)
    return x * lax.rsqrt(ms + EPS) * g


def _norm_kernel(x_ref, g_ref, o_ref):
    o_ref[...] = _rms(x_ref[...], g_ref[...]).astype(o_ref.dtype)


def rmsnorm_cast(x, g, dtype, tm=512):
    t, d = x.shape
    return pl.pallas_call(
        _norm_kernel,
        grid=(t // tm,),
        in_specs=[pl.BlockSpec((tm, d), lambda i: (i, 0)),
                  pl.BlockSpec((1, d), lambda i: (0, 0))],
        out_specs=pl.BlockSpec((tm, d), lambda i: (i, 0)),
        out_shape=jax.ShapeDtypeStruct((t, d), dtype),
        compiler_params=_cparams(("parallel",)),
        name="rmsnorm_cast",
    )(x, g.reshape(1, d))


def _norm_matmul_kernel(x_ref, g_ref, w_ref, o_ref, h_ref):
    @pl.when(pl.program_id(1) == 0)
    def _():
        h_ref[...] = _rms(x_ref[...], g_ref[...]).astype(BF16)

    o_ref[...] = jnp.dot(h_ref[...], w_ref[...],
                         preferred_element_type=F32).astype(o_ref.dtype)


def norm_matmul(x, g, w, tm=1024, tn=1024):
    t, d = x.shape
    n = w.shape[1]
    tm = min(tm, t)
    return pl.pallas_call(
        _norm_matmul_kernel,
        grid=(t // tm, n // tn),
        in_specs=[pl.BlockSpec((tm, d), lambda i, j: (i, 0)),
                  pl.BlockSpec((1, d), lambda i, j: (0, 0)),
                  pl.BlockSpec((d, tn), lambda i, j: (0, j))],
        out_specs=pl.BlockSpec((tm, tn), lambda i, j: (i, j)),
        out_shape=jax.ShapeDtypeStruct((t, n), BF16),
        scratch_shapes=[pltpu.VMEM((tm, d), BF16)],
        compiler_params=_cparams(("parallel", "arbitrary")),
        name="norm_matmul",
    )(x, g.reshape(1, d), w)


def _split2(x):
    hi = x.astype(BF16)
    return hi, (x - hi.astype(F32)).astype(BF16)


def _split3(x):
    hi = x.astype(BF16)
    r1 = x - hi.astype(F32)
    mid = r1.astype(BF16)
    lo = (r1 - mid.astype(F32)).astype(BF16)
    return hi, mid, lo


def _hgrn2_kernel(q_ref, f_ref, v_ref, g_ref, lb_ref, ng_ref, o_ref, st_ref):
    c = HG_CHUNK
    hd = HG_HEAD_DIM
    heads = st_ref.shape[0]
    n_chunks = q_ref.shape[0] // c
    ng = ng_ref[...]
    row = lax.broadcasted_iota(jnp.int32, (c, c), 0)
    col = lax.broadcasted_iota(jnp.int32, (c, c), 1)
    causal = row >= col
    tril = causal.astype(BF16)
    mid = c // 2

    @pl.when(pl.program_id(1) == 0)
    def _():
        st_ref[...] = jnp.zeros_like(st_ref)

    nt_dims = (((1,), (1,)), ((), ()))
    hs = range(heads)
    cols = [slice(h * hd, (h + 1) * hd) for h in hs]

    def step(n, carry):
        rows = pl.ds(pl.multiple_of(n * c, c), c)
        q = [q_ref[rows, cl].astype(F32) for cl in cols]
        f = [lb_ref[:, cl] + (1.0 - lb_ref[:, cl]) * jax.nn.sigmoid(f_ref[rows, cl].astype(F32))
             for cl in cols]
        k = [1.0 - fh for fh in f]
        parts = [_split3(jnp.log(fh)) for fh in f]
        b = [sum(jnp.dot(tril, p, preferred_element_type=F32) for p in ph) for ph in parts]
        ref = [bh[mid - 1:mid, :] for bh in b]
        b_last = [bh[c - 1:c, :] for bh in b]
        st = [st_ref[h] for h in hs]
        qs = [(q[h] * jnp.exp(b[h] - ref[h])).astype(BF16) for h in hs]
        ks = [(k[h] * jnp.exp(ref[h] - b[h])).astype(BF16) for h in hs]
        scores = [lax.dot_general(qs[h], ks[h], nt_dims, preferred_element_type=F32) for h in hs]
        qe = [(q[h] * jnp.exp(b[h])).astype(BF16) for h in hs]
        o_inter = [lax.dot_general(qe[h], st[h].astype(BF16), nt_dims, preferred_element_type=F32)
                   for h in hs]
        kd = [(k[h] * jnp.exp(b_last[h] - b[h])).astype(BF16) for h in hs]
        upd = [lax.dot_general(v_ref[rows, cols[h]], kd[h], (((0,), (0,)), ((), ())),
                               preferred_element_type=F32) for h in hs]
        for h in hs:
            st_ref[h] = st[h] * jnp.exp(b_last[h]) + upd[h]
        sc = [jnp.where(causal, s, 0.0).astype(BF16) for s in scores]
        o = [o_inter[h] + jnp.dot(sc[h], v_ref[rows, cols[h]], preferred_element_type=F32)
             for h in hs]
        for h in hs:
            oh = o[h] * lax.rsqrt(jnp.mean(o[h] * o[h], axis=-1, keepdims=True) + EPS) * ng
            gate = g_ref[rows, cols[h]].astype(F32)
            o_ref[rows, cols[h]] = (oh * (gate * jax.nn.sigmoid(gate))).astype(o_ref.dtype)
        return carry

    lax.fori_loop(0, n_chunks, step, 0)


def hgrn2(z, lb, norm_g, bsz, seq, width, col0, ts=1024):
    hd = HG_HEAD_DIM
    heads = width // hd
    cb = col0 // width
    ts = min(ts, seq)
    nt = seq // ts

    def zspec(k):
        return pl.BlockSpec((ts, width), lambda b, s: (b * nt + s, cb + k))

    return pl.pallas_call(
        _hgrn2_kernel,
        grid=(bsz, nt),
        in_specs=[zspec(0), zspec(1), zspec(2), zspec(3),
                  pl.BlockSpec((1, width), lambda b, s: (0, 0)),
                  pl.BlockSpec((1, hd), lambda b, s: (0, 0))],
        out_specs=pl.BlockSpec((ts, width), lambda b, s: (b * nt + s, 0)),
        out_shape=jax.ShapeDtypeStruct((bsz * seq, width), BF16),
        scratch_shapes=[pltpu.VMEM((heads, hd, hd), F32)],
        compiler_params=_cparams(("parallel", "arbitrary")),
        name="hgrn2",
    )(z, z, z, z, lb.reshape(1, width), norm_g.reshape(1, hd))


def _outproj_kernel(seq_tiles, ab_ref, ac_ref, ah_ref, hc_ref, hh_ref, cw_ref,
                    ob_ref, x_ref, wa_ref, wb_ref, o_ref):
    i = pl.program_id(0)
    u = ac_ref[...].astype(F32) * ah_ref[...].astype(F32)
    halo = hc_ref[...].astype(F32) * hh_ref[...].astype(F32)
    halo = jnp.where(i % seq_tiles == 0, 0.0, halo)
    hr = halo.shape[0]
    row = lax.broadcasted_iota(jnp.int32, u.shape, 0)
    u1 = jnp.where(row == 0, halo[hr - 1:hr, :], pltpu.roll(u, 1, axis=0))
    u2 = jnp.where(row == 0, halo[hr - 2:hr - 1, :],
                   jnp.where(row == 1, halo[hr - 1:hr, :], pltpu.roll(u, 2, axis=0)))
    cw = cw_ref[...]
    conv = cw[2:3, :] * u + cw[1:2, :] * u1 + cw[0:1, :] * u2
    ya = (ab_ref[...].astype(F32) * conv).astype(BF16)
    mix = jnp.dot(ya, wa_ref[...], preferred_element_type=F32)
    mix = mix + jnp.dot(ob_ref[...], wb_ref[...], preferred_element_type=F32)
    o_ref[...] = x_ref[...] + mix


def outproj_even(z, conv_w, o_b, x, w_out, seq, tm=256, halo=16):
    t, d = x.shape
    cw = conv_w.shape[1]
    hw = o_b.shape[1]
    hb = tm // halo
    kern = functools.partial(_outproj_kernel, seq // tm)
    return pl.pallas_call(
        kern,
        grid=(t // tm,),
        in_specs=[pl.BlockSpec((tm, cw), lambda i: (i, 0)),
                  pl.BlockSpec((tm, cw), lambda i: (i, 1)),
                  pl.BlockSpec((tm, cw), lambda i: (i, 2)),
                  pl.BlockSpec((halo, cw), lambda i: (jnp.maximum(i * hb - 1, 0), 1)),
                  pl.BlockSpec((halo, cw), lambda i: (jnp.maximum(i * hb - 1, 0), 2)),
                  pl.BlockSpec((CONV_K, cw), lambda i: (0, 0)),
                  pl.BlockSpec((tm, hw), lambda i: (i, 0)),
                  pl.BlockSpec((tm, d), lambda i: (i, 0)),
                  pl.BlockSpec((cw, d), lambda i: (0, 0)),
                  pl.BlockSpec((hw, d), lambda i: (1, 0))],
        out_specs=pl.BlockSpec((tm, d), lambda i: (i, 0)),
        out_shape=jax.ShapeDtypeStruct((t, d), F32),
        compiler_params=_cparams(("parallel",)),
        name="outproj_even",
    )(z, z, z, z, z, conv_w, o_b, x, w_out, w_out)


def _rep_rows(x, n):
    r, l = x.shape
    return jnp.broadcast_to(x[:, None, :], (r, n, l)).reshape(r * n, l)


def _tile_rows(x, n):
    r, l = x.shape
    return jnp.broadcast_to(x[None, :, :], (n, r, l)).reshape(n * r, l)


def _s5_group(n_chunks, prm, u, m_ref):
    lc = S5_CHUNK
    gc = S5_GROUP
    half = S5_STATE
    nt_dims = (((1,), (1,)), ((), ()))
    bqa, ce1, ce2, bwa, bwb, cga, cgb = (prm[i * gc:(i + 1) * gc, :] for i in range(7))
    alpha = prm[8 * gc:8 * gc + 1, :]
    beta = prm[8 * gc + 1:8 * gc + 2, :]

    def cpow(tt):
        mag = jnp.exp(alpha * tt)
        ang = beta * tt
        return mag * jnp.cos(ang), mag * jnp.sin(ang)

    t_col = lax.broadcasted_iota(jnp.int32, (lc, LANES), 0).astype(F32)
    p0r, p0i = cpow(t_col)
    prr, pri = cpow(float(lc - 1) - t_col)
    a1r, a1i = cpow(jnp.full((1, LANES), 1.0, F32))
    p1r, p1i = p0r * a1r - p0i * a1i, p0r * a1i + p0i * a1r

    ecat = _rep_rows(p0r, gc) * _tile_rows(ce1, lc) + _rep_rows(p0i, gc) * _tile_rows(ce2, lc)
    k_row = lax.dot_general(bqa, ecat, nt_dims, precision=lax.Precision.HIGHEST,
                            preferred_element_type=F32)
    pos = lax.broadcasted_iota(jnp.int32, k_row.shape, 1)
    for s in range(lc):
        shifted = k_row if s == 0 else pltpu.roll(k_row, s * gc, axis=1)
        m_ref[s * gc:(s + 1) * gc, :] = jnp.where(pos >= s * gc, shifted, 0.0).astype(BF16)

    wcat = (_rep_rows(prr, gc) * _tile_rows(bwa, lc)
            + _rep_rows(pri, gc) * _tile_rows(bwb, lc)).astype(BF16)
    st = jnp.dot(u, wcat, preferred_element_type=F32)

    rows = st.shape[0]
    rpos = lax.broadcasted_iota(jnp.int32, (rows, LANES), 0) % n_chunks
    rlane = lax.broadcasted_iota(jnp.int32, (rows, LANES), 1)
    ar, ai = cpow(jnp.full((1, LANES), float(lc), F32))
    d = 1
    while d < n_chunks:
        sh = jnp.where(rpos >= d, pltpu.roll(st, d, axis=0), 0.0)
        a2 = jnp.where(rlane < half, -ai, ai)
        st = st + ar * sh + a2 * pltpu.roll(sh, half, axis=1)
        ar, ai = ar * ar - ai * ai, 2.0 * ar * ai
        d *= 2
    h0 = jnp.where(rpos >= 1, pltpu.roll(st, 1, axis=0), 0.0).astype(BF16)

    gcat = (_rep_rows(p1r, gc) * _tile_rows(cga, lc)
            + _rep_rows(p1i, gc) * _tile_rows(cgb, lc)).astype(BF16)
    y = jnp.dot(u, m_ref[...], preferred_element_type=F32)
    return y + lax.dot_general(h0, gcat, nt_dims, preferred_element_type=F32)


def _s5_kernel(n_chunks, h_ref, p_ref, perm_ref, y_ref, u_ref, m_ref):
    lc = S5_CHUNK
    gps = u_ref.shape[0]
    rows = u_ref.shape[1]
    spv = LANES // S5_GROUP
    lane_blk = lambda k: slice(k * LANES, (k + 1) * LANES)

    for j in range(lc // spv):
        xcat = jnp.concatenate(
            [h_ref[pl.ds(j * spv + s, rows, stride=lc), :] for s in range(spv)], axis=1)
        uall = jnp.dot(xcat.astype(BF16), perm_ref[...], preferred_element_type=F32).astype(BF16)
        for gi in range(gps):
            u_ref[gi, :, lane_blk(j)] = uall[:, lane_blk(gi)]

    def group(gi, carry):
        y = _s5_group(n_chunks, p_ref[gi], u_ref[gi], m_ref)
        u_ref[gi] = y.astype(BF16)
        return carry
    lax.fori_loop(0, gps, group, 0)

    for j in range(lc // spv):
        ycat = jnp.concatenate([u_ref[gi, :, lane_blk(j)] for gi in range(gps)], axis=1)
        z = jnp.dot(ycat, perm_ref[...], preferred_element_type=F32)
        for s in range(spv):
            y_ref[pl.ds(j * spv + s, rows, stride=lc), :] = z[:, lane_blk(s)]


def _s5_tables(lam_re, lam_im, log_dt, b_re, b_im, c_re, c_im):
    dt = jnp.exp(log_dt.astype(F32))[:, None]
    lr = lam_re.astype(F32)
    li = lam_im.astype(F32)
    mag = jnp.exp(lr * dt)
    ar = mag * jnp.cos(li * dt)
    ai = mag * jnp.sin(li * dt)
    den = lr * lr + li * li
    cr = ((ar - 1.0) * lr + ai * li) / den
    ci = (ai * lr - (ar - 1.0) * li) / den
    bbr = (cr[..., None] * b_re - ci[..., None] * b_im).transpose(0, 2, 1)
    bbi = (cr[..., None] * b_im + ci[..., None] * b_re).transpose(0, 2, 1)
    cre = c_re.astype(F32)
    cim = c_im.astype(F32)
    cat = lambda a, b: jnp.concatenate([a, b], axis=-1)
    alpha = (lr * dt)[:, None, :]
    beta = (li * dt)[:, None, :]
    pad = jnp.zeros((lr.shape[0], 6, 2 * lr.shape[1]), F32)
    return jnp.concatenate([
        cat(bbr, -bbi), cat(cre, cim), cat(-cim, cre),
        cat(bbr, bbi), cat(-bbi, bbr),
        cat(cre, -cim), cat(-cim, -cre), jnp.zeros_like(cat(cre, cre)),
        cat(alpha, alpha), cat(beta, beta), pad], axis=1)


def s5_mix(h, tables, bsz, seq):
    t, w = h.shape
    lc, gc = S5_CHUNK, S5_GROUP
    gps = LANES // gc
    slabs = w // LANES
    n = seq // lc
    rows = bsz * n
    k = lc * gc
    idx = jnp.arange(k, dtype=jnp.int32)
    tgt = (idx % LANES) // gc * LANES + idx // LANES * gc + idx % gc
    perm = (tgt[:, None] == idx[None, :]).astype(BF16)
    tables = tables.reshape((slabs, gps) + tables.shape[1:])
    return pl.pallas_call(
        functools.partial(_s5_kernel, n),
        grid=(slabs,),
        in_specs=[pl.BlockSpec((t, LANES), lambda i: (0, i)),
                  pl.BlockSpec((None, gps) + tables.shape[2:], lambda i: (i, 0, 0, 0)),
                  pl.BlockSpec((k, k), lambda i: (0, 0))],
        out_specs=pl.BlockSpec((t, LANES), lambda i: (0, i)),
        out_shape=jax.ShapeDtypeStruct((t, w), F32),
        scratch_shapes=[pltpu.VMEM((gps, rows, k), BF16),
                        pltpu.VMEM((k, k), BF16)],
        compiler_params=_cparams(("parallel",)),
        name="s5_mix",
    )(h, tables, perm)


def _glu_kernel(y_ref, h_ref, d_ref, x_ref, wa_ref, wb_ref, o_ref, act_ref):
    @pl.when(pl.program_id(1) == 0)
    def _():
        y = y_ref[...].astype(F32) + d_ref[...] * h_ref[...].astype(F32)
        cdf = 0.5 * (1.0 + jnp.tanh(math.sqrt(2.0 / math.pi) * (y + 0.044715 * (y * y * y))))
        act_ref[...] = (y * cdf).astype(BF16)

    act = act_ref[...]
    za = jnp.dot(act, wa_ref[...], preferred_element_type=F32)
    zb = jnp.dot(act, wb_ref[...], preferred_element_type=F32)
    o_ref[...] = x_ref[...] + za * jax.nn.sigmoid(zb)


def glu_residual(y, h, d_skip, x, w_glu, tm=512, tn=512):
    t, d = x.shape
    w = y.shape[1]
    tm = min(tm, t)
    nb = d // tn
    return pl.pallas_call(
        _glu_kernel,
        grid=(t // tm, nb),
        in_specs=[pl.BlockSpec((tm, w), lambda i, j: (i, 0)),
                  pl.BlockSpec((tm, w), lambda i, j: (i, 0)),
                  pl.BlockSpec((1, w), lambda i, j: (0, 0)),
                  pl.BlockSpec((tm, tn), lambda i, j: (i, j)),
                  pl.BlockSpec((w, tn), lambda i, j: (0, j)),
                  pl.BlockSpec((w, tn), lambda i, j: (0, j + nb))],
        out_specs=pl.BlockSpec((tm, tn), lambda i, j: (i, j)),
        out_shape=jax.ShapeDtypeStruct((t, d), F32),
        scratch_shapes=[pltpu.VMEM((tm, w), BF16)],
        compiler_params=_cparams(("parallel", "arbitrary")),
        name="glu_residual",
    )(y, h, d_skip.reshape(1, w), x, w_glu, w_glu)


def _router_kernel(x_ref, g_ref, w_ref, b_ref, e_ref, p_ref, r_ref, c_ref, cnt_ref):
    tm = x_ref.shape[0]

    @pl.when(pl.program_id(0) == 0)
    def _():
        cnt_ref[...] = jnp.zeros_like(cnt_ref)

    h = _rms(x_ref[...], g_ref[...])
    nt_dot = lambda a, b: lax.dot_general(a, b, (((1,), (1,)), ((), ())),
                                          preferred_element_type=F32)
    w_hi, w_lo = _split2(w_ref[...])
    h_hi, h_lo = _split2(h)
    logits = (nt_dot(w_hi, h_hi) + nt_dot(w_hi, h_lo) + nt_dot(w_lo, h_hi)
              + b_ref[...])
    lc = [logits[i:i + 1, :] for i in range(MOE_GROUPS)]
    m = functools.reduce(jnp.maximum, lc)
    grp = jnp.full(m.shape, MOE_GROUPS - 1, jnp.int32)
    for i in range(MOE_GROUPS - 2, -1, -1):
        grp = jnp.where(lc[i] == m, i, grp)
    den = functools.reduce(lambda a, b: a + b, [jnp.exp(l - m) for l in lc])
    p_top = 1.0 / den
    fine = [logits[MOE_GROUPS + j:MOE_GROUPS + j + 1, :] for j in range(MOE_EXPERTS)]
    sel = []
    for e in range(MOE_EPG):
        v = fine[(MOE_GROUPS - 1) * MOE_EPG + e]
        for i in range(MOE_GROUPS - 2, -1, -1):
            v = jnp.where(grp == i, fine[i * MOE_EPG + e], v)
        sel.append(v)

    def top1(vals):
        best = functools.reduce(jnp.maximum, vals)
        idx = jnp.full(best.shape, MOE_EPG - 1, jnp.int32)
        for e in range(MOE_EPG - 2, -1, -1):
            idx = jnp.where(vals[e] == best, e, idx)
        return best, idx

    v1, i1 = top1(sel)
    v2, i2 = top1([jnp.where(i1 == e, -jnp.inf, sel[e]) for e in range(MOE_EPG)])
    ex = jnp.exp(v2 - v1)
    s = 1.0 + ex
    e0 = grp * MOE_EPG + i1
    e1 = grp * MOE_EPG + i2
    e_ref[0:1, :] = e0
    e_ref[1:2, :] = e1
    p_ref[0:1, :] = p_top * (1.0 / s)
    p_ref[1:2, :] = p_top * (ex / s)

    eid = lax.broadcasted_iota(jnp.int32, (MOE_EXPERTS, tm), 0)
    tri = (lax.broadcasted_iota(jnp.int32, (tm, tm), 0)
           <= lax.broadcasted_iota(jnp.int32, (tm, tm), 1)).astype(BF16)
    base = cnt_ref[...]
    for k, ek in enumerate((e0, e1)):
        hot = eid == ek
        hot_f = hot.astype(F32)
        csum = jnp.dot(hot.astype(BF16), tri, preferred_element_type=F32)
        rank = jnp.sum(hot_f * (base + csum - 1.0), axis=0, keepdims=True)
        r_ref[k:k + 1, :] = rank.astype(jnp.int32)
        base = base + jnp.sum(hot_f, axis=1, keepdims=True)
    cnt_ref[...] = base
    c_ref[...] = base.astype(jnp.int32)


def moe_router(x, g, w_coarse, b_coarse, w_fine, b_fine, tm=512):
    t, d = x.shape
    pad = ROUTER_ROWS - MOE_GROUPS - MOE_EXPERTS
    w = jnp.concatenate([w_coarse.T, w_fine.T, jnp.zeros((pad, d), F32)], axis=0)
    b = jnp.concatenate([b_coarse, b_fine, jnp.zeros((pad,), F32)]).reshape(ROUTER_ROWS, 1)
    tok_spec = pl.BlockSpec((2, tm), lambda i: (0, i))
    return pl.pallas_call(
        _router_kernel,
        grid=(t // tm,),
        in_specs=[pl.BlockSpec((tm, d), lambda i: (i, 0)),
                  pl.BlockSpec((1, d), lambda i: (0, 0)),
                  pl.BlockSpec((ROUTER_ROWS, d), lambda i: (0, 0)),
                  pl.BlockSpec((ROUTER_ROWS, 1), lambda i: (0, 0))],
        out_specs=[tok_spec, tok_spec, tok_spec,
                   pl.BlockSpec((MOE_EXPERTS, 1), lambda i: (0, 0))],
        out_shape=[jax.ShapeDtypeStruct((2, t), jnp.int32),
                   jax.ShapeDtypeStruct((2, t), F32),
                   jax.ShapeDtypeStruct((2, t), jnp.int32),
                   jax.ShapeDtypeStruct((MOE_EXPERTS, 1), jnp.int32)],
        scratch_shapes=[pltpu.VMEM((MOE_EXPERTS, 1), F32)],
        compiler_params=_cparams(("arbitrary",)),
        name="moe_router",
    )(x, g.reshape(1, d), w, b)


def _dispatch_plan(experts, rank, counts):
    n_assign = experts.size
    counts = counts.reshape(-1)
    padded = (counts + MOE_ROWS - 1) // MOE_ROWS * MOE_ROWS
    pend = jnp.cumsum(padded)
    pstart = pend - padded
    eids = jnp.arange(MOE_EXPERTS, dtype=jnp.int32)
    dest = rank + jnp.sum(jnp.where(experts[..., None] == eids, pstart, 0), axis=-1)
    n_groups = -(-n_assign // MOE_ROWS) + MOE_EXPERTS
    g0 = jnp.arange(n_groups, dtype=jnp.int32) * MOE_ROWS
    grp_expert = jnp.minimum(jnp.sum(pend[None, :] <= g0[:, None], axis=1), MOE_EXPERTS - 1)
    n_active = (pend[-1] // MOE_ROWS).reshape(1)
    grp_first = g0 == jnp.sum(jnp.where(grp_expert[:, None] == eids, pstart, 0), axis=1)
    used = counts > 0
    e_slot = (jnp.cumsum(used) - 1) % 2
    later_used = jnp.logical_and(used[None, :], eids[None, :] > eids[:, None])
    e_next = jnp.min(jnp.where(later_used, eids[None, :], MOE_EXPERTS), axis=1)
    e_next = jnp.where(e_next == MOE_EXPERTS, -1, e_next)
    i32 = lambda a: a.astype(jnp.int32)
    runs = (i32(grp_expert), i32(n_active), i32(grp_first), i32(e_slot), i32(e_next))
    return i32(dest).reshape(-1), runs, i32(pstart + counts), i32(padded - counts)


def _dispatch_kernel(dest_ref, ps_ref, pn_ref, na_ref, x_ref, g_ref, xs_hbm, hbuf, zbuf, sem,
                     zsem):
    i = pl.program_id(0)
    n = pl.num_programs(0)
    tm = x_ref.shape[0]
    t = n * tm
    slot = i % 2

    def wait_slot(s):
        for _ in range(2):
            pltpu.make_async_copy(hbuf.at[s], xs_hbm.at[pl.ds(0, tm)], sem.at[s]).wait()

    def zero_row_copy(row):
        return pltpu.make_async_copy(zbuf.at[pl.ds(0, 1)], xs_hbm.at[pl.ds(row, 1)], zsem.at[0])

    def for_each_pad_row(fn):
        def per_expert(e, c):
            def per_row(r, c2):
                fn(ps_ref[e] + r)
                return c2
            return lax.fori_loop(0, pn_ref[e], per_row, c)
        lax.fori_loop(0, ps_ref.shape[0], per_expert, 0)

    def tail_group_copy(grp):
        rows = pl.ds(pl.multiple_of(grp * tm, tm), tm)
        return pltpu.make_async_copy(hbuf.at[1], xs_hbm.at[rows], sem.at[1])

    def for_each_tail_group(fn):
        def body(grp, c):
            fn(grp)
            return c
        lax.fori_loop(na_ref[0], xs_hbm.shape[0] // tm, body, 0)

    @pl.when(i == 0)
    def _():
        zbuf[...] = jnp.zeros_like(zbuf)
        for_each_pad_row(lambda row: zero_row_copy(row).start())
        hbuf[1] = jnp.zeros(hbuf.shape[1:], hbuf.dtype)
        for_each_tail_group(lambda grp: tail_group_copy(grp).start())
        for_each_tail_group(lambda grp: tail_group_copy(grp).wait())

    @pl.when(i >= 2)
    def _():
        wait_slot(slot)

    hbuf[slot] = _rms(x_ref[...], g_ref[...])

    def body(r, c):
        for k in range(2):
            row = dest_ref[k * t + i * tm + r]
            pltpu.make_async_copy(hbuf.at[slot, pl.ds(r, 1)], xs_hbm.at[pl.ds(row, 1)],
                                  sem.at[slot]).start()
        return c
    lax.fori_loop(0, tm, body, 0, unroll=8)

    @pl.when(i == n - 1)
    def _():
        wait_slot(slot)

        @pl.when(n >= 2)
        def _():
            wait_slot(1 - slot)

        for_each_pad_row(lambda row: zero_row_copy(row).wait())


def moe_dispatch(x, g, dest, pad_start, pad_len, n_active, total):
    t, d = x.shape
    tm = MOE_ROWS
    grid_spec = pltpu.PrefetchScalarGridSpec(
        num_scalar_prefetch=4,
        grid=(t // tm,),
        in_specs=[pl.BlockSpec((tm, d), lambda i, *_: (i, 0)),
                  pl.BlockSpec((1, d), lambda i, *_: (0, 0))],
        out_specs=pl.BlockSpec(memory_space=pl.ANY),
        scratch_shapes=[pltpu.VMEM((2, tm, d), F32),
                        pltpu.VMEM((8, d), F32),
                        pltpu.SemaphoreType.DMA((2,)),
                        pltpu.SemaphoreType.DMA((1,))],
    )
    return pl.pallas_call(
        _dispatch_kernel,
        grid_spec=grid_spec,
        out_shape=jax.ShapeDtypeStruct((total, d), F32),
        compiler_params=_cparams(("arbitrary",)),
        name="moe_dispatch",
    )(dest, pad_start, pad_len, n_active, x, g.reshape(1, d))


def _expert_weights_step(layer, n_mats, ge_ref, na_ref, first_ref, slot_ref, next_ref,
                         w_hbm, wbuf, wb16, sem):
    g = pl.program_id(0)
    e = ge_ref[g]

    def copies(expert, slot):
        return [pltpu.make_async_copy(w_hbm[m].at[layer, expert], wbuf.at[slot, m],
                                      sem.at[slot, m]) for m in range(n_mats)]

    @pl.when(jnp.logical_and(g < na_ref[0], first_ref[g] == 1))
    def _():
        slot = slot_ref[e]

        @pl.when(g == 0)
        def _():
            for c in copies(e, slot):
                c.start()

        for c in copies(e, slot):
            c.wait()
        nxt = next_ref[e]

        @pl.when(nxt >= 0)
        def _():
            for c in copies(nxt, 1 - slot):
                c.start()

        rows = wbuf.shape[2]
        step = 256
        for m in range(n_mats):
            for r in range(0, rows, step):
                wb16[m, r:r + step, :] = wbuf[slot, m, r:r + step, :].astype(BF16)


def _experts_up_kernel(layer, ge_ref, na_ref, first_ref, slot_ref, next_ref,
                       xs_ref, wg_hbm, wu_hbm, h_ref, wbuf, wb16, sem):
    _expert_weights_step(layer, 2, ge_ref, na_ref, first_ref, slot_ref, next_ref,
                         (wg_hbm, wu_hbm), wbuf, wb16, sem)
    g = pl.program_id(0)

    @pl.when(g < na_ref[0])
    def _():
        xb = xs_ref[...].astype(BF16)
        a = jnp.dot(xb, wb16[0], preferred_element_type=F32)
        b = jnp.dot(xb, wb16[1], preferred_element_type=F32)
        h_ref[...] = (a * jax.nn.sigmoid(a) * b).astype(h_ref.dtype)

    @pl.when(g >= na_ref[0])
    def _():
        h_ref[...] = jnp.zeros_like(h_ref)


def _experts_down_kernel(layer, ge_ref, na_ref, first_ref, slot_ref, next_ref,
                         h_ref, wd_hbm, o_ref, wbuf, wb16, sem):
    _expert_weights_step(layer, 1, ge_ref, na_ref, first_ref, slot_ref, next_ref,
                         (wd_hbm,), wbuf, wb16, sem)
    g = pl.program_id(0)

    @pl.when(g < na_ref[0])
    def _():
        o_ref[...] = jnp.dot(h_ref[...], wb16[0], preferred_element_type=F32)

    @pl.when(g >= na_ref[0])
    def _():
        o_ref[...] = jnp.zeros_like(o_ref)


def moe_experts(xs, runs, w_gate, w_up, w_down, layer):
    total, d = xs.shape
    ff = w_gate.shape[3]
    n_groups = total // MOE_ROWS
    n_pref = len(runs)

    def blk(g, ge, na, *_):
        return (jnp.maximum(jnp.minimum(g, na[0] - 1), 0), 0)

    def call(body, n_w, in_cols, w_rows, w_cols, out_dtype, name):
        return pl.pallas_call(
            functools.partial(body, layer),
            grid_spec=pltpu.PrefetchScalarGridSpec(
                num_scalar_prefetch=n_pref, grid=(n_groups,),
                in_specs=[pl.BlockSpec((MOE_ROWS, in_cols), blk)]
                + [pl.BlockSpec(memory_space=pl.ANY)] * n_w,
                out_specs=pl.BlockSpec((MOE_ROWS, w_cols), lambda g, *_: (g, 0)),
                scratch_shapes=[pltpu.VMEM((2, n_w, w_rows, w_cols), F32),
                                pltpu.VMEM((n_w, w_rows, w_cols), BF16),
                                pltpu.SemaphoreType.DMA((2, n_w))]),
            out_shape=jax.ShapeDtypeStruct((total, w_cols), out_dtype),
            compiler_params=_cparams(("arbitrary",)),
            name=name)

    hid = call(_experts_up_kernel, 2, d, d, ff, BF16, "moe_experts_up")(*runs, xs, w_gate, w_up)
    return call(_experts_down_kernel, 1, ff, ff, d, F32, "moe_experts_down")(*runs, hid, w_down)


def _combine_kernel(final_norm, pos_ref, yb_hbm, x_ref, p_ref, g_ref, o_ref, ybuf, sem):
    i = pl.program_id(0)
    n = pl.num_programs(0)
    tm = x_ref.shape[0]
    t = n * tm

    def issue(tile, slot):
        def body(r, c):
            for k in range(2):
                row = pos_ref[k * t + tile * tm + r]
                pltpu.make_async_copy(yb_hbm.at[pl.ds(row, 1)], ybuf.at[slot, k, pl.ds(r, 1)],
                                      sem.at[slot]).start()
            return c
        lax.fori_loop(0, tm, body, 0, unroll=8)

    @pl.when(i == 0)
    def _():
        issue(0, 0)

    @pl.when(i + 1 < n)
    def _():
        issue(i + 1, (i + 1) % 2)

    slot = i % 2
    for k in range(2):
        pltpu.make_async_copy(yb_hbm.at[pl.ds(0, tm)], ybuf.at[slot, k], sem.at[slot]).wait()
    p = p_ref[...]
    out = x_ref[...] + (p[:, 0:1] * ybuf[slot, 0] + p[:, 1:2] * ybuf[slot, 1])
    if final_norm:
        out = _rms(out, g_ref[...])
    o_ref[...] = out


def moe_combine(yb, dest, gates_t, x, g_final=None, tm=256):
    t, d = x.shape
    final_norm = g_final is not None
    g = (g_final if final_norm else jnp.ones((d,), F32)).reshape(1, d)
    grid_spec = pltpu.PrefetchScalarGridSpec(
        num_scalar_prefetch=1,
        grid=(t // tm,),
        in_specs=[pl.BlockSpec(memory_space=pl.ANY),
                  pl.BlockSpec((tm, d), lambda i, pos: (i, 0)),
                  pl.BlockSpec((tm, 2), lambda i, pos: (i, 0)),
                  pl.BlockSpec((1, d), lambda i, pos: (0, 0))],
        out_specs=pl.BlockSpec((tm, d), lambda i, pos: (i, 0)),
        scratch_shapes=[pltpu.VMEM((2, 2, tm, d), F32),
                        pltpu.SemaphoreType.DMA((2,))],
    )
    return pl.pallas_call(
        functools.partial(_combine_kernel, final_norm),
        grid_spec=grid_spec,
        out_shape=jax.ShapeDtypeStruct((t, d), F32),
        compiler_params=_cparams(("arbitrary",)),
        name="moe_combine",
    )(dest, yb, x, gates_t, g)


def hier_moe_residual(x, g_ffn, w_coarse, b_coarse, w_fine, b_fine, w_gate, w_up, w_down,
                      layer, g_final=None):
    experts, gates, rank, counts = moe_router(x, g_ffn, w_coarse, b_coarse, w_fine, b_fine)
    dest, runs, pad_start, pad_len = _dispatch_plan(experts, rank, counts)
    total = runs[0].shape[0] * MOE_ROWS
    xs = moe_dispatch(x, g_ffn, dest, pad_start, pad_len, runs[1], total)
    yb = moe_experts(xs, runs, w_gate, w_up, w_down, layer)
    return moe_combine(yb, dest, gates.T, x, g_final)


def kernel(x, g_mix, g_ffn, g_final, w_in_even, conv_w, hg_lower_bound, hg_norm_g, w_out_even, s5_lambda_re, s5_lambda_im, s5_log_dt, s5_b_re, s5_b_im, s5_c_re, s5_c_im, s5_d, w_glu, moe_w_coarse, moe_b_coarse, moe_w_fine, moe_b_fine, moe_w_gate, moe_w_up, moe_w_down):
    bsz, seq, d = x.shape
    depth = g_mix.shape[0]
    conv_width = conv_w.shape[2]
    hg_width = hg_lower_bound.shape[1]
    lb_all = jnp.cumsum(jax.nn.softmax(hg_lower_bound.astype(F32), axis=0), axis=0)
    xf = x.reshape(bsz * seq, d)
    for layer in range(depth):
        j = layer // 2
        if layer % 2 == 0:
            z = norm_matmul(xf, g_mix[layer], w_in_even[j].astype(BF16))
            o_b = hgrn2(z, lb_all[j], hg_norm_g[j], bsz, seq, hg_width, 3 * conv_width)
            xf = outproj_even(z, conv_w[j], o_b, xf, w_out_even[j].astype(BF16), seq)
        else:
            h = rmsnorm_cast(xf, g_mix[layer], F32)
            tables = _s5_tables(s5_lambda_re[j], s5_lambda_im[j], s5_log_dt[j],
                                s5_b_re[j], s5_b_im[j], s5_c_re[j], s5_c_im[j])
            y = s5_mix(h, tables, bsz, seq)
            xf = glu_residual(y, h, s5_d[j], xf, w_glu[j].astype(BF16))
        xf = hier_moe_residual(
            xf, g_ffn[layer], moe_w_coarse[layer], moe_b_coarse[layer], moe_w_fine[layer],
            moe_b_fine[layer], moe_w_gate, moe_w_up, moe_w_down, layer,
            g_final if layer == depth - 1 else None)
    return xf.reshape(bsz, seq, d)
```

```python
import functools
import math

import jax
import jax.numpy as jnp
from jax import lax
from jax.experimental import pallas as pl
from jax.experimental.pallas import tpu as pltpu

F32 = jnp.float32
BF16 = jnp.bfloat16
EPS = 1e-6

LANES = 128
HG_HEAD_DIM = 128
HG_CHUNK = 64
CONV_K = 3
S5_GROUP = 16
S5_STATE = 64
S5_CHUNK = 64
MOE_GROUPS = 4
MOE_EPG = 8
MOE_EXPERTS = MOE_GROUPS * MOE_EPG
MOE_ROWS = 256
ROUTER_ROWS = 40
ROW_UNROLL = 8
VMEM_LIMIT = 56 * 1024 * 1024


def _cparams(sem, vmem=VMEM_LIMIT):
    return pltpu.CompilerParams(dimension_semantics=sem, vmem_limit_bytes=vmem)


def _rms(x, g):
    ms = jnp.mean(x * x, axis=-1, keepdims=True)
    return x * lax.rsqrt(ms + EPS) * g


def _norm_kernel(x_ref, g_ref, o_ref):
    o_ref[...] = _rms(x_ref[...], g_ref[...]).astype(o_ref.dtype)


def rmsnorm_cast(x, g, dtype, tm=512):
    t, d = x.shape
    return pl.pallas_call(
        _norm_kernel,
        grid=(t // tm,),
        in_specs=[pl.BlockSpec((tm, d), lambda i: (i, 0)),
                  pl.BlockSpec((1, d), lambda i: (0, 0))],
        out_specs=pl.BlockSpec((tm, d), lambda i: (i, 0)),
        out_shape=jax.ShapeDtypeStruct((t, d), dtype),
        compiler_params=_cparams(("parallel",)),
        name="rmsnorm_cast",
    )(x, g.reshape(1, d))


def _norm_matmul_kernel(x_ref, g_ref, w_ref, o_ref, h_ref):
    @pl.when(pl.program_id(1) == 0)
    def _():
        h_ref[...] = _rms(x_ref[...], g_ref[...]).astype(BF16)

    o_ref[...] = jnp.dot(h_ref[...], w_ref[...],
                         preferred_element_type=F32).astype(o_ref.dtype)


def norm_matmul(x, g, w, tm=1024, tn=1024):
    t, d = x.shape
    n = w.shape[1]
    tm = min(tm, t)
    return pl.pallas_call(
        _norm_matmul_kernel,
        grid=(t // tm, n // tn),
        in_specs=[pl.BlockSpec((tm, d), lambda i, j: (i, 0)),
                  pl.BlockSpec((1, d), lambda i, j: (0, 0)),
                  pl.BlockSpec((d, tn), lambda i, j: (0, j))],
        out_specs=pl.BlockSpec((tm, tn), lambda i, j: (i, j)),
        out_shape=jax.ShapeDtypeStruct((t, n), BF16),
        scratch_shapes=[pltpu.VMEM((tm, d), BF16)],
        compiler_params=_cparams(("parallel", "arbitrary")),
        name="norm_matmul",
    )(x, g.reshape(1, d), w)


def _split2(x):
    hi = x.astype(BF16)
    return hi, (x - hi.astype(F32)).astype(BF16)


def _split3(x):
    hi = x.astype(BF16)
    r1 = x - hi.astype(F32)
    mid = r1.astype(BF16)
    lo = (r1 - mid.astype(F32)).astype(BF16)
    return hi, mid, lo


def _hgrn2_kernel(q_ref, f_ref, v_ref, g_ref, lb_ref, ng_ref, o_ref, st_ref):
    c = HG_CHUNK
    hd = HG_HEAD_DIM
    heads = st_ref.shape[0]
    n_chunks = q_ref.shape[0] // c
    ng = ng_ref[...]
    row = lax.broadcasted_iota(jnp.int32, (c, c), 0)
    col = lax.broadcasted_iota(jnp.int32, (c, c), 1)
    causal = row >= col
    tril = causal.astype(BF16)
    mid = c // 2

    @pl.when(pl.program_id(1) == 0)
    def _():
        st_ref[...] = jnp.zeros_like(st_ref)

    nt_dims = (((1,), (1,)), ((), ()))
    hs = range(heads)
    cols = [slice(h * hd, (h + 1) * hd) for h in hs]

    def step(n, carry):
        rows = pl.ds(pl.multiple_of(n * c, c), c)
        q = [q_ref[rows, cl].astype(F32) for cl in cols]
        f = [lb_ref[:, cl] + (1.0 - lb_ref[:, cl]) * jax.nn.sigmoid(f_ref[rows, cl].astype(F32))
             for cl in cols]
        k = [1.0 - fh for fh in f]
        parts = [_split3(jnp.log(fh)) for fh in f]
        b = [sum(jnp.dot(tril, p, preferred_element_type=F32) for p in ph) for ph in parts]
        ref = [bh[mid - 1:mid, :] for bh in b]
        b_last = [bh[c - 1:c, :] for bh in b]
        st = [st_ref[h] for h in hs]
        qs = [(q[h] * jnp.exp(b[h] - ref[h])).astype(BF16) for h in hs]
        ks = [(k[h] * jnp.exp(ref[h] - b[h])).astype(BF16) for h in hs]
        scores = [lax.dot_general(qs[h], ks[h], nt_dims, preferred_element_type=F32) for h in hs]
        qe = [(q[h] * jnp.exp(b[h])).astype(BF16) for h in hs]
        o_inter = [lax.dot_general(qe[h], st[h].astype(BF16), nt_dims, preferred_element_type=F32)
                   for h in hs]
        kd = [(k[h] * jnp.exp(b_last[h] - b[h])).astype(BF16) for h in hs]
        upd = [lax.dot_general(v_ref[rows, cols[h]], kd[h], (((0,), (0,)), ((), ())),
                               preferred_element_type=F32) for h in hs]
        for h in hs:
            st_ref[h] = st[h] * jnp.exp(b_last[h]) + upd[h]
        sc = [jnp.where(causal, s, 0.0).astype(BF16) for s in scores]
        o = [o_inter[h] + jnp.dot(sc[h], v_ref[rows, cols[h]], preferred_element_type=F32)
             for h in hs]
        for h in hs:
            oh = o[h] * lax.rsqrt(jnp.mean(o[h] * o[h], axis=-1, keepdims=True) + EPS) * ng
            gate = g_ref[rows, cols[h]].astype(F32)
            o_ref[rows, cols[h]] = (oh * (gate * jax.nn.sigmoid(gate))).astype(o_ref.dtype)
        return carry

    lax.fori_loop(0, n_chunks, step, 0)


def hgrn2(z, lb, norm_g, bsz, seq, width, col0, ts=1024):
    hd = HG_HEAD_DIM
    heads = width // hd
    cb = col0 // width
    ts = min(ts, seq)
    nt = seq // ts

    def zspec(k):
        return pl.BlockSpec((ts, width), lambda b, s: (b * nt + s, cb + k))

    return pl.pallas_call(
        _hgrn2_kernel,
        grid=(bsz, nt),
        in_specs=[zspec(0), zspec(1), zspec(2), zspec(3),
                  pl.BlockSpec((1, width), lambda b, s: (0, 0)),
                  pl.BlockSpec((1, hd), lambda b, s: (0, 0))],
        out_specs=pl.BlockSpec((ts, width), lambda b, s: (b * nt + s, 0)),
        out_shape=jax.ShapeDtypeStruct((bsz * seq, width), BF16),
        scratch_shapes=[pltpu.VMEM((heads, hd, hd), F32)],
        compiler_params=_cparams(("parallel", "arbitrary")),
        name="hgrn2",
    )(z, z, z, z, lb.reshape(1, width), norm_g.reshape(1, hd))


def _outproj_kernel(seq_tiles, ab_ref, ac_ref, ah_ref, hc_ref, hh_ref, cw_ref,
                    ob_ref, x_ref, wa_ref, wb_ref, o_ref):
    i = pl.program_id(0)
    u = ac_ref[...].astype(F32) * ah_ref[...].astype(F32)
    halo = hc_ref[...].astype(F32) * hh_ref[...].astype(F32)
    halo = jnp.where(i % seq_tiles == 0, 0.0, halo)
    hr = halo.shape[0]
    row = lax.broadcasted_iota(jnp.int32, u.shape, 0)
    u1 = jnp.where(row == 0, halo[hr - 1:hr, :], pltpu.roll(u, 1, axis=0))
    u2 = jnp.where(row == 0, halo[hr - 2:hr - 1, :],
                   jnp.where(row == 1, halo[hr - 1:hr, :], pltpu.roll(u, 2, axis=0)))
    cw = cw_ref[...]
    conv = cw[2:3, :] * u + cw[1:2, :] * u1 + cw[0:1, :] * u2
    ya = (ab_ref[...].astype(F32) * conv).astype(BF16)
    mix = jnp.dot(ya, wa_ref[...], preferred_element_type=F32)
    mix = mix + jnp.dot(ob_ref[...], wb_ref[...], preferred_element_type=F32)
    o_ref[...] = x_ref[...] + mix


def outproj_even(z, conv_w, o_b, x, w_out, seq, tm=256, halo=16):
    t, d = x.shape
    cw = conv_w.shape[1]
    hw = o_b.shape[1]
    hb = tm // halo
    kern = functools.partial(_outproj_kernel, seq // tm)
    return pl.pallas_call(
        kern,
        grid=(t // tm,),
        in_specs=[pl.BlockSpec((tm, cw), lambda i: (i, 0)),
                  pl.BlockSpec((tm, cw), lambda i: (i, 1)),
                  pl.BlockSpec((tm, cw), lambda i: (i, 2)),
                  pl.BlockSpec((halo, cw), lambda i: (jnp.maximum(i * hb - 1, 0), 1)),
                  pl.BlockSpec((halo, cw), lambda i: (jnp.maximum(i * hb - 1, 0), 2)),
                  pl.BlockSpec((CONV_K, cw), lambda i: (0, 0)),
                  pl.BlockSpec((tm, hw), lambda i: (i, 0)),
                  pl.BlockSpec((tm, d), lambda i: (i, 0)),
                  pl.BlockSpec((cw, d), lambda i: (0, 0)),
                  pl.BlockSpec((hw, d), lambda i: (1, 0))],
        out_specs=pl.BlockSpec((tm, d), lambda i: (i, 0)),
        out_shape=jax.ShapeDtypeStruct((t, d), F32),
        compiler_params=_cparams(("parallel",)),
        name="outproj_even",
    )(z, z, z, z, z, conv_w, o_b, x, w_out, w_out)


def _rep_rows(x, n):
    r, l = x.shape
    return jnp.broadcast_to(x[:, None, :], (r, n, l)).reshape(r * n, l)


def _tile_rows(x, n):
    r, l = x.shape
    return jnp.broadcast_to(x[None, :, :], (n, r, l)).reshape(n * r, l)


def _s5_group(n_chunks, prm, u, m_ref):
    lc = S5_CHUNK
    gc = S5_GROUP
    half = S5_STATE
    nt_dims = (((1,), (1,)), ((), ()))
    bqa, ce1, ce2, bwa, bwb, cga, cgb = (prm[i * gc:(i + 1) * gc, :] for i in range(7))
    alpha = prm[8 * gc:8 * gc + 1, :]
    beta = prm[8 * gc + 1:8 * gc + 2, :]

    def cpow(tt):
        mag = jnp.exp(alpha * tt)
        ang = beta * tt
        return mag * jnp.cos(ang), mag * jnp.sin(ang)

    t_col = lax.broadcasted_iota(jnp.int32, (lc, LANES), 0).astype(F32)
    p0r, p0i = cpow(t_col)
    prr, pri = cpow(float(lc - 1) - t_col)
    a1r, a1i = cpow(jnp.full((1, LANES), 1.0, F32))
    p1r, p1i = p0r * a1r - p0i * a1i, p0r * a1i + p0i * a1r

    ecat = _rep_rows(p0r, gc) * _tile_rows(ce1, lc) + _rep_rows(p0i, gc) * _tile_rows(ce2, lc)
    k_row = lax.dot_general(bqa, ecat, nt_dims, precision=lax.Precision.HIGHEST,
                            preferred_element_type=F32)
    pos = lax.broadcasted_iota(jnp.int32, k_row.shape, 1)
    for s in range(lc):
        shifted = k_row if s == 0 else pltpu.roll(k_row, s * gc, axis=1)
        m_ref[s * gc:(s + 1) * gc, :] = jnp.where(pos >= s * gc, shifted, 0.0).astype(BF16)

    wcat = (_rep_rows(prr, gc) * _tile_rows(bwa, lc)
            + _rep_rows(pri, gc) * _tile_rows(bwb, lc)).astype(BF16)
    st = jnp.dot(u, wcat, preferred_element_type=F32)

    rows = st.shape[0]
    rpos = lax.broadcasted_iota(jnp.int32, (rows, LANES), 0) % n_chunks
    rlane = lax.broadcasted_iota(jnp.int32, (rows, LANES), 1)
    ar, ai = cpow(jnp.full((1, LANES), float(lc), F32))
    d = 1
    while d < n_chunks:
        sh = jnp.where(rpos >= d, pltpu.roll(st, d, axis=0), 0.0)
        a2 = jnp.where(rlane < half, -ai, ai)
        st = st + ar * sh + a2 * pltpu.roll(sh, half, axis=1)
        ar, ai = ar * ar - ai * ai, 2.0 * ar * ai
        d *= 2
    h0 = jnp.where(rpos >= 1, pltpu.roll(st, 1, axis=0), 0.0).astype(BF16)

    gcat = (_rep_rows(p1r, gc) * _tile_rows(cga, lc)
            + _rep_rows(p1i, gc) * _tile_rows(cgb, lc)).astype(BF16)
    y = jnp.dot(u, m_ref[...], preferred_element_type=F32)
    return y + lax.dot_general(h0, gcat, nt_dims, preferred_element_type=F32)


def _s5_kernel(n_chunks, h_ref, p_ref, perm_ref, d_ref, y_ref, u_ref, m_ref):
    lc = S5_CHUNK
    gps = u_ref.shape[0]
    rows = u_ref.shape[1]
    spv = LANES // S5_GROUP
    lane_blk = lambda k: slice(k * LANES, (k + 1) * LANES)

    for j in range(lc // spv):
        xcat = jnp.concatenate(
            [h_ref[pl.ds(j * spv + s, rows, stride=lc), :] for s in range(spv)], axis=1)
        uall = jnp.dot(xcat.astype(BF16), perm_ref[...], preferred_element_type=F32).astype(BF16)
        for gi in range(gps):
            u_ref[gi, :, lane_blk(j)] = uall[:, lane_blk(gi)]

    def group(gi, carry):
        y = _s5_group(n_chunks, p_ref[gi], u_ref[gi], m_ref)
        u_ref[gi] = y.astype(BF16)
        return carry
    lax.fori_loop(0, gps, group, 0)

    for j in range(lc // spv):
        ycat = jnp.concatenate([u_ref[gi, :, lane_blk(j)] for gi in range(gps)], axis=1)
        z = jnp.dot(ycat, perm_ref[...], preferred_element_type=F32)
        for s in range(spv):
            tok = pl.ds(j * spv + s, rows, stride=lc)
            y_ref[tok, :] = z[:, lane_blk(s)] + d_ref[...] * h_ref[tok, :]


def _s5_tables(lam_re, lam_im, log_dt, b_re, b_im, c_re, c_im):
    dt = jnp.exp(log_dt.astype(F32))[:, None]
    lr = lam_re.astype(F32)
    li = lam_im.astype(F32)
    mag = jnp.exp(lr * dt)
    ar = mag * jnp.cos(li * dt)
    ai = mag * jnp.sin(li * dt)
    den = lr * lr + li * li
    cr = ((ar - 1.0) * lr + ai * li) / den
    ci = (ai * lr - (ar - 1.0) * li) / den
    bbr = (cr[..., None] * b_re - ci[..., None] * b_im).transpose(0, 2, 1)
    bbi = (cr[..., None] * b_im + ci[..., None] * b_re).transpose(0, 2, 1)
    cre = c_re.astype(F32)
    cim = c_im.astype(F32)
    cat = lambda a, b: jnp.concatenate([a, b], axis=-1)
    alpha = (lr * dt)[:, None, :]
    beta = (li * dt)[:, None, :]
    pad = jnp.zeros((lr.shape[0], 6, 2 * lr.shape[1]), F32)
    return jnp.concatenate([
        cat(bbr, -bbi), cat(cre, cim), cat(-cim, cre),
        cat(bbr, bbi), cat(-bbi, bbr),
        cat(cre, -cim), cat(-cim, -cre), jnp.zeros_like(cat(cre, cre)),
        cat(alpha, alpha), cat(beta, beta), pad], axis=1)


def s5_mix(h, tables, d_skip, bsz, seq):
    t, w = h.shape
    lc, gc = S5_CHUNK, S5_GROUP
    gps = LANES // gc
    slabs = w // LANES
    n = seq // lc
    rows = bsz * n
    k = lc * gc
    idx = jnp.arange(k, dtype=jnp.int32)
    tgt = (idx % LANES) // gc * LANES + idx // LANES * gc + idx % gc
    perm = (tgt[:, None] == idx[None, :]).astype(BF16)
    tables = tables.reshape((slabs, gps) + tables.shape[1:])
    return pl.pallas_call(
        functools.partial(_s5_kernel, n),
        grid=(slabs,),
        in_specs=[pl.BlockSpec((t, LANES), lambda i: (0, i)),
                  pl.BlockSpec((None, gps) + tables.shape[2:], lambda i: (i, 0, 0, 0)),
                  pl.BlockSpec((k, k), lambda i: (0, 0)),
                  pl.BlockSpec((1, LANES), lambda i: (0, i))],
        out_specs=pl.BlockSpec((t, LANES), lambda i: (0, i)),
        out_shape=jax.ShapeDtypeStruct((t, w), F32),
        scratch_shapes=[pltpu.VMEM((gps, rows, k), BF16),
                        pltpu.VMEM((k, k), BF16)],
        compiler_params=_cparams(("parallel",)),
        name="s5_mix",
    )(h, tables, perm, d_skip.reshape(1, w))


def _glu_kernel(y_ref, x_ref, wa_ref, wb_ref, o_ref, act_ref):
    @pl.when(pl.program_id(1) == 0)
    def _():
        y = y_ref[...]
        cdf = 0.5 * (1.0 + jnp.tanh(math.sqrt(2.0 / math.pi) * (y + 0.044715 * (y * y * y))))
        act_ref[...] = (y * cdf).astype(BF16)

    act = act_ref[...]
    za = jnp.dot(act, wa_ref[...], preferred_element_type=F32)
    zb = jnp.dot(act, wb_ref[...], preferred_element_type=F32)
    o_ref[...] = x_ref[...] + za * jax.nn.sigmoid(zb)


def glu_residual(y, x, w_glu, tm=1024, tn=512):
    t, d = x.shape
    w = y.shape[1]
    tm = min(tm, t)
    nb = d // tn
    return pl.pallas_call(
        _glu_kernel,
        grid=(t // tm, nb),
        in_specs=[pl.BlockSpec((tm, w), lambda i, j: (i, 0)),
                  pl.BlockSpec((tm, tn), lambda i, j: (i, j)),
                  pl.BlockSpec((w, tn), lambda i, j: (0, j)),
                  pl.BlockSpec((w, tn), lambda i, j: (0, j + nb))],
        out_specs=pl.BlockSpec((tm, tn), lambda i, j: (i, j)),
        out_shape=jax.ShapeDtypeStruct((t, d), F32),
        scratch_shapes=[pltpu.VMEM((tm, w), BF16)],
        compiler_params=_cparams(("parallel", "arbitrary")),
        name="glu_residual",
    )(y, x, w_glu, w_glu)


def _router_kernel(x_ref, g_ref, w_ref, b_ref, e_ref, p_ref, r_ref, c_ref, cnt_ref):
    tm = x_ref.shape[0]

    @pl.when(pl.program_id(0) == 0)
    def _():
        cnt_ref[...] = jnp.zeros_like(cnt_ref)

    h = _rms(x_ref[...], g_ref[...])
    nt_dot = lambda a, b: lax.dot_general(a, b, (((1,), (1,)), ((), ())),
                                          preferred_element_type=F32)
    w_hi, w_lo = _split2(w_ref[...])
    h_hi, h_lo = _split2(h)
    logits = (nt_dot(w_hi, h_hi) + nt_dot(w_hi, h_lo) + nt_dot(w_lo, h_hi)
              + b_ref[...])
    lc = [logits[i:i + 1, :] for i in range(MOE_GROUPS)]
    m = functools.reduce(jnp.maximum, lc)
    grp = jnp.full(m.shape, MOE_GROUPS - 1, jnp.int32)
    for i in range(MOE_GROUPS - 2, -1, -1):
        grp = jnp.where(lc[i] == m, i, grp)
    den = functools.reduce(lambda a, b: a + b, [jnp.exp(l - m) for l in lc])
    p_top = 1.0 / den
    fine = [logits[MOE_GROUPS + j:MOE_GROUPS + j + 1, :] for j in range(MOE_EXPERTS)]
    sel = []
    for e in range(MOE_EPG):
        v = fine[(MOE_GROUPS - 1) * MOE_EPG + e]
        for i in range(MOE_GROUPS - 2, -1, -1):
            v = jnp.where(grp == i, fine[i * MOE_EPG + e], v)
        sel.append(v)

    def top1(vals):
        best = functools.reduce(jnp.maximum, vals)
        idx = jnp.full(best.shape, MOE_EPG - 1, jnp.int32)
        for e in range(MOE_EPG - 2, -1, -1):
            idx = jnp.where(vals[e] == best, e, idx)
        return best, idx

    v1, i1 = top1(sel)
    v2, i2 = top1([jnp.where(i1 == e, -jnp.inf, sel[e]) for e in range(MOE_EPG)])
    ex = jnp.exp(v2 - v1)
    s = 1.0 + ex
    e0 = grp * MOE_EPG + i1
    e1 = grp * MOE_EPG + i2
    e_ref[0:1, :] = e0
    e_ref[1:2, :] = e1
    p_ref[0:1, :] = p_top * (1.0 / s)
    p_ref[1:2, :] = p_top * (ex / s)

    eid = lax.broadcasted_iota(jnp.int32, (MOE_EXPERTS, tm), 0)
    tri = (lax.broadcasted_iota(jnp.int32, (tm, tm), 0)
           <= lax.broadcasted_iota(jnp.int32, (tm, tm), 1)).astype(BF16)
    base = cnt_ref[...]
    for k, ek in enumerate((e0, e1)):
        hot = eid == ek
        hot_f = hot.astype(F32)
        csum = jnp.dot(hot.astype(BF16), tri, preferred_element_type=F32)
        rank = jnp.sum(hot_f * (base + csum - 1.0), axis=0, keepdims=True)
        r_ref[k:k + 1, :] = rank.astype(jnp.int32)
        base = base + jnp.sum(hot_f, axis=1, keepdims=True)
    cnt_ref[...] = base
    c_ref[...] = base.astype(jnp.int32)


def moe_router(x, g, w_coarse, b_coarse, w_fine, b_fine, tm=512):
    t, d = x.shape
    pad = ROUTER_ROWS - MOE_GROUPS - MOE_EXPERTS
    w = jnp.concatenate([w_coarse.T, w_fine.T, jnp.zeros((pad, d), F32)], axis=0)
    b = jnp.concatenate([b_coarse, b_fine, jnp.zeros((pad,), F32)]).reshape(ROUTER_ROWS, 1)
    tok_spec = pl.BlockSpec((2, tm), lambda i: (0, i))
    return pl.pallas_call(
        _router_kernel,
        grid=(t // tm,),
        in_specs=[pl.BlockSpec((tm, d), lambda i: (i, 0)),
                  pl.BlockSpec((1, d), lambda i: (0, 0)),
                  pl.BlockSpec((ROUTER_ROWS, d), lambda i: (0, 0)),
                  pl.BlockSpec((ROUTER_ROWS, 1), lambda i: (0, 0))],
        out_specs=[tok_spec, tok_spec, tok_spec,
                   pl.BlockSpec((MOE_EXPERTS, 1), lambda i: (0, 0))],
        out_shape=[jax.ShapeDtypeStruct((2, t), jnp.int32),
                   jax.ShapeDtypeStruct((2, t), F32),
                   jax.ShapeDtypeStruct((2, t), jnp.int32),
                   jax.ShapeDtypeStruct((MOE_EXPERTS, 1), jnp.int32)],
        scratch_shapes=[pltpu.VMEM((MOE_EXPERTS, 1), F32)],
        compiler_params=_cparams(("arbitrary",)),
        name="moe_router",
    )(x, g.reshape(1, d), w, b)


def _dispatch_plan(experts, rank, counts):
    n_assign = experts.size
    counts = counts.reshape(-1)
    padded = (counts + MOE_ROWS - 1) // MOE_ROWS * MOE_ROWS
    pend = jnp.cumsum(padded)
    pstart = pend - padded
    eids = jnp.arange(MOE_EXPERTS, dtype=jnp.int32)
    dest = rank + jnp.sum(jnp.where(experts[..., None] == eids, pstart, 0), axis=-1)
    n_groups = -(-n_assign // MOE_ROWS) + MOE_EXPERTS
    g0 = jnp.arange(n_groups, dtype=jnp.int32) * MOE_ROWS
    grp_expert = jnp.minimum(jnp.sum(pend[None, :] <= g0[:, None], axis=1), MOE_EXPERTS - 1)
    n_active = (pend[-1] // MOE_ROWS).reshape(1)
    grp_first = g0 == jnp.sum(jnp.where(grp_expert[:, None] == eids, pstart, 0), axis=1)
    used = counts > 0
    e_slot = (jnp.cumsum(used) - 1) % 2
    later_used = jnp.logical_and(used[None, :], eids[None, :] > eids[:, None])
    e_next = jnp.min(jnp.where(later_used, eids[None, :], MOE_EXPERTS), axis=1)
    e_next = jnp.where(e_next == MOE_EXPERTS, -1, e_next)
    i32 = lambda a: a.astype(jnp.int32)
    runs = (i32(grp_expert), i32(n_active), i32(grp_first), i32(e_slot), i32(e_next))
    return i32(dest).reshape(-1), runs, i32(pstart + counts), i32(padded - counts)


def _dispatch_kernel(dest_ref, ps_ref, pn_ref, na_ref, x_ref, g_ref, xs_hbm, hbuf, zbuf, sem,
                     zsem):
    i = pl.program_id(0)
    n = pl.num_programs(0)
    tm = x_ref.shape[0]
    t = n * tm
    slot = i % 2

    def wait_slot(s):
        for _ in range(2):
            pltpu.make_async_copy(hbuf.at[s], xs_hbm.at[pl.ds(0, tm)], sem.at[s]).wait()

    def zero_row_copy(row):
        return pltpu.make_async_copy(zbuf.at[pl.ds(0, 1)], xs_hbm.at[pl.ds(row, 1)], zsem.at[0])

    def for_each_pad_row(fn):
        def per_expert(e, c):
            def per_row(r, c2):
                fn(ps_ref[e] + r)
                return c2
            return lax.fori_loop(0, pn_ref[e], per_row, c)
        lax.fori_loop(0, ps_ref.shape[0], per_expert, 0)

    def tail_group_copy(grp):
        rows = pl.ds(pl.multiple_of(grp * tm, tm), tm)
        return pltpu.make_async_copy(hbuf.at[1], xs_hbm.at[rows], sem.at[1])

    def for_each_tail_group(fn):
        def body(grp, c):
            fn(grp)
            return c
        lax.fori_loop(na_ref[0], xs_hbm.shape[0] // tm, body, 0)

    @pl.when(i == 0)
    def _():
        zbuf[...] = jnp.zeros_like(zbuf)
        for_each_pad_row(lambda row: zero_row_copy(row).start())
        hbuf[1] = jnp.zeros(hbuf.shape[1:], hbuf.dtype)
        for_each_tail_group(lambda grp: tail_group_copy(grp).start())
        for_each_tail_group(lambda grp: tail_group_copy(grp).wait())

    @pl.when(i >= 2)
    def _():
        wait_slot(slot)

    hbuf[slot] = _rms(x_ref[...], g_ref[...])

    def body(rb, c):
        r0 = pl.multiple_of(rb * ROW_UNROLL, ROW_UNROLL)
        for u in range(ROW_UNROLL):
            for k in range(2):
                row = dest_ref[k * t + i * tm + r0 + u]
                pltpu.make_async_copy(hbuf.at[slot, pl.ds(r0 + u, 1)],
                                      xs_hbm.at[pl.ds(row, 1)], sem.at[slot]).start()
        return c
    lax.fori_loop(0, tm // ROW_UNROLL, body, 0)

    @pl.when(i == n - 1)
    def _():
        wait_slot(slot)

        @pl.when(n >= 2)
        def _():
            wait_slot(1 - slot)

        for_each_pad_row(lambda row: zero_row_copy(row).wait())


def moe_dispatch(x, g, dest, pad_start, pad_len, n_active, total):
    t, d = x.shape
    tm = MOE_ROWS
    grid_spec = pltpu.PrefetchScalarGridSpec(
        num_scalar_prefetch=4,
        grid=(t // tm,),
        in_specs=[pl.BlockSpec((tm, d), lambda i, *_: (i, 0)),
                  pl.BlockSpec((1, d), lambda i, *_: (0, 0))],
        out_specs=pl.BlockSpec(memory_space=pl.ANY),
        scratch_shapes=[pltpu.VMEM((2, tm, d), F32),
                        pltpu.VMEM((8, d), F32),
                        pltpu.SemaphoreType.DMA((2,)),
                        pltpu.SemaphoreType.DMA((1,))],
    )
    return pl.pallas_call(
        _dispatch_kernel,
        grid_spec=grid_spec,
        out_shape=jax.ShapeDtypeStruct((total, d), F32),
        compiler_params=_cparams(("arbitrary",)),
        name="moe_dispatch",
    )(dest, pad_start, pad_len, n_active, x, g.reshape(1, d))


def _expert_weights_step(layer, n_mats, ge_ref, na_ref, first_ref, slot_ref, next_ref,
                         w_hbm, wbuf, wb16, sem):
    g = pl.program_id(0)
    e = ge_ref[g]

    def copies(expert, slot):
        return [pltpu.make_async_copy(w_hbm[m].at[layer, expert], wbuf.at[slot, m],
                                      sem.at[slot, m]) for m in range(n_mats)]

    @pl.when(jnp.logical_and(g < na_ref[0], first_ref[g] == 1))
    def _():
        slot = slot_ref[e]

        @pl.when(g == 0)
        def _():
            for c in copies(e, slot):
                c.start()

        for c in copies(e, slot):
            c.wait()
        nxt = next_ref[e]

        @pl.when(nxt >= 0)
        def _():
            for c in copies(nxt, 1 - slot):
                c.start()

        rows = wbuf.shape[2]
        step = 256
        for m in range(n_mats):
            for r in range(0, rows, step):
                wb16[m, r:r + step, :] = wbuf[slot, m, r:r + step, :].astype(BF16)


def _experts_up_kernel(layer, ge_ref, na_ref, first_ref, slot_ref, next_ref,
                       xs_ref, wg_hbm, wu_hbm, h_ref, wbuf, wb16, sem):
    _expert_weights_step(layer, 2, ge_ref, na_ref, first_ref, slot_ref, next_ref,
                         (wg_hbm, wu_hbm), wbuf, wb16, sem)
    g = pl.program_id(0)

    @pl.when(g < na_ref[0])
    def _():
        xb = xs_ref[...].astype(BF16)
        a = jnp.dot(xb, wb16[0], preferred_element_type=F32)
        b = jnp.dot(xb, wb16[1], preferred_element_type=F32)
        h_ref[...] = (a * jax.nn.sigmoid(a) * b).astype(h_ref.dtype)

    @pl.when(g >= na_ref[0])
    def _():
        h_ref[...] = jnp.zeros_like(h_ref)


def _experts_down_kernel(layer, ge_ref, na_ref, first_ref, slot_ref, next_ref,
                         h_ref, wd_hbm, o_ref, wbuf, wb16, sem):
    _expert_weights_step(layer, 1, ge_ref, na_ref, first_ref, slot_ref, next_ref,
                         (wd_hbm,), wbuf, wb16, sem)
    g = pl.program_id(0)

    @pl.when(g < na_ref[0])
    def _():
        o_ref[...] = jnp.dot(h_ref[...], wb16[0], preferred_element_type=F32)

    @pl.when(g >= na_ref[0])
    def _():
        o_ref[...] = jnp.zeros_like(o_ref)


def moe_experts(xs, runs, w_gate, w_up, w_down, layer):
    total, d = xs.shape
    ff = w_gate.shape[3]
    n_groups = total // MOE_ROWS
    n_pref = len(runs)

    def blk(g, ge, na, *_):
        return (jnp.maximum(jnp.minimum(g, na[0] - 1), 0), 0)

    def call(body, n_w, in_cols, w_rows, w_cols, out_dtype, name):
        return pl.pallas_call(
            functools.partial(body, layer),
            grid_spec=pltpu.PrefetchScalarGridSpec(
                num_scalar_prefetch=n_pref, grid=(n_groups,),
                in_specs=[pl.BlockSpec((MOE_ROWS, in_cols), blk)]
                + [pl.BlockSpec(memory_space=pl.ANY)] * n_w,
                out_specs=pl.BlockSpec((MOE_ROWS, w_cols), lambda g, *_: (g, 0)),
                scratch_shapes=[pltpu.VMEM((2, n_w, w_rows, w_cols), F32),
                                pltpu.VMEM((n_w, w_rows, w_cols), BF16),
                                pltpu.SemaphoreType.DMA((2, n_w))]),
            out_shape=jax.ShapeDtypeStruct((total, w_cols), out_dtype),
            compiler_params=_cparams(("arbitrary",)),
            name=name)

    hid = call(_experts_up_kernel, 2, d, d, ff, BF16, "moe_experts_up")(*runs, xs, w_gate, w_up)
    return call(_experts_down_kernel, 1, ff, ff, d, F32, "moe_experts_down")(*runs, hid, w_down)


def _combine_kernel(emit_x, emit_norm, pos_ref, yb_hbm, x_ref, p_ref, g_ref, *rest):
    out_refs, (ybuf, sem) = rest[:-2], rest[-2:]
    i = pl.program_id(0)
    n = pl.num_programs(0)
    tm = x_ref.shape[0]
    t = n * tm

    def issue(tile, slot):
        def body(rb, c):
            r0 = pl.multiple_of(rb * ROW_UNROLL, ROW_UNROLL)
            for u in range(ROW_UNROLL):
                for k in range(2):
                    row = pos_ref[k * t + tile * tm + r0 + u]
                    pltpu.make_async_copy(yb_hbm.at[pl.ds(row, 1)],
                                          ybuf.at[slot, k, pl.ds(r0 + u, 1)],
                                          sem.at[slot]).start()
            return c
        lax.fori_loop(0, tm // ROW_UNROLL, body, 0)

    @pl.when(i == 0)
    def _():
        issue(0, 0)

    @pl.when(i + 1 < n)
    def _():
        issue(i + 1, (i + 1) % 2)

    slot = i % 2
    for k in range(2):
        pltpu.make_async_copy(yb_hbm.at[pl.ds(0, tm)], ybuf.at[slot, k], sem.at[slot]).wait()
    p = p_ref[...]
    out = x_ref[...] + (p[:, 0:1] * ybuf[slot, 0] + p[:, 1:2] * ybuf[slot, 1])
    outs = ([out] if emit_x else []) + ([_rms(out, g_ref[...])] if emit_norm else [])
    for o_ref, val in zip(out_refs, outs):
        o_ref[...] = val


def moe_combine(yb, dest, gates_t, x, norm_g=None, keep_x=True, tm=256):
    t, d = x.shape
    emit_norm = norm_g is not None
    emit_x = keep_x or not emit_norm
    g = (norm_g if emit_norm else jnp.ones((d,), F32)).reshape(1, d)
    n_out = int(emit_x) + int(emit_norm)
    row_spec = pl.BlockSpec((tm, d), lambda i, pos: (i, 0))
    grid_spec = pltpu.PrefetchScalarGridSpec(
        num_scalar_prefetch=1,
        grid=(t // tm,),
        in_specs=[pl.BlockSpec(memory_space=pl.ANY),
                  row_spec,
                  pl.BlockSpec((tm, 2), lambda i, pos: (i, 0)),
                  pl.BlockSpec((1, d), lambda i, pos: (0, 0))],
        out_specs=[row_spec] * n_out,
        scratch_shapes=[pltpu.VMEM((2, 2, tm, d), F32),
                        pltpu.SemaphoreType.DMA((2,))],
    )
    outs = pl.pallas_call(
        functools.partial(_combine_kernel, emit_x, emit_norm),
        grid_spec=grid_spec,
        out_shape=[jax.ShapeDtypeStruct((t, d), F32)] * n_out,
        compiler_params=_cparams(("arbitrary",)),
        name="moe_combine",
    )(dest, yb, x, gates_t, g)
    return outs if n_out > 1 else outs[0]


def hier_moe_residual(x, g_ffn, w_coarse, b_coarse, w_fine, b_fine, w_gate, w_up, w_down,
                      layer, norm_g=None, keep_x=True):
    experts, gates, rank, counts = moe_router(x, g_ffn, w_coarse, b_coarse, w_fine, b_fine)
    dest, runs, pad_start, pad_len = _dispatch_plan(experts, rank, counts)
    total = runs[0].shape[0] * MOE_ROWS
    xs = moe_dispatch(x, g_ffn, dest, pad_start, pad_len, runs[1], total)
    yb = moe_experts(xs, runs, w_gate, w_up, w_down, layer)
    return moe_combine(yb, dest, gates.T, x, norm_g, keep_x)


def kernel(x, g_mix, g_ffn, g_final, w_in_even, conv_w, hg_lower_bound, hg_norm_g, w_out_even, s5_lambda_re, s5_lambda_im, s5_log_dt, s5_b_re, s5_b_im, s5_c_re, s5_c_im, s5_d, w_glu, moe_w_coarse, moe_b_coarse, moe_w_fine, moe_b_fine, moe_w_gate, moe_w_up, moe_w_down):
    bsz, seq, d = x.shape
    depth = g_mix.shape[0]
    conv_width = conv_w.shape[2]
    hg_width = hg_lower_bound.shape[1]
    lb_all = jnp.cumsum(jax.nn.softmax(hg_lower_bound.astype(F32), axis=0), axis=0)
    xf = x.reshape(bsz * seq, d)
    h = None
    for layer in range(depth):
        j = layer // 2
        last = layer == depth - 1
        if layer % 2 == 0:
            z = norm_matmul(xf, g_mix[layer], w_in_even[j].astype(BF16))
            o_b = hgrn2(z, lb_all[j], hg_norm_g[j], bsz, seq, hg_width, 3 * conv_width)
            xf = outproj_even(z, conv_w[j], o_b, xf, w_out_even[j].astype(BF16), seq)
        else:
            if h is None:
                h = rmsnorm_cast(xf, g_mix[layer], F32)
            tables = _s5_tables(s5_lambda_re[j], s5_lambda_im[j], s5_log_dt[j],
                                s5_b_re[j], s5_b_im[j], s5_c_re[j], s5_c_im[j])
            y = s5_mix(h, tables, s5_d[j], bsz, seq)
            xf = glu_residual(y, xf, w_glu[j].astype(BF16))
        next_s5 = not last and (layer + 1) % 2 == 1
        norm_g = g_final if last else (g_mix[layer + 1] if next_s5 else None)
        out = hier_moe_residual(
            xf, g_ffn[layer], moe_w_coarse[layer], moe_b_coarse[layer], moe_w_fine[layer],
            moe_b_fine[layer], moe_w_gate, moe_w_up, moe_w_down, layer, norm_g,
            keep_x=not last)
        xf, h = out if next_s5 else (out, None)
    return xf.reshape(bsz, seq, d)
```

```python
import functools
import math

import jax
import jax.numpy as jnp
from jax import lax
from jax.experimental import pallas as pl
from jax.experimental.pallas import tpu as pltpu

F32 = jnp.float32
BF16 = jnp.bfloat16
EPS = 1e-6

LANES = 128
HG_HEAD_DIM = 128
HG_CHUNK = 64
CONV_K = 3
S5_GROUP = 16
S5_STATE = 64
S5_CHUNK = 64
MOE_GROUPS = 4
MOE_EPG = 8
MOE_EXPERTS = MOE_GROUPS * MOE_EPG
MOE_ROWS = 256
ROUTER_ROWS = 40
ROW_UNROLL = 8
VMEM_LIMIT = 56 * 1024 * 1024


def _cparams(sem, vmem=VMEM_LIMIT):
    return pltpu.CompilerParams(dimension_semantics=sem, vmem_limit_bytes=vmem)


def _rms(x, g):
    ms = jnp.mean(x * x, axis=-1, keepdims=True)
    return x * lax.rsqrt(ms + EPS) * g


def _norm_kernel(x_ref, g_ref, o_ref):
    o_ref[...] = _rms(x_ref[...], g_ref[...]).astype(o_ref.dtype)


def rmsnorm_cast(x, g, dtype, tm=512):
    t, d = x.shape
    return pl.pallas_call(
        _norm_kernel,
        grid=(t // tm,),
        in_specs=[pl.BlockSpec((tm, d), lambda i: (i, 0)),
                  pl.BlockSpec((1, d), lambda i: (0, 0))],
        out_specs=pl.BlockSpec((tm, d), lambda i: (i, 0)),
        out_shape=jax.ShapeDtypeStruct((t, d), dtype),
        compiler_params=_cparams(("parallel",)),
        name="rmsnorm_cast",
    )(x, g.reshape(1, d))


def _norm_matmul_kernel(x_ref, g_ref, w_ref, o_ref, h_ref):
    @pl.when(pl.program_id(1) == 0)
    def _():
        h_ref[...] = _rms(x_ref[...], g_ref[...]).astype(BF16)

    o_ref[...] = jnp.dot(h_ref[...], w_ref[...],
                         preferred_element_type=F32).astype(o_ref.dtype)


def norm_matmul(x, g, w, tm=1024, tn=1024):
    t, d = x.shape
    n = w.shape[1]
    tm = min(tm, t)
    return pl.pallas_call(
        _norm_matmul_kernel,
        grid=(t // tm, n // tn),
        in_specs=[pl.BlockSpec((tm, d), lambda i, j: (i, 0)),
                  pl.BlockSpec((1, d), lambda i, j: (0, 0)),
                  pl.BlockSpec((d, tn), lambda i, j: (0, j))],
        out_specs=pl.BlockSpec((tm, tn), lambda i, j: (i, j)),
        out_shape=jax.ShapeDtypeStruct((t, n), BF16),
        scratch_shapes=[pltpu.VMEM((tm, d), BF16)],
        compiler_params=_cparams(("parallel", "arbitrary")),
        name="norm_matmul",
    )(x, g.reshape(1, d), w)


def _split2(x):
    hi = x.astype(BF16)
    return hi, (x - hi.astype(F32)).astype(BF16)


def _split3(x):
    hi = x.astype(BF16)
    r1 = x - hi.astype(F32)
    mid = r1.astype(BF16)
    lo = (r1 - mid.astype(F32)).astype(BF16)
    return hi, mid, lo


def _hgrn2_kernel(q_ref, f_ref, v_ref, g_ref, lb_ref, ng_ref, o_ref, st_ref):
    c = HG_CHUNK
    hd = HG_HEAD_DIM
    heads = st_ref.shape[0]
    n_chunks = q_ref.shape[0] // c
    ng = ng_ref[...]
    row = lax.broadcasted_iota(jnp.int32, (c, c), 0)
    col = lax.broadcasted_iota(jnp.int32, (c, c), 1)
    causal = row >= col
    tril = causal.astype(BF16)
    mid = c // 2

    @pl.when(pl.program_id(1) == 0)
    def _():
        st_ref[...] = jnp.zeros_like(st_ref)

    nt_dims = (((1,), (1,)), ((), ()))
    hs = range(heads)
    cols = [slice(h * hd, (h + 1) * hd) for h in hs]

    def step(n, carry):
        rows = pl.ds(pl.multiple_of(n * c, c), c)
        q = [q_ref[rows, cl].astype(F32) for cl in cols]
        f = [lb_ref[:, cl] + (1.0 - lb_ref[:, cl]) * jax.nn.sigmoid(f_ref[rows, cl].astype(F32))
             for cl in cols]
        k = [1.0 - fh for fh in f]
        parts = [_split3(jnp.log(fh)) for fh in f]
        b = [sum(jnp.dot(tril, p, preferred_element_type=F32) for p in ph) for ph in parts]
        ref = [bh[mid - 1:mid, :] for bh in b]
        b_last = [bh[c - 1:c, :] for bh in b]
        st = [st_ref[h] for h in hs]
        qs = [(q[h] * jnp.exp(b[h] - ref[h])).astype(BF16) for h in hs]
        ks = [(k[h] * jnp.exp(ref[h] - b[h])).astype(BF16) for h in hs]
        scores = [lax.dot_general(qs[h], ks[h], nt_dims, preferred_element_type=F32) for h in hs]
        qe = [(q[h] * jnp.exp(b[h])).astype(BF16) for h in hs]
        o_inter = [lax.dot_general(qe[h], st[h].astype(BF16), nt_dims, preferred_element_type=F32)
                   for h in hs]
        kd = [(k[h] * jnp.exp(b_last[h] - b[h])).astype(BF16) for h in hs]
        upd = [lax.dot_general(v_ref[rows, cols[h]], kd[h], (((0,), (0,)), ((), ())),
                               preferred_element_type=F32) for h in hs]
        for h in hs:
            st_ref[h] = st[h] * jnp.exp(b_last[h]) + upd[h]
        sc = [jnp.where(causal, s, 0.0).astype(BF16) for s in scores]
        o = [o_inter[h] + jnp.dot(sc[h], v_ref[rows, cols[h]], preferred_element_type=F32)
             for h in hs]
        for h in hs:
            oh = o[h] * lax.rsqrt(jnp.mean(o[h] * o[h], axis=-1, keepdims=True) + EPS) * ng
            gate = g_ref[rows, cols[h]].astype(F32)
            o_ref[rows, cols[h]] = (oh * (gate * jax.nn.sigmoid(gate))).astype(o_ref.dtype)
        return carry

    lax.fori_loop(0, n_chunks, step, 0)


def hgrn2(z, lb, norm_g, bsz, seq, width, col0, ts=1024):
    hd = HG_HEAD_DIM
    heads = width // hd
    cb = col0 // width
    ts = min(ts, seq)
    nt = seq // ts

    def zspec(k):
        return pl.BlockSpec((ts, width), lambda b, s: (b * nt + s, cb + k))

    return pl.pallas_call(
        _hgrn2_kernel,
        grid=(bsz, nt),
        in_specs=[zspec(0), zspec(1), zspec(2), zspec(3),
                  pl.BlockSpec((1, width), lambda b, s: (0, 0)),
                  pl.BlockSpec((1, hd), lambda b, s: (0, 0))],
        out_specs=pl.BlockSpec((ts, width), lambda b, s: (b * nt + s, 0)),
        out_shape=jax.ShapeDtypeStruct((bsz * seq, width), BF16),
        scratch_shapes=[pltpu.VMEM((heads, hd, hd), F32)],
        compiler_params=_cparams(("parallel", "arbitrary")),
        name="hgrn2",
    )(z, z, z, z, lb.reshape(1, width), norm_g.reshape(1, hd))


def _outproj_kernel(seq_tiles, ab_ref, ac_ref, ah_ref, hc_ref, hh_ref, cw_ref,
                    ob_ref, x_ref, wa_ref, wb_ref, o_ref):
    i = pl.program_id(0)
    u = ac_ref[...].astype(F32) * ah_ref[...].astype(F32)
    halo = hc_ref[...].astype(F32) * hh_ref[...].astype(F32)
    halo = jnp.where(i % seq_tiles == 0, 0.0, halo)
    hr = halo.shape[0]
    row = lax.broadcasted_iota(jnp.int32, u.shape, 0)
    u1 = jnp.where(row == 0, halo[hr - 1:hr, :], pltpu.roll(u, 1, axis=0))
    u2 = jnp.where(row == 0, halo[hr - 2:hr - 1, :],
                   jnp.where(row == 1, halo[hr - 1:hr, :], pltpu.roll(u, 2, axis=0)))
    cw = cw_ref[...]
    conv = cw[2:3, :] * u + cw[1:2, :] * u1 + cw[0:1, :] * u2
    ya = (ab_ref[...].astype(F32) * conv).astype(BF16)
    mix = jnp.dot(ya, wa_ref[...], preferred_element_type=F32)
    mix = mix + jnp.dot(ob_ref[...], wb_ref[...], preferred_element_type=F32)
    o_ref[...] = x_ref[...] + mix


def outproj_even(z, conv_w, o_b, x, w_out, seq, tm=512, halo=16):
    t, d = x.shape
    tm = min(tm, seq)
    cw = conv_w.shape[1]
    hw = o_b.shape[1]
    hb = tm // halo
    kern = functools.partial(_outproj_kernel, seq // tm)
    return pl.pallas_call(
        kern,
        grid=(t // tm,),
        in_specs=[pl.BlockSpec((tm, cw), lambda i: (i, 0)),
                  pl.BlockSpec((tm, cw), lambda i: (i, 1)),
                  pl.BlockSpec((tm, cw), lambda i: (i, 2)),
                  pl.BlockSpec((halo, cw), lambda i: (jnp.maximum(i * hb - 1, 0), 1)),
                  pl.BlockSpec((halo, cw), lambda i: (jnp.maximum(i * hb - 1, 0), 2)),
                  pl.BlockSpec((CONV_K, cw), lambda i: (0, 0)),
                  pl.BlockSpec((tm, hw), lambda i: (i, 0)),
                  pl.BlockSpec((tm, d), lambda i: (i, 0)),
                  pl.BlockSpec((cw, d), lambda i: (0, 0)),
                  pl.BlockSpec((hw, d), lambda i: (1, 0))],
        out_specs=pl.BlockSpec((tm, d), lambda i: (i, 0)),
        out_shape=jax.ShapeDtypeStruct((t, d), F32),
        compiler_params=_cparams(("parallel",)),
        name="outproj_even",
    )(z, z, z, z, z, conv_w, o_b, x, w_out, w_out)


def _rep_rows(x, n):
    r, l = x.shape
    return jnp.broadcast_to(x[:, None, :], (r, n, l)).reshape(r * n, l)


def _tile_rows(x, n):
    r, l = x.shape
    return jnp.broadcast_to(x[None, :, :], (n, r, l)).reshape(n * r, l)


def _s5_group(n_chunks, prm, u, m_ref):
    lc = S5_CHUNK
    gc = S5_GROUP
    half = S5_STATE
    nt_dims = (((1,), (1,)), ((), ()))
    bqa, ce1, ce2, bwa, bwb, cga, cgb = (prm[i * gc:(i + 1) * gc, :] for i in range(7))
    alpha = prm[8 * gc:8 * gc + 1, :]
    beta = prm[8 * gc + 1:8 * gc + 2, :]

    def cpow(tt):
        mag = jnp.exp(alpha * tt)
        ang = beta * tt
        return mag * jnp.cos(ang), mag * jnp.sin(ang)

    t_col = lax.broadcasted_iota(jnp.int32, (lc, LANES), 0).astype(F32)
    p0r, p0i = cpow(t_col)
    prr, pri = cpow(float(lc - 1) - t_col)
    a1r, a1i = cpow(jnp.full((1, LANES), 1.0, F32))
    p1r, p1i = p0r * a1r - p0i * a1i, p0r * a1i + p0i * a1r

    ecat = _rep_rows(p0r, gc) * _tile_rows(ce1, lc) + _rep_rows(p0i, gc) * _tile_rows(ce2, lc)
    k_row = lax.dot_general(bqa, ecat, nt_dims, precision=lax.Precision.HIGHEST,
                            preferred_element_type=F32)
    pos = lax.broadcasted_iota(jnp.int32, k_row.shape, 1)
    for s in range(lc):
        shifted = k_row if s == 0 else pltpu.roll(k_row, s * gc, axis=1)
        m_ref[s * gc:(s + 1) * gc, :] = jnp.where(pos >= s * gc, shifted, 0.0).astype(BF16)

    wcat = (_rep_rows(prr, gc) * _tile_rows(bwa, lc)
            + _rep_rows(pri, gc) * _tile_rows(bwb, lc)).astype(BF16)
    st = jnp.dot(u, wcat, preferred_element_type=F32)

    rows = st.shape[0]
    rpos = lax.broadcasted_iota(jnp.int32, (rows, LANES), 0) % n_chunks
    rlane = lax.broadcasted_iota(jnp.int32, (rows, LANES), 1)
    ar, ai = cpow(jnp.full((1, LANES), float(lc), F32))
    d = 1
    while d < n_chunks:
        sh = jnp.where(rpos >= d, pltpu.roll(st, d, axis=0), 0.0)
        a2 = jnp.where(rlane < half, -ai, ai)
        st = st + ar * sh + a2 * pltpu.roll(sh, half, axis=1)
        ar, ai = ar * ar - ai * ai, 2.0 * ar * ai
        d *= 2
    h0 = jnp.where(rpos >= 1, pltpu.roll(st, 1, axis=0), 0.0).astype(BF16)

    gcat = (_rep_rows(p1r, gc) * _tile_rows(cga, lc)
            + _rep_rows(p1i, gc) * _tile_rows(cgb, lc)).astype(BF16)
    y = jnp.dot(u, m_ref[...], preferred_element_type=F32)
    y = y + lax.dot_general(h0, gcat, nt_dims, preferred_element_type=F32)
    d_row = prm[8 * gc + 2:8 * gc + 3, :]
    return y + jnp.concatenate([d_row] * (lc * gc // LANES), axis=1) * u.astype(F32)


def _s5_kernel(n_chunks, h_ref, p_ref, perm_ref, y_ref, u_ref, m_ref):
    lc = S5_CHUNK
    gps = u_ref.shape[0]
    rows = u_ref.shape[1]
    spv = LANES // S5_GROUP
    lane_blk = lambda k: slice(k * LANES, (k + 1) * LANES)

    for j in range(lc // spv):
        xcat = jnp.concatenate(
            [h_ref[pl.ds(j * spv + s, rows, stride=lc), :] for s in range(spv)], axis=1)
        uall = jnp.dot(xcat.astype(BF16), perm_ref[...], preferred_element_type=F32).astype(BF16)
        for gi in range(gps):
            u_ref[gi, :, lane_blk(j)] = uall[:, lane_blk(gi)]

    def group(gi, carry):
        y = _s5_group(n_chunks, p_ref[gi], u_ref[gi], m_ref)
        u_ref[gi] = y.astype(BF16)
        return carry
    lax.fori_loop(0, gps, group, 0)

    for j in range(lc // spv):
        ycat = jnp.concatenate([u_ref[gi, :, lane_blk(j)] for gi in range(gps)], axis=1)
        z = jnp.dot(ycat, perm_ref[...], preferred_element_type=F32)
        for s in range(spv):
            y_ref[pl.ds(j * spv + s, rows, stride=lc), :] = z[:, lane_blk(s)]


def _s5_tables(lam_re, lam_im, log_dt, b_re, b_im, c_re, c_im, d_skip):
    dt = jnp.exp(log_dt.astype(F32))[:, None]
    lr = lam_re.astype(F32)
    li = lam_im.astype(F32)
    mag = jnp.exp(lr * dt)
    ar = mag * jnp.cos(li * dt)
    ai = mag * jnp.sin(li * dt)
    den = lr * lr + li * li
    cr = ((ar - 1.0) * lr + ai * li) / den
    ci = (ai * lr - (ar - 1.0) * li) / den
    bbr = (cr[..., None] * b_re - ci[..., None] * b_im).transpose(0, 2, 1)
    bbi = (cr[..., None] * b_im + ci[..., None] * b_re).transpose(0, 2, 1)
    cre = c_re.astype(F32)
    cim = c_im.astype(F32)
    cat = lambda a, b: jnp.concatenate([a, b], axis=-1)
    alpha = (lr * dt)[:, None, :]
    beta = (li * dt)[:, None, :]
    groups, states = lr.shape
    gc = b_re.shape[2]
    d_row = jnp.tile(d_skip.astype(F32).reshape(groups, 1, gc), (1, 1, 2 * states // gc))
    pad = jnp.zeros((groups, 5, 2 * states), F32)
    return jnp.concatenate([
        cat(bbr, -bbi), cat(cre, cim), cat(-cim, cre),
        cat(bbr, bbi), cat(-bbi, bbr),
        cat(cre, -cim), cat(-cim, -cre), jnp.zeros_like(cat(cre, cre)),
        cat(alpha, alpha), cat(beta, beta), d_row, pad], axis=1)


def s5_mix(h, tables, bsz, seq):
    t, w = h.shape
    lc, gc = S5_CHUNK, S5_GROUP
    gps = LANES // gc
    slabs = w // LANES
    n = seq // lc
    rows = bsz * n
    k = lc * gc
    idx = jnp.arange(k, dtype=jnp.int32)
    tgt = (idx % LANES) // gc * LANES + idx // LANES * gc + idx % gc
    perm = (tgt[:, None] == idx[None, :]).astype(BF16)
    tables = tables.reshape((slabs, gps) + tables.shape[1:])
    return pl.pallas_call(
        functools.partial(_s5_kernel, n),
        grid=(slabs,),
        in_specs=[pl.BlockSpec((t, LANES), lambda i: (0, i)),
                  pl.BlockSpec((None, gps) + tables.shape[2:], lambda i: (i, 0, 0, 0)),
                  pl.BlockSpec((k, k), lambda i: (0, 0))],
        out_specs=pl.BlockSpec((t, LANES), lambda i: (0, i)),
        out_shape=jax.ShapeDtypeStruct((t, w), F32),
        scratch_shapes=[pltpu.VMEM((gps, rows, k), BF16),
                        pltpu.VMEM((k, k), BF16)],
        compiler_params=_cparams(("parallel",)),
        name="s5_mix",
    )(h, tables, perm)


def _glu_kernel(tn, y_ref, x_ref, w_ref, o_ref):
    y = y_ref[...]
    cdf = 0.5 * (1.0 + jnp.tanh(math.sqrt(2.0 / math.pi) * (y + 0.044715 * (y * y * y))))
    act = (y * cdf).astype(BF16)
    d = o_ref.shape[1]
    for c in range(0, d, tn):
        za = jnp.dot(act, w_ref[:, c:c + tn], preferred_element_type=F32)
        zb = jnp.dot(act, w_ref[:, d + c:d + c + tn], preferred_element_type=F32)
        o_ref[:, c:c + tn] = x_ref[:, c:c + tn] + za * jax.nn.sigmoid(zb)


def glu_residual(y, x, w_glu, tm=512, tn=512):
    t, d = x.shape
    w = y.shape[1]
    tm = min(tm, t)
    return pl.pallas_call(
        functools.partial(_glu_kernel, tn),
        grid=(t // tm,),
        in_specs=[pl.BlockSpec((tm, w), lambda i: (i, 0)),
                  pl.BlockSpec((tm, d), lambda i: (i, 0)),
                  pl.BlockSpec((w, 2 * d), lambda i: (0, 0), pipeline_mode=pl.Buffered(1))],
        out_specs=pl.BlockSpec((tm, d), lambda i: (i, 0)),
        out_shape=jax.ShapeDtypeStruct((t, d), F32),
        compiler_params=_cparams(("parallel",)),
        name="glu_residual",
    )(y, x, w_glu)


def _router_kernel(x_ref, g_ref, w_ref, b_ref, e_ref, p_ref, r_ref, c_ref, cnt_ref):
    tm = x_ref.shape[0]

    @pl.when(pl.program_id(0) == 0)
    def _():
        cnt_ref[...] = jnp.zeros_like(cnt_ref)

    h = _rms(x_ref[...], g_ref[...])
    nt_dot = lambda a, b: lax.dot_general(a, b, (((1,), (1,)), ((), ())),
                                          preferred_element_type=F32)
    w_hi, w_lo = _split2(w_ref[...])
    h_hi, h_lo = _split2(h)
    logits = (nt_dot(w_hi, h_hi) + nt_dot(w_hi, h_lo) + nt_dot(w_lo, h_hi)
              + b_ref[...])
    lc = [logits[i:i + 1, :] for i in range(MOE_GROUPS)]
    m = functools.reduce(jnp.maximum, lc)
    grp = jnp.full(m.shape, MOE_GROUPS - 1, jnp.int32)
    for i in range(MOE_GROUPS - 2, -1, -1):
        grp = jnp.where(lc[i] == m, i, grp)
    den = functools.reduce(lambda a, b: a + b, [jnp.exp(l - m) for l in lc])
    p_top = 1.0 / den
    fine = [logits[MOE_GROUPS + j:MOE_GROUPS + j + 1, :] for j in range(MOE_EXPERTS)]
    sel = []
    for e in range(MOE_EPG):
        v = fine[(MOE_GROUPS - 1) * MOE_EPG + e]
        for i in range(MOE_GROUPS - 2, -1, -1):
            v = jnp.where(grp == i, fine[i * MOE_EPG + e], v)
        sel.append(v)

    def top1(vals):
        best = functools.reduce(jnp.maximum, vals)
        idx = jnp.full(best.shape, MOE_EPG - 1, jnp.int32)
        for e in range(MOE_EPG - 2, -1, -1):
            idx = jnp.where(vals[e] == best, e, idx)
        return best, idx

    v1, i1 = top1(sel)
    v2, i2 = top1([jnp.where(i1 == e, -jnp.inf, sel[e]) for e in range(MOE_EPG)])
    ex = jnp.exp(v2 - v1)
    s = 1.0 + ex
    e0 = grp * MOE_EPG + i1
    e1 = grp * MOE_EPG + i2
    e_ref[0:1, :] = e0
    e_ref[1:2, :] = e1
    p_ref[0:1, :] = p_top * (1.0 / s)
    p_ref[1:2, :] = p_top * (ex / s)

    eid = lax.broadcasted_iota(jnp.int32, (MOE_EXPERTS, tm), 0)
    tri = (lax.broadcasted_iota(jnp.int32, (tm, tm), 0)
           <= lax.broadcasted_iota(jnp.int32, (tm, tm), 1)).astype(BF16)
    base = cnt_ref[...]
    for k, ek in enumerate((e0, e1)):
        hot = eid == ek
        hot_f = hot.astype(F32)
        csum = jnp.dot(hot.astype(BF16), tri, preferred_element_type=F32)
        rank = jnp.sum(hot_f * (base + csum - 1.0), axis=0, keepdims=True)
        r_ref[k:k + 1, :] = rank.astype(jnp.int32)
        base = base + jnp.sum(hot_f, axis=1, keepdims=True)
    cnt_ref[...] = base
    c_ref[...] = base.astype(jnp.int32)


def moe_router(x, g, w_coarse, b_coarse, w_fine, b_fine, tm=512):
    t, d = x.shape
    pad = ROUTER_ROWS - MOE_GROUPS - MOE_EXPERTS
    w = jnp.concatenate([w_coarse.T, w_fine.T, jnp.zeros((pad, d), F32)], axis=0)
    b = jnp.concatenate([b_coarse, b_fine, jnp.zeros((pad,), F32)]).reshape(ROUTER_ROWS, 1)
    tok_spec = pl.BlockSpec((2, tm), lambda i: (0, i))
    return pl.pallas_call(
        _router_kernel,
        grid=(t // tm,),
        in_specs=[pl.BlockSpec((tm, d), lambda i: (i, 0)),
                  pl.BlockSpec((1, d), lambda i: (0, 0)),
                  pl.BlockSpec((ROUTER_ROWS, d), lambda i: (0, 0)),
                  pl.BlockSpec((ROUTER_ROWS, 1), lambda i: (0, 0))],
        out_specs=[tok_spec, tok_spec, tok_spec,
                   pl.BlockSpec((MOE_EXPERTS, 1), lambda i: (0, 0))],
        out_shape=[jax.ShapeDtypeStruct((2, t), jnp.int32),
                   jax.ShapeDtypeStruct((2, t), F32),
                   jax.ShapeDtypeStruct((2, t), jnp.int32),
                   jax.ShapeDtypeStruct((MOE_EXPERTS, 1), jnp.int32)],
        scratch_shapes=[pltpu.VMEM((MOE_EXPERTS, 1), F32)],
        compiler_params=_cparams(("arbitrary",)),
        name="moe_router",
    )(x, g.reshape(1, d), w, b)


def _dispatch_plan(experts, rank, counts):
    n_assign = experts.size
    counts = counts.reshape(-1)
    padded = (counts + MOE_ROWS - 1) // MOE_ROWS * MOE_ROWS
    pend = jnp.cumsum(padded)
    pstart = pend - padded
    eids = jnp.arange(MOE_EXPERTS, dtype=jnp.int32)
    dest = rank + jnp.sum(jnp.where(experts[..., None] == eids, pstart, 0), axis=-1)
    n_groups = -(-n_assign // MOE_ROWS) + MOE_EXPERTS
    g0 = jnp.arange(n_groups, dtype=jnp.int32) * MOE_ROWS
    grp_expert = jnp.minimum(jnp.sum(pend[None, :] <= g0[:, None], axis=1), MOE_EXPERTS - 1)
    n_active = (pend[-1] // MOE_ROWS).reshape(1)
    grp_first = g0 == jnp.sum(jnp.where(grp_expert[:, None] == eids, pstart, 0), axis=1)
    used = counts > 0
    e_slot = (jnp.cumsum(used) - 1) % 2
    later_used = jnp.logical_and(used[None, :], eids[None, :] > eids[:, None])
    e_next = jnp.min(jnp.where(later_used, eids[None, :], MOE_EXPERTS), axis=1)
    e_next = jnp.where(e_next == MOE_EXPERTS, -1, e_next)
    i32 = lambda a: a.astype(jnp.int32)
    runs = (i32(grp_expert), i32(n_active), i32(grp_first), i32(e_slot), i32(e_next))
    return i32(dest).reshape(-1), runs, i32(pstart + counts), i32(padded - counts)


def _dispatch_kernel(dest_ref, ps_ref, pn_ref, na_ref, x_ref, g_ref, xs_hbm, hbuf, zbuf, sem,
                     zsem):
    i = pl.program_id(0)
    n = pl.num_programs(0)
    tm = x_ref.shape[0]
    t = n * tm
    slot = i % 2

    def wait_slot(s):
        for _ in range(2):
            pltpu.make_async_copy(hbuf.at[s], xs_hbm.at[pl.ds(0, tm)], sem.at[s]).wait()

    def zero_row_copy(row):
        return pltpu.make_async_copy(zbuf.at[pl.ds(0, 1)], xs_hbm.at[pl.ds(row, 1)], zsem.at[0])

    def for_each_pad_row(fn):
        def per_expert(e, c):
            def per_row(r, c2):
                fn(ps_ref[e] + r)
                return c2
            return lax.fori_loop(0, pn_ref[e], per_row, c)
        lax.fori_loop(0, ps_ref.shape[0], per_expert, 0)

    def tail_group_copy(grp):
        rows = pl.ds(pl.multiple_of(grp * tm, tm), tm)
        return pltpu.make_async_copy(hbuf.at[1], xs_hbm.at[rows], sem.at[1])

    def for_each_tail_group(fn):
        def body(grp, c):
            fn(grp)
            return c
        lax.fori_loop(na_ref[0], xs_hbm.shape[0] // tm, body, 0)

    @pl.when(i == 0)
    def _():
        zbuf[...] = jnp.zeros_like(zbuf)
        for_each_pad_row(lambda row: zero_row_copy(row).start())
        hbuf[1] = jnp.zeros(hbuf.shape[1:], hbuf.dtype)
        for_each_tail_group(lambda grp: tail_group_copy(grp).start())
        for_each_tail_group(lambda grp: tail_group_copy(grp).wait())

    @pl.when(i >= 2)
    def _():
        wait_slot(slot)

    hbuf[slot] = _rms(x_ref[...], g_ref[...])

    def body(rb, c):
        r0 = pl.multiple_of(rb * ROW_UNROLL, ROW_UNROLL)
        for u in range(ROW_UNROLL):
            for k in range(2):
                row = dest_ref[k * t + i * tm + r0 + u]
                pltpu.make_async_copy(hbuf.at[slot, pl.ds(r0 + u, 1)],
                                      xs_hbm.at[pl.ds(row, 1)], sem.at[slot]).start()
        return c
    lax.fori_loop(0, tm // ROW_UNROLL, body, 0)

    @pl.when(i == n - 1)
    def _():
        wait_slot(slot)

        @pl.when(n >= 2)
        def _():
            wait_slot(1 - slot)

        for_each_pad_row(lambda row: zero_row_copy(row).wait())


def moe_dispatch(x, g, dest, pad_start, pad_len, n_active, total):
    t, d = x.shape
    tm = MOE_ROWS
    grid_spec = pltpu.PrefetchScalarGridSpec(
        num_scalar_prefetch=4,
        grid=(t // tm,),
        in_specs=[pl.BlockSpec((tm, d), lambda i, *_: (i, 0)),
                  pl.BlockSpec((1, d), lambda i, *_: (0, 0))],
        out_specs=pl.BlockSpec(memory_space=pl.ANY),
        scratch_shapes=[pltpu.VMEM((2, tm, d), F32),
                        pltpu.VMEM((8, d), F32),
                        pltpu.SemaphoreType.DMA((2,)),
                        pltpu.SemaphoreType.DMA((1,))],
    )
    return pl.pallas_call(
        _dispatch_kernel,
        grid_spec=grid_spec,
        out_shape=jax.ShapeDtypeStruct((total, d), F32),
        compiler_params=_cparams(("arbitrary",)),
        name="moe_dispatch",
    )(dest, pad_start, pad_len, n_active, x, g.reshape(1, d))


def _expert_weights_step(layer, n_mats, ge_ref, na_ref, first_ref, slot_ref, next_ref,
                         w_hbm, wbuf, wb16, sem):
    g = pl.program_id(0)
    e = ge_ref[g]

    def copies(expert, slot):
        return [pltpu.make_async_copy(w_hbm[m].at[layer, expert], wbuf.at[slot, m],
                                      sem.at[slot, m]) for m in range(n_mats)]

    @pl.when(jnp.logical_and(g < na_ref[0], first_ref[g] == 1))
    def _():
        slot = slot_ref[e]

        @pl.when(g == 0)
        def _():
            for c in copies(e, slot):
                c.start()

        for c in copies(e, slot):
            c.wait()
        nxt = next_ref[e]

        @pl.when(nxt >= 0)
        def _():
            for c in copies(nxt, 1 - slot):
                c.start()

        rows = wbuf.shape[2]
        step = 256
        for m in range(n_mats):
            for r in range(0, rows, step):
                wb16[m, r:r + step, :] = wbuf[slot, m, r:r + step, :].astype(BF16)


def _experts_up_kernel(layer, ge_ref, na_ref, first_ref, slot_ref, next_ref,
                       xs_ref, wg_hbm, wu_hbm, h_ref, wbuf, wb16, sem):
    _expert_weights_step(layer, 2, ge_ref, na_ref, first_ref, slot_ref, next_ref,
                         (wg_hbm, wu_hbm), wbuf, wb16, sem)
    g = pl.program_id(0)

    @pl.when(g < na_ref[0])
    def _():
        xb = xs_ref[...].astype(BF16)
        a = jnp.dot(xb, wb16[0], preferred_element_type=F32)
        b = jnp.dot(xb, wb16[1], preferred_element_type=F32)
        h_ref[...] = (a * jax.nn.sigmoid(a) * b).astype(h_ref.dtype)

    @pl.when(g >= na_ref[0])
    def _():
        h_ref[...] = jnp.zeros_like(h_ref)


def _experts_down_kernel(layer, ge_ref, na_ref, first_ref, slot_ref, next_ref,
                         h_ref, wd_hbm, o_ref, wbuf, wb16, sem):
    _expert_weights_step(layer, 1, ge_ref, na_ref, first_ref, slot_ref, next_ref,
                         (wd_hbm,), wbuf, wb16, sem)
    g = pl.program_id(0)

    @pl.when(g < na_ref[0])
    def _():
        o_ref[...] = jnp.dot(h_ref[...], wb16[0], preferred_element_type=F32)

    @pl.when(g >= na_ref[0])
    def _():
        o_ref[...] = jnp.zeros_like(o_ref)


def moe_experts(xs, runs, w_gate, w_up, w_down, layer):
    total, d = xs.shape
    ff = w_gate.shape[3]
    n_groups = total // MOE_ROWS
    n_pref = len(runs)

    def blk(g, ge, na, *_):
        return (jnp.maximum(jnp.minimum(g, na[0] - 1), 0), 0)

    def call(body, n_w, in_cols, w_rows, w_cols, out_dtype, name):
        return pl.pallas_call(
            functools.partial(body, layer),
            grid_spec=pltpu.PrefetchScalarGridSpec(
                num_scalar_prefetch=n_pref, grid=(n_groups,),
                in_specs=[pl.BlockSpec((MOE_ROWS, in_cols), blk)]
                + [pl.BlockSpec(memory_space=pl.ANY)] * n_w,
                out_specs=pl.BlockSpec((MOE_ROWS, w_cols), lambda g, *_: (g, 0)),
                scratch_shapes=[pltpu.VMEM((2, n_w, w_rows, w_cols), F32),
                                pltpu.VMEM((n_w, w_rows, w_cols), BF16),
                                pltpu.SemaphoreType.DMA((2, n_w))]),
            out_shape=jax.ShapeDtypeStruct((total, w_cols), out_dtype),
            compiler_params=_cparams(("arbitrary",)),
            name=name)

    hid = call(_experts_up_kernel, 2, d, d, ff, BF16, "moe_experts_up")(*runs, xs, w_gate, w_up)
    return call(_experts_down_kernel, 1, ff, ff, d, F32, "moe_experts_down")(*runs, hid, w_down)


def _combine_kernel(emit_x, emit_norm, pos_ref, yb_hbm, x_ref, p_ref, g_ref, *rest):
    out_refs, (ybuf, sem) = rest[:-2], rest[-2:]
    i = pl.program_id(0)
    n = pl.num_programs(0)
    tm = x_ref.shape[0]
    t = n * tm

    def issue(tile, slot):
        def body(rb, c):
            r0 = pl.multiple_of(rb * ROW_UNROLL, ROW_UNROLL)
            for u in range(ROW_UNROLL):
                for k in range(2):
                    row = pos_ref[k * t + tile * tm + r0 + u]
                    pltpu.make_async_copy(yb_hbm.at[pl.ds(row, 1)],
                                          ybuf.at[slot, k, pl.ds(r0 + u, 1)],
                                          sem.at[slot]).start()
            return c
        lax.fori_loop(0, tm // ROW_UNROLL, body, 0)

    @pl.when(i == 0)
    def _():
        issue(0, 0)

    @pl.when(i + 1 < n)
    def _():
        issue(i + 1, (i + 1) % 2)

    slot = i % 2
    for k in range(2):
        pltpu.make_async_copy(yb_hbm.at[pl.ds(0, tm)], ybuf.at[slot, k], sem.at[slot]).wait()
    p = p_ref[...]
    out = x_ref[...] + (p[:, 0:1] * ybuf[slot, 0] + p[:, 1:2] * ybuf[slot, 1])
    outs = ([out] if emit_x else []) + ([_rms(out, g_ref[...])] if emit_norm else [])
    for o_ref, val in zip(out_refs, outs):
        o_ref[...] = val


def moe_combine(yb, dest, gates_t, x, norm_g=None, keep_x=True, tm=256):
    t, d = x.shape
    emit_norm = norm_g is not None
    emit_x = keep_x or not emit_norm
    g = (norm_g if emit_norm else jnp.ones((d,), F32)).reshape(1, d)
    n_out = int(emit_x) + int(emit_norm)
    row_spec = pl.BlockSpec((tm, d), lambda i, pos: (i, 0))
    grid_spec = pltpu.PrefetchScalarGridSpec(
        num_scalar_prefetch=1,
        grid=(t // tm,),
        in_specs=[pl.BlockSpec(memory_space=pl.ANY),
                  row_spec,
                  pl.BlockSpec((tm, 2), lambda i, pos: (i, 0)),
                  pl.BlockSpec((1, d), lambda i, pos: (0, 0))],
        out_specs=[row_spec] * n_out,
        scratch_shapes=[pltpu.VMEM((2, 2, tm, d), F32),
                        pltpu.SemaphoreType.DMA((2,))],
    )
    outs = pl.pallas_call(
        functools.partial(_combine_kernel, emit_x, emit_norm),
        grid_spec=grid_spec,
        out_shape=[jax.ShapeDtypeStruct((t, d), F32)] * n_out,
        compiler_params=_cparams(("arbitrary",)),
        name="moe_combine",
    )(dest, yb, x, gates_t, g)
    return outs if n_out > 1 else outs[0]


def hier_moe_residual(x, g_ffn, w_coarse, b_coarse, w_fine, b_fine, w_gate, w_up, w_down,
                      layer, norm_g=None, keep_x=True):
    experts, gates, rank, counts = moe_router(x, g_ffn, w_coarse, b_coarse, w_fine, b_fine)
    dest, runs, pad_start, pad_len = _dispatch_plan(experts, rank, counts)
    total = runs[0].shape[0] * MOE_ROWS
    xs = moe_dispatch(x, g_ffn, dest, pad_start, pad_len, runs[1], total)
    yb = moe_experts(xs, runs, w_gate, w_up, w_down, layer)
    return moe_combine(yb, dest, gates.T, x, norm_g, keep_x)


def kernel(x, g_mix, g_ffn, g_final, w_in_even, conv_w, hg_lower_bound, hg_norm_g, w_out_even, s5_lambda_re, s5_lambda_im, s5_log_dt, s5_b_re, s5_b_im, s5_c_re, s5_c_im, s5_d, w_glu, moe_w_coarse, moe_b_coarse, moe_w_fine, moe_b_fine, moe_w_gate, moe_w_up, moe_w_down):
    bsz, seq, d = x.shape
    depth = g_mix.shape[0]
    conv_width = conv_w.shape[2]
    hg_width = hg_lower_bound.shape[1]
    lb_all = jnp.cumsum(jax.nn.softmax(hg_lower_bound.astype(F32), axis=0), axis=0)
    xf = x.reshape(bsz * seq, d)
    h = None
    for layer in range(depth):
        j = layer // 2
        last = layer == depth - 1
        if layer % 2 == 0:
            z = norm_matmul(xf, g_mix[layer], w_in_even[j].astype(BF16))
            o_b = hgrn2(z, lb_all[j], hg_norm_g[j], bsz, seq, hg_width, 3 * conv_width)
            xf = outproj_even(z, conv_w[j], o_b, xf, w_out_even[j].astype(BF16), seq)
        else:
            if h is None:
                h = rmsnorm_cast(xf, g_mix[layer], F32)
            tables = _s5_tables(s5_lambda_re[j], s5_lambda_im[j], s5_log_dt[j],
                                s5_b_re[j], s5_b_im[j], s5_c_re[j], s5_c_im[j], s5_d[j])
            y = s5_mix(h, tables, bsz, seq)
            xf = glu_residual(y, xf, w_glu[j].astype(BF16))
        next_s5 = not last and (layer + 1) % 2 == 1
        norm_g = g_final if last else (g_mix[layer + 1] if next_s5 else None)
        out = hier_moe_residual(
            xf, g_ffn[layer], moe_w_coarse[layer], moe_b_coarse[layer], moe_w_fine[layer],
            moe_b_fine[layer], moe_w_gate, moe_w_up, moe_w_down, layer, norm_g,
            keep_x=not last)
        xf, h = out if next_s5 else (out, None)
    return xf.reshape(bsz, seq, d)
```

```python
import functools
import math

import jax
import jax.numpy as jnp
from jax import lax
from jax.experimental import pallas as pl
from jax.experimental.pallas import tpu as pltpu

F32 = jnp.float32
BF16 = jnp.bfloat16
EPS = 1e-6

LANES = 128
HG_HEAD_DIM = 128
HG_CHUNK = 64
CONV_K = 3
S5_GROUP = 16
S5_STATE = 64
S5_CHUNK = 64
MOE_GROUPS = 4
MOE_EPG = 8
MOE_EXPERTS = MOE_GROUPS * MOE_EPG
MOE_ROWS = 256
ROUTER_ROWS = 40
ROW_UNROLL = 8
VMEM_LIMIT = 56 * 1024 * 1024


def _cparams(sem, vmem=VMEM_LIMIT):
    return pltpu.CompilerParams(dimension_semantics=sem, vmem_limit_bytes=vmem)


def _rms(x, g):
    ms = jnp.mean(x * x, axis=-1, keepdims=True)
    return x * lax.rsqrt(ms + EPS) * g


def _norm_kernel(x_ref, g_ref, o_ref):
    o_ref[...] = _rms(x_ref[...], g_ref[...]).astype(o_ref.dtype)


def rmsnorm_cast(x, g, dtype, tm=512):
    t, d = x.shape
    return pl.pallas_call(
        _norm_kernel,
        grid=(t // tm,),
        in_specs=[pl.BlockSpec((tm, d), lambda i: (i, 0)),
                  pl.BlockSpec((1, d), lambda i: (0, 0))],
        out_specs=pl.BlockSpec((tm, d), lambda i: (i, 0)),
        out_shape=jax.ShapeDtypeStruct((t, d), dtype),
        compiler_params=_cparams(("parallel",)),
        name="rmsnorm_cast",
    )(x, g.reshape(1, d))


def _norm_matmul_kernel(x_ref, g_ref, w_ref, o_ref, h_ref):
    @pl.when(pl.program_id(1) == 0)
    def _():
        h_ref[...] = _rms(x_ref[...], g_ref[...]).astype(BF16)

    o_ref[...] = jnp.dot(h_ref[...], w_ref[...],
                         preferred_element_type=F32).astype(o_ref.dtype)


def norm_matmul(x, g, w, tm=1024, tn=1024):
    t, d = x.shape
    n = w.shape[1]
    tm = min(tm, t)
    return pl.pallas_call(
        _norm_matmul_kernel,
        grid=(t // tm, n // tn),
        in_specs=[pl.BlockSpec((tm, d), lambda i, j: (i, 0)),
                  pl.BlockSpec((1, d), lambda i, j: (0, 0)),
                  pl.BlockSpec((d, tn), lambda i, j: (0, j))],
        out_specs=pl.BlockSpec((tm, tn), lambda i, j: (i, j)),
        out_shape=jax.ShapeDtypeStruct((t, n), BF16),
        scratch_shapes=[pltpu.VMEM((tm, d), BF16)],
        compiler_params=_cparams(("parallel", "arbitrary")),
        name="norm_matmul",
    )(x, g.reshape(1, d), w)


def _split2(x):
    hi = x.astype(BF16)
    return hi, (x - hi.astype(F32)).astype(BF16)


def _hgrn2_kernel(q_ref, f_ref, v_ref, g_ref, lb_ref, ng_ref, o_ref, st_ref):
    c = HG_CHUNK
    hd = HG_HEAD_DIM
    heads = st_ref.shape[0]
    n_chunks = q_ref.shape[0] // c
    ng = ng_ref[...]
    row = lax.broadcasted_iota(jnp.int32, (c, c), 0)
    col = lax.broadcasted_iota(jnp.int32, (c, c), 1)
    causal = row >= col
    tril = causal.astype(BF16)
    mid = c // 2

    @pl.when(pl.program_id(1) == 0)
    def _():
        st_ref[...] = jnp.zeros_like(st_ref)

    nt_dims = (((1,), (1,)), ((), ()))
    hs = range(heads)
    cols = [slice(h * hd, (h + 1) * hd) for h in hs]

    def step(n, carry):
        rows = pl.ds(pl.multiple_of(n * c, c), c)
        q = [q_ref[rows, cl].astype(F32) for cl in cols]
        f = [lb_ref[:, cl] + (1.0 - lb_ref[:, cl]) * jax.nn.sigmoid(f_ref[rows, cl].astype(F32))
             for cl in cols]
        k = [1.0 - fh for fh in f]
        parts = [_split2(jnp.log(fh)) for fh in f]
        b = [sum(jnp.dot(tril, p, preferred_element_type=F32) for p in ph) for ph in parts]
        ref = [bh[mid - 1:mid, :] for bh in b]
        b_last = [bh[c - 1:c, :] for bh in b]
        st = [st_ref[h] for h in hs]
        qs = [(q[h] * jnp.exp(b[h] - ref[h])).astype(BF16) for h in hs]
        ks = [(k[h] * jnp.exp(ref[h] - b[h])).astype(BF16) for h in hs]
        scores = [lax.dot_general(qs[h], ks[h], nt_dims, preferred_element_type=F32) for h in hs]
        qe = [(q[h] * jnp.exp(b[h])).astype(BF16) for h in hs]
        o_inter = [lax.dot_general(qe[h], st[h].astype(BF16), nt_dims, preferred_element_type=F32)
                   for h in hs]
        kd = [(k[h] * jnp.exp(b_last[h] - b[h])).astype(BF16) for h in hs]
        upd = [lax.dot_general(v_ref[rows, cols[h]], kd[h], (((0,), (0,)), ((), ())),
                               preferred_element_type=F32) for h in hs]
        for h in hs:
            st_ref[h] = st[h] * jnp.exp(b_last[h]) + upd[h]
        sc = [jnp.where(causal, s, 0.0).astype(BF16) for s in scores]
        o = [o_inter[h] + jnp.dot(sc[h], v_ref[rows, cols[h]], preferred_element_type=F32)
             for h in hs]
        for h in hs:
            oh = o[h] * lax.rsqrt(jnp.mean(o[h] * o[h], axis=-1, keepdims=True) + EPS) * ng
            gate = g_ref[rows, cols[h]].astype(F32)
            o_ref[rows, cols[h]] = (oh * (gate * jax.nn.sigmoid(gate))).astype(o_ref.dtype)
        return carry

    lax.fori_loop(0, n_chunks, step, 0)


def hgrn2(z, lb, norm_g, bsz, seq, width, col0, ts=1024):
    hd = HG_HEAD_DIM
    heads = width // hd
    cb = col0 // width
    ts = min(ts, seq)
    nt = seq // ts

    def zspec(k):
        return pl.BlockSpec((ts, width), lambda b, s: (b * nt + s, cb + k))

    return pl.pallas_call(
        _hgrn2_kernel,
        grid=(bsz, nt),
        in_specs=[zspec(0), zspec(1), zspec(2), zspec(3),
                  pl.BlockSpec((1, width), lambda b, s: (0, 0)),
                  pl.BlockSpec((1, hd), lambda b, s: (0, 0))],
        out_specs=pl.BlockSpec((ts, width), lambda b, s: (b * nt + s, 0)),
        out_shape=jax.ShapeDtypeStruct((bsz * seq, width), BF16),
        scratch_shapes=[pltpu.VMEM((heads, hd, hd), F32)],
        compiler_params=_cparams(("parallel", "arbitrary")),
        name="hgrn2",
    )(z, z, z, z, lb.reshape(1, width), norm_g.reshape(1, hd))


def _outproj_kernel(seq_tiles, ab_ref, ac_ref, ah_ref, hc_ref, hh_ref, cw_ref,
                    ob_ref, x_ref, wa_ref, wb_ref, o_ref):
    i = pl.program_id(0)
    u = ac_ref[...].astype(F32) * ah_ref[...].astype(F32)
    halo = hc_ref[...].astype(F32) * hh_ref[...].astype(F32)
    halo = jnp.where(i % seq_tiles == 0, 0.0, halo)
    hr = halo.shape[0]
    row = lax.broadcasted_iota(jnp.int32, u.shape, 0)
    u1 = jnp.where(row == 0, halo[hr - 1:hr, :], pltpu.roll(u, 1, axis=0))
    u2 = jnp.where(row == 0, halo[hr - 2:hr - 1, :],
                   jnp.where(row == 1, halo[hr - 1:hr, :], pltpu.roll(u, 2, axis=0)))
    cw = cw_ref[...]
    conv = cw[2:3, :] * u + cw[1:2, :] * u1 + cw[0:1, :] * u2
    ya = (ab_ref[...].astype(F32) * conv).astype(BF16)
    mix = jnp.dot(ya, wa_ref[...], preferred_element_type=F32)
    mix = mix + jnp.dot(ob_ref[...], wb_ref[...], preferred_element_type=F32)
    o_ref[...] = x_ref[...] + mix


def outproj_even(z, conv_w, o_b, x, w_out, seq, tm=512, halo=16):
    t, d = x.shape
    tm = min(tm, seq)
    cw = conv_w.shape[1]
    hw = o_b.shape[1]
    hb = tm // halo
    kern = functools.partial(_outproj_kernel, seq // tm)
    return pl.pallas_call(
        kern,
        grid=(t // tm,),
        in_specs=[pl.BlockSpec((tm, cw), lambda i: (i, 0)),
                  pl.BlockSpec((tm, cw), lambda i: (i, 1)),
                  pl.BlockSpec((tm, cw), lambda i: (i, 2)),
                  pl.BlockSpec((halo, cw), lambda i: (jnp.maximum(i * hb - 1, 0), 1)),
                  pl.BlockSpec((halo, cw), lambda i: (jnp.maximum(i * hb - 1, 0), 2)),
                  pl.BlockSpec((CONV_K, cw), lambda i: (0, 0)),
                  pl.BlockSpec((tm, hw), lambda i: (i, 0)),
                  pl.BlockSpec((tm, d), lambda i: (i, 0)),
                  pl.BlockSpec((cw, d), lambda i: (0, 0)),
                  pl.BlockSpec((hw, d), lambda i: (1, 0))],
        out_specs=pl.BlockSpec((tm, d), lambda i: (i, 0)),
        out_shape=jax.ShapeDtypeStruct((t, d), F32),
        compiler_params=_cparams(("parallel",)),
        name="outproj_even",
    )(z, z, z, z, z, conv_w, o_b, x, w_out, w_out)


def _rep_rows(x, n):
    r, l = x.shape
    return jnp.broadcast_to(x[:, None, :], (r, n, l)).reshape(r * n, l)


def _tile_rows(x, n):
    r, l = x.shape
    return jnp.broadcast_to(x[None, :, :], (n, r, l)).reshape(n * r, l)


def _s5_group(n_chunks, prm, u, m_ref):
    lc = S5_CHUNK
    gc = S5_GROUP
    half = S5_STATE
    nt_dims = (((1,), (1,)), ((), ()))
    bqa, ce1, ce2, bwa, bwb, cga, cgb = (prm[i * gc:(i + 1) * gc, :] for i in range(7))
    alpha = prm[8 * gc:8 * gc + 1, :]
    beta = prm[8 * gc + 1:8 * gc + 2, :]

    def cpow(tt):
        mag = jnp.exp(alpha * tt)
        ang = beta * tt
        return mag * jnp.cos(ang), mag * jnp.sin(ang)

    t_col = lax.broadcasted_iota(jnp.int32, (lc, LANES), 0).astype(F32)
    p0r, p0i = cpow(t_col)
    prr, pri = cpow(float(lc - 1) - t_col)
    a1r, a1i = cpow(jnp.full((1, LANES), 1.0, F32))
    p1r, p1i = p0r * a1r - p0i * a1i, p0r * a1i + p0i * a1r

    ecat = _rep_rows(p0r, gc) * _tile_rows(ce1, lc) + _rep_rows(p0i, gc) * _tile_rows(ce2, lc)
    b_hi, b_lo = _split2(bqa)
    e_hi, e_lo = _split2(ecat)
    nt_dot = lambda a, b: lax.dot_general(a, b, nt_dims, preferred_element_type=F32)
    k_row = nt_dot(b_hi, e_hi) + nt_dot(b_hi, e_lo) + nt_dot(b_lo, e_hi)
    spv = LANES // gc
    width = lc * gc
    pos = lax.broadcasted_iota(jnp.int32, k_row.shape, 1)
    for r in range(spv):
        base = k_row if r == 0 else jnp.where(pos >= r * gc, pltpu.roll(k_row, r * gc, axis=1), 0.0)
        base = base.astype(BF16)
        for q in range(lc // spv):
            s = q * spv + r
            blk = base if q == 0 else jnp.concatenate(
                [jnp.zeros((gc, q * LANES), BF16), base[:, :width - q * LANES]], axis=1)
            m_ref[s * gc:(s + 1) * gc, :] = blk

    wcat = (_rep_rows(prr, gc) * _tile_rows(bwa, lc)
            + _rep_rows(pri, gc) * _tile_rows(bwb, lc)).astype(BF16)
    st = jnp.dot(u, wcat, preferred_element_type=F32)

    rows = st.shape[0]
    rpos = lax.broadcasted_iota(jnp.int32, (rows, LANES), 0) % n_chunks
    rlane = lax.broadcasted_iota(jnp.int32, (rows, LANES), 1)
    ar, ai = cpow(jnp.full((1, LANES), float(lc), F32))
    d = 1
    while d < n_chunks:
        sh = jnp.where(rpos >= d, pltpu.roll(st, d, axis=0), 0.0)
        a2 = jnp.where(rlane < half, -ai, ai)
        st = st + ar * sh + a2 * pltpu.roll(sh, half, axis=1)
        ar, ai = ar * ar - ai * ai, 2.0 * ar * ai
        d *= 2
    h0 = jnp.where(rpos >= 1, pltpu.roll(st, 1, axis=0), 0.0).astype(BF16)

    gcat = (_rep_rows(p1r, gc) * _tile_rows(cga, lc)
            + _rep_rows(p1i, gc) * _tile_rows(cgb, lc)).astype(BF16)
    y = jnp.dot(u, m_ref[...], preferred_element_type=F32)
    y = y + lax.dot_general(h0, gcat, nt_dims, preferred_element_type=F32)
    d_row = prm[8 * gc + 2:8 * gc + 3, :]
    return y + jnp.concatenate([d_row] * (lc * gc // LANES), axis=1) * u.astype(F32)


def _s5_kernel(n_chunks, h_ref, p_ref, perm_ref, y_ref, u_ref, m_ref):
    lc = S5_CHUNK
    gps = u_ref.shape[0]
    rows = u_ref.shape[1]
    spv = LANES // S5_GROUP
    lane_blk = lambda k: slice(k * LANES, (k + 1) * LANES)

    for j in range(lc // spv):
        xcat = jnp.concatenate(
            [h_ref[pl.ds(j * spv + s, rows, stride=lc), :] for s in range(spv)], axis=1)
        uall = jnp.dot(xcat.astype(BF16), perm_ref[...], preferred_element_type=F32).astype(BF16)
        for gi in range(gps):
            u_ref[gi, :, lane_blk(j)] = uall[:, lane_blk(gi)]

    def group(gi, carry):
        y = _s5_group(n_chunks, p_ref[gi], u_ref[gi], m_ref)
        u_ref[gi] = y.astype(BF16)
        return carry
    lax.fori_loop(0, gps, group, 0)

    for j in range(lc // spv):
        ycat = jnp.concatenate([u_ref[gi, :, lane_blk(j)] for gi in range(gps)], axis=1)
        z = jnp.dot(ycat, perm_ref[...], preferred_element_type=F32)
        for s in range(spv):
            y_ref[pl.ds(j * spv + s, rows, stride=lc), :] = z[:, lane_blk(s)]


def _s5_tables(lam_re, lam_im, log_dt, b_re, b_im, c_re, c_im, d_skip):
    dt = jnp.exp(log_dt.astype(F32))[:, None]
    lr = lam_re.astype(F32)
    li = lam_im.astype(F32)
    mag = jnp.exp(lr * dt)
    ar = mag * jnp.cos(li * dt)
    ai = mag * jnp.sin(li * dt)
    den = lr * lr + li * li
    cr = ((ar - 1.0) * lr + ai * li) / den
    ci = (ai * lr - (ar - 1.0) * li) / den
    bbr = (cr[..., None] * b_re - ci[..., None] * b_im).transpose(0, 2, 1)
    bbi = (cr[..., None] * b_im + ci[..., None] * b_re).transpose(0, 2, 1)
    cre = c_re.astype(F32)
    cim = c_im.astype(F32)
    cat = lambda a, b: jnp.concatenate([a, b], axis=-1)
    alpha = (lr * dt)[:, None, :]
    beta = (li * dt)[:, None, :]
    groups, states = lr.shape
    gc = b_re.shape[2]
    d_row = jnp.tile(d_skip.astype(F32).reshape(groups, 1, gc), (1, 1, 2 * states // gc))
    pad = jnp.zeros((groups, 5, 2 * states), F32)
    return jnp.concatenate([
        cat(bbr, -bbi), cat(cre, cim), cat(-cim, cre),
        cat(bbr, bbi), cat(-bbi, bbr),
        cat(cre, -cim), cat(-cim, -cre), jnp.zeros_like(cat(cre, cre)),
        cat(alpha, alpha), cat(beta, beta), d_row, pad], axis=1)


def s5_mix(h, tables, bsz, seq):
    t, w = h.shape
    lc, gc = S5_CHUNK, S5_GROUP
    gps = LANES // gc
    slabs = w // LANES
    n = seq // lc
    rows = bsz * n
    k = lc * gc
    idx = jnp.arange(k, dtype=jnp.int32)
    tgt = (idx % LANES) // gc * LANES + idx // LANES * gc + idx % gc
    perm = (tgt[:, None] == idx[None, :]).astype(BF16)
    tables = tables.reshape((slabs, gps) + tables.shape[1:])
    return pl.pallas_call(
        functools.partial(_s5_kernel, n),
        grid=(slabs,),
        in_specs=[pl.BlockSpec((t, LANES), lambda i: (0, i)),
                  pl.BlockSpec((None, gps) + tables.shape[2:], lambda i: (i, 0, 0, 0)),
                  pl.BlockSpec((k, k), lambda i: (0, 0))],
        out_specs=pl.BlockSpec((t, LANES), lambda i: (0, i)),
        out_shape=jax.ShapeDtypeStruct((t, w), F32),
        scratch_shapes=[pltpu.VMEM((gps, rows, k), BF16),
                        pltpu.VMEM((k, k), BF16)],
        compiler_params=_cparams(("parallel",)),
        name="s5_mix",
    )(h, tables, perm)


def _glu_kernel(tn, y_ref, x_ref, w_ref, o_ref):
    y = y_ref[...]
    cdf = 0.5 * (1.0 + jnp.tanh(math.sqrt(2.0 / math.pi) * (y + 0.044715 * (y * y * y))))
    act = (y * cdf).astype(BF16)
    d = o_ref.shape[1]
    for c in range(0, d, tn):
        za = jnp.dot(act, w_ref[:, c:c + tn], preferred_element_type=F32)
        zb = jnp.dot(act, w_ref[:, d + c:d + c + tn], preferred_element_type=F32)
        o_ref[:, c:c + tn] = x_ref[:, c:c + tn] + za * jax.nn.sigmoid(zb)


def glu_residual(y, x, w_glu, tm=512, tn=512):
    t, d = x.shape
    w = y.shape[1]
    tm = min(tm, t)
    return pl.pallas_call(
        functools.partial(_glu_kernel, tn),
        grid=(t // tm,),
        in_specs=[pl.BlockSpec((tm, w), lambda i: (i, 0)),
                  pl.BlockSpec((tm, d), lambda i: (i, 0)),
                  pl.BlockSpec((w, 2 * d), lambda i: (0, 0), pipeline_mode=pl.Buffered(1))],
        out_specs=pl.BlockSpec((tm, d), lambda i: (i, 0)),
        out_shape=jax.ShapeDtypeStruct((t, d), F32),
        compiler_params=_cparams(("parallel",)),
        name="glu_residual",
    )(y, x, w_glu)


def _router_kernel(x_ref, g_ref, w_ref, b_ref, e_ref, p_ref, r_ref, c_ref, cnt_ref):
    tm = x_ref.shape[0]

    @pl.when(pl.program_id(0) == 0)
    def _():
        cnt_ref[...] = jnp.zeros_like(cnt_ref)

    h = _rms(x_ref[...], g_ref[...])
    nt_dot = lambda a, b: lax.dot_general(a, b, (((1,), (1,)), ((), ())),
                                          preferred_element_type=F32)
    w_hi, w_lo = _split2(w_ref[...])
    h_hi, h_lo = _split2(h)
    logits = (nt_dot(w_hi, h_hi) + nt_dot(w_hi, h_lo) + nt_dot(w_lo, h_hi)
              + b_ref[...])
    lc = [logits[i:i + 1, :] for i in range(MOE_GROUPS)]
    m = functools.reduce(jnp.maximum, lc)
    grp = jnp.full(m.shape, MOE_GROUPS - 1, jnp.int32)
    for i in range(MOE_GROUPS - 2, -1, -1):
        grp = jnp.where(lc[i] == m, i, grp)
    den = functools.reduce(lambda a, b: a + b, [jnp.exp(l - m) for l in lc])
    p_top = 1.0 / den
    fine = [logits[MOE_GROUPS + j:MOE_GROUPS + j + 1, :] for j in range(MOE_EXPERTS)]
    sel = []
    for e in range(MOE_EPG):
        v = fine[(MOE_GROUPS - 1) * MOE_EPG + e]
        for i in range(MOE_GROUPS - 2, -1, -1):
            v = jnp.where(grp == i, fine[i * MOE_EPG + e], v)
        sel.append(v)

    def top1(vals):
        best = functools.reduce(jnp.maximum, vals)
        idx = jnp.full(best.shape, MOE_EPG - 1, jnp.int32)
        for e in range(MOE_EPG - 2, -1, -1):
            idx = jnp.where(vals[e] == best, e, idx)
        return best, idx

    v1, i1 = top1(sel)
    v2, i2 = top1([jnp.where(i1 == e, -jnp.inf, sel[e]) for e in range(MOE_EPG)])
    ex = jnp.exp(v2 - v1)
    s = 1.0 + ex
    e0 = grp * MOE_EPG + i1
    e1 = grp * MOE_EPG + i2
    e_ref[0:1, :] = e0
    e_ref[1:2, :] = e1
    p_ref[0:1, :] = p_top * (1.0 / s)
    p_ref[1:2, :] = p_top * (ex / s)

    eid = lax.broadcasted_iota(jnp.int32, (MOE_EXPERTS, tm), 0)
    tri = (lax.broadcasted_iota(jnp.int32, (tm, tm), 0)
           <= lax.broadcasted_iota(jnp.int32, (tm, tm), 1)).astype(BF16)
    base = cnt_ref[...]
    for k, ek in enumerate((e0, e1)):
        hot = eid == ek
        hot_f = hot.astype(F32)
        csum = jnp.dot(hot.astype(BF16), tri, preferred_element_type=F32)
        rank = jnp.sum(hot_f * (base + csum - 1.0), axis=0, keepdims=True)
        r_ref[k:k + 1, :] = rank.astype(jnp.int32)
        base = base + jnp.sum(hot_f, axis=1, keepdims=True)
    cnt_ref[...] = base
    c_ref[...] = base.astype(jnp.int32)


def moe_router(x, g, w_coarse, b_coarse, w_fine, b_fine, tm=512):
    t, d = x.shape
    pad = ROUTER_ROWS - MOE_GROUPS - MOE_EXPERTS
    w = jnp.concatenate([w_coarse.T, w_fine.T, jnp.zeros((pad, d), F32)], axis=0)
    b = jnp.concatenate([b_coarse, b_fine, jnp.zeros((pad,), F32)]).reshape(ROUTER_ROWS, 1)
    tok_spec = pl.BlockSpec((2, tm), lambda i: (0, i))
    return pl.pallas_call(
        _router_kernel,
        grid=(t // tm,),
        in_specs=[pl.BlockSpec((tm, d), lambda i: (i, 0)),
                  pl.BlockSpec((1, d), lambda i: (0, 0)),
                  pl.BlockSpec((ROUTER_ROWS, d), lambda i: (0, 0)),
                  pl.BlockSpec((ROUTER_ROWS, 1), lambda i: (0, 0))],
        out_specs=[tok_spec, tok_spec, tok_spec,
                   pl.BlockSpec((MOE_EXPERTS, 1), lambda i: (0, 0))],
        out_shape=[jax.ShapeDtypeStruct((2, t), jnp.int32),
                   jax.ShapeDtypeStruct((2, t), F32),
                   jax.ShapeDtypeStruct((2, t), jnp.int32),
                   jax.ShapeDtypeStruct((MOE_EXPERTS, 1), jnp.int32)],
        scratch_shapes=[pltpu.VMEM((MOE_EXPERTS, 1), F32)],
        compiler_params=_cparams(("arbitrary",)),
        name="moe_router",
    )(x, g.reshape(1, d), w, b)


def _dispatch_plan(experts, rank, counts):
    n_assign = experts.size
    counts = counts.reshape(-1)
    padded = (counts + MOE_ROWS - 1) // MOE_ROWS * MOE_ROWS
    pend = jnp.cumsum(padded)
    pstart = pend - padded
    eids = jnp.arange(MOE_EXPERTS, dtype=jnp.int32)
    dest = rank + jnp.sum(jnp.where(experts[..., None] == eids, pstart, 0), axis=-1)
    n_groups = -(-n_assign // MOE_ROWS) + MOE_EXPERTS
    g0 = jnp.arange(n_groups, dtype=jnp.int32) * MOE_ROWS
    grp_expert = jnp.minimum(jnp.sum(pend[None, :] <= g0[:, None], axis=1), MOE_EXPERTS - 1)
    n_active = (pend[-1] // MOE_ROWS).reshape(1)
    grp_first = g0 == jnp.sum(jnp.where(grp_expert[:, None] == eids, pstart, 0), axis=1)
    used = counts > 0
    e_slot = (jnp.cumsum(used) - 1) % 2
    later_used = jnp.logical_and(used[None, :], eids[None, :] > eids[:, None])
    e_next = jnp.min(jnp.where(later_used, eids[None, :], MOE_EXPERTS), axis=1)
    e_next = jnp.where(e_next == MOE_EXPERTS, -1, e_next)
    i32 = lambda a: a.astype(jnp.int32)
    runs = (i32(grp_expert), i32(n_active), i32(grp_first), i32(e_slot), i32(e_next))
    return i32(dest).reshape(-1), runs, i32(pstart + counts), i32(padded - counts)


def _dispatch_kernel(dest_ref, ps_ref, pn_ref, na_ref, x_ref, g_ref, xs_hbm, hbuf, zbuf, sem,
                     zsem):
    i = pl.program_id(0)
    n = pl.num_programs(0)
    tm = x_ref.shape[0]
    t = n * tm
    slot = i % 2

    def wait_slot(s):
        for _ in range(2):
            pltpu.make_async_copy(hbuf.at[s], xs_hbm.at[pl.ds(0, tm)], sem.at[s]).wait()

    def zero_row_copy(row):
        return pltpu.make_async_copy(zbuf.at[pl.ds(0, 1)], xs_hbm.at[pl.ds(row, 1)], zsem.at[0])

    def for_each_pad_row(fn):
        def per_expert(e, c):
            def per_row(r, c2):
                fn(ps_ref[e] + r)
                return c2
            return lax.fori_loop(0, pn_ref[e], per_row, c)
        lax.fori_loop(0, ps_ref.shape[0], per_expert, 0)

    def tail_group_copy(grp):
        rows = pl.ds(pl.multiple_of(grp * tm, tm), tm)
        return pltpu.make_async_copy(hbuf.at[1], xs_hbm.at[rows], sem.at[1])

    def for_each_tail_group(fn):
        def body(grp, c):
            fn(grp)
            return c
        lax.fori_loop(na_ref[0], xs_hbm.shape[0] // tm, body, 0)

    @pl.when(i == 0)
    def _():
        zbuf[...] = jnp.zeros_like(zbuf)
        for_each_pad_row(lambda row: zero_row_copy(row).start())
        hbuf[1] = jnp.zeros(hbuf.shape[1:], hbuf.dtype)
        for_each_tail_group(lambda grp: tail_group_copy(grp).start())
        for_each_tail_group(lambda grp: tail_group_copy(grp).wait())

    @pl.when(i >= 2)
    def _():
        wait_slot(slot)

    hbuf[slot] = _rms(x_ref[...], g_ref[...])

    def body(rb, c):
        r0 = pl.multiple_of(rb * ROW_UNROLL, ROW_UNROLL)
        for u in range(ROW_UNROLL):
            for k in range(2):
                row = dest_ref[k * t + i * tm + r0 + u]
                pltpu.make_async_copy(hbuf.at[slot, pl.ds(r0 + u, 1)],
                                      xs_hbm.at[pl.ds(row, 1)], sem.at[slot]).start()
        return c
    lax.fori_loop(0, tm // ROW_UNROLL, body, 0)

    @pl.when(i == n - 1)
    def _():
        wait_slot(slot)

        @pl.when(n >= 2)
        def _():
            wait_slot(1 - slot)

        for_each_pad_row(lambda row: zero_row_copy(row).wait())


def moe_dispatch(x, g, dest, pad_start, pad_len, n_active, total):
    t, d = x.shape
    tm = MOE_ROWS
    grid_spec = pltpu.PrefetchScalarGridSpec(
        num_scalar_prefetch=4,
        grid=(t // tm,),
        in_specs=[pl.BlockSpec((tm, d), lambda i, *_: (i, 0)),
                  pl.BlockSpec((1, d), lambda i, *_: (0, 0))],
        out_specs=pl.BlockSpec(memory_space=pl.ANY),
        scratch_shapes=[pltpu.VMEM((2, tm, d), F32),
                        pltpu.VMEM((8, d), F32),
                        pltpu.SemaphoreType.DMA((2,)),
                        pltpu.SemaphoreType.DMA((1,))],
    )
    return pl.pallas_call(
        _dispatch_kernel,
        grid_spec=grid_spec,
        out_shape=jax.ShapeDtypeStruct((total, d), F32),
        compiler_params=_cparams(("arbitrary",)),
        name="moe_dispatch",
    )(dest, pad_start, pad_len, n_active, x, g.reshape(1, d))


def _expert_weights_step(layer, n_mats, ge_ref, na_ref, first_ref, slot_ref, next_ref,
                         w_hbm, wbuf, wb16, sem):
    g = pl.program_id(0)
    e = ge_ref[g]

    def copies(expert, slot):
        return [pltpu.make_async_copy(w_hbm[m].at[layer, expert], wbuf.at[slot, m],
                                      sem.at[slot, m]) for m in range(n_mats)]

    @pl.when(jnp.logical_and(g < na_ref[0], first_ref[g] == 1))
    def _():
        slot = slot_ref[e]

        @pl.when(g == 0)
        def _():
            for c in copies(e, slot):
                c.start()

        for c in copies(e, slot):
            c.wait()
        nxt = next_ref[e]

        @pl.when(nxt >= 0)
        def _():
            for c in copies(nxt, 1 - slot):
                c.start()

        rows = wbuf.shape[2]
        step = 256
        for m in range(n_mats):
            for r in range(0, rows, step):
                wb16[m, r:r + step, :] = wbuf[slot, m, r:r + step, :].astype(BF16)


def _experts_up_kernel(layer, ge_ref, na_ref, first_ref, slot_ref, next_ref,
                       xs_ref, wg_hbm, wu_hbm, h_ref, wbuf, wb16, sem):
    _expert_weights_step(layer, 2, ge_ref, na_ref, first_ref, slot_ref, next_ref,
                         (wg_hbm, wu_hbm), wbuf, wb16, sem)
    g = pl.program_id(0)

    @pl.when(g < na_ref[0])
    def _():
        xb = xs_ref[...].astype(BF16)
        a = jnp.dot(xb, wb16[0], preferred_element_type=F32)
        b = jnp.dot(xb, wb16[1], preferred_element_type=F32)
        h_ref[...] = (a * jax.nn.sigmoid(a) * b).astype(h_ref.dtype)

    @pl.when(g >= na_ref[0])
    def _():
        h_ref[...] = jnp.zeros_like(h_ref)


def _experts_down_kernel(layer, ge_ref, na_ref, first_ref, slot_ref, next_ref,
                         h_ref, wd_hbm, o_ref, wbuf, wb16, sem):
    _expert_weights_step(layer, 1, ge_ref, na_ref, first_ref, slot_ref, next_ref,
                         (wd_hbm,), wbuf, wb16, sem)
    g = pl.program_id(0)

    @pl.when(g < na_ref[0])
    def _():
        o_ref[...] = jnp.dot(h_ref[...], wb16[0], preferred_element_type=F32)

    @pl.when(g >= na_ref[0])
    def _():
        o_ref[...] = jnp.zeros_like(o_ref)


def moe_experts(xs, runs, w_gate, w_up, w_down, layer):
    total, d = xs.shape
    ff = w_gate.shape[3]
    n_groups = total // MOE_ROWS
    n_pref = len(runs)

    def blk(g, ge, na, *_):
        return (jnp.maximum(jnp.minimum(g, na[0] - 1), 0), 0)

    def call(body, n_w, in_cols, w_rows, w_cols, out_dtype, name):
        return pl.pallas_call(
            functools.partial(body, layer),
            grid_spec=pltpu.PrefetchScalarGridSpec(
                num_scalar_prefetch=n_pref, grid=(n_groups,),
                in_specs=[pl.BlockSpec((MOE_ROWS, in_cols), blk)]
                + [pl.BlockSpec(memory_space=pl.ANY)] * n_w,
                out_specs=pl.BlockSpec((MOE_ROWS, w_cols), lambda g, *_: (g, 0)),
                scratch_shapes=[pltpu.VMEM((2, n_w, w_rows, w_cols), F32),
                                pltpu.VMEM((n_w, w_rows, w_cols), BF16),
                                pltpu.SemaphoreType.DMA((2, n_w))]),
            out_shape=jax.ShapeDtypeStruct((total, w_cols), out_dtype),
            compiler_params=_cparams(("arbitrary",)),
            name=name)

    hid = call(_experts_up_kernel, 2, d, d, ff, BF16, "moe_experts_up")(*runs, xs, w_gate, w_up)
    return call(_experts_down_kernel, 1, ff, ff, d, F32, "moe_experts_down")(*runs, hid, w_down)


def _combine_kernel(emit_x, emit_norm, pos_ref, yb_hbm, x_ref, p_ref, g_ref, *rest):
    out_refs, (ybuf, sem) = rest[:-2], rest[-2:]
    i = pl.program_id(0)
    n = pl.num_programs(0)
    tm = x_ref.shape[0]
    t = n * tm

    def issue(tile, slot):
        def body(rb, c):
            r0 = pl.multiple_of(rb * ROW_UNROLL, ROW_UNROLL)
            for u in range(ROW_UNROLL):
                for k in range(2):
                    row = pos_ref[k * t + tile * tm + r0 + u]
                    pltpu.make_async_copy(yb_hbm.at[pl.ds(row, 1)],
                                          ybuf.at[slot, k, pl.ds(r0 + u, 1)],
                                          sem.at[slot]).start()
            return c
        lax.fori_loop(0, tm // ROW_UNROLL, body, 0)

    @pl.when(i == 0)
    def _():
        issue(0, 0)

    @pl.when(i + 1 < n)
    def _():
        issue(i + 1, (i + 1) % 2)

    slot = i % 2
    for k in range(2):
        pltpu.make_async_copy(yb_hbm.at[pl.ds(0, tm)], ybuf.at[slot, k], sem.at[slot]).wait()
    p = p_ref[...]
    out = x_ref[...] + (p[:, 0:1] * ybuf[slot, 0] + p[:, 1:2] * ybuf[slot, 1])
    outs = ([out] if emit_x else []) + ([_rms(out, g_ref[...])] if emit_norm else [])
    for o_ref, val in zip(out_refs, outs):
        o_ref[...] = val


def moe_combine(yb, dest, gates_t, x, norm_g=None, keep_x=True, tm=256):
    t, d = x.shape
    emit_norm = norm_g is not None
    emit_x = keep_x or not emit_norm
    g = (norm_g if emit_norm else jnp.ones((d,), F32)).reshape(1, d)
    n_out = int(emit_x) + int(emit_norm)
    row_spec = pl.BlockSpec((tm, d), lambda i, pos: (i, 0))
    grid_spec = pltpu.PrefetchScalarGridSpec(
        num_scalar_prefetch=1,
        grid=(t // tm,),
        in_specs=[pl.BlockSpec(memory_space=pl.ANY),
                  row_spec,
                  pl.BlockSpec((tm, 2), lambda i, pos: (i, 0)),
                  pl.BlockSpec((1, d), lambda i, pos: (0, 0))],
        out_specs=[row_spec] * n_out,
        scratch_shapes=[pltpu.VMEM((2, 2, tm, d), F32),
                        pltpu.SemaphoreType.DMA((2,))],
    )
    outs = pl.pallas_call(
        functools.partial(_combine_kernel, emit_x, emit_norm),
        grid_spec=grid_spec,
        out_shape=[jax.ShapeDtypeStruct((t, d), F32)] * n_out,
        compiler_params=_cparams(("arbitrary",)),
        name="moe_combine",
    )(dest, yb, x, gates_t, g)
    return outs if n_out > 1 else outs[0]


def hier_moe_residual(x, g_ffn, w_coarse, b_coarse, w_fine, b_fine, w_gate, w_up, w_down,
                      layer, norm_g=None, keep_x=True):
    experts, gates, rank, counts = moe_router(x, g_ffn, w_coarse, b_coarse, w_fine, b_fine)
    dest, runs, pad_start, pad_len = _dispatch_plan(experts, rank, counts)
    total = runs[0].shape[0] * MOE_ROWS
    xs = moe_dispatch(x, g_ffn, dest, pad_start, pad_len, runs[1], total)
    yb = moe_experts(xs, runs, w_gate, w_up, w_down, layer)
    return moe_combine(yb, dest, gates.T, x, norm_g, keep_x)


def kernel(x, g_mix, g_ffn, g_final, w_in_even, conv_w, hg_lower_bound, hg_norm_g, w_out_even, s5_lambda_re, s5_lambda_im, s5_log_dt, s5_b_re, s5_b_im, s5_c_re, s5_c_im, s5_d, w_glu, moe_w_coarse, moe_b_coarse, moe_w_fine, moe_b_fine, moe_w_gate, moe_w_up, moe_w_down):
    bsz, seq, d = x.shape
    depth = g_mix.shape[0]
    conv_width = conv_w.shape[2]
    hg_width = hg_lower_bound.shape[1]
    lb_all = jnp.cumsum(jax.nn.softmax(hg_lower_bound.astype(F32), axis=0), axis=0)
    xf = x.reshape(bsz * seq, d)
    h = None
    for layer in range(depth):
        j = layer // 2
        last = layer == depth - 1
        if layer % 2 == 0:
            z = norm_matmul(xf, g_mix[layer], w_in_even[j].astype(BF16))
            o_b = hgrn2(z, lb_all[j], hg_norm_g[j], bsz, seq, hg_width, 3 * conv_width)
            xf = outproj_even(z, conv_w[j], o_b, xf, w_out_even[j].astype(BF16), seq)
        else:
            if h is None:
                h = rmsnorm_cast(xf, g_mix[layer], F32)
            tables = _s5_tables(s5_lambda_re[j], s5_lambda_im[j], s5_log_dt[j],
                                s5_b_re[j], s5_b_im[j], s5_c_re[j], s5_c_im[j], s5_d[j])
            y = s5_mix(h, tables, bsz, seq)
            xf = glu_residual(y, xf, w_glu[j].astype(BF16))
        next_s5 = not last and (layer + 1) % 2 == 1
        norm_g = g_final if last else (g_mix[layer + 1] if next_s5 else None)
        out = hier_moe_residual(
            xf, g_ffn[layer], moe_w_coarse[layer], moe_b_coarse[layer], moe_w_fine[layer],
            moe_b_fine[layer], moe_w_gate, moe_w_up, moe_w_down, layer, norm_g,
            keep_x=not last)
        xf, h = out if next_s5 else (out, None)
    return xf.reshape(bsz, seq, d)
```

```python
import functools
import math

import jax
import jax.numpy as jnp
from jax import lax
from jax.experimental import pallas as pl
from jax.experimental.pallas import tpu as pltpu

F32 = jnp.float32
BF16 = jnp.bfloat16
EPS = 1e-6

LANES = 128
HG_HEAD_DIM = 128
HG_CHUNK = 64
CONV_K = 3
S5_GROUP = 16
S5_STATE = 64
S5_CHUNK = 64
MOE_GROUPS = 4
MOE_EPG = 8
MOE_EXPERTS = MOE_GROUPS * MOE_EPG
MOE_ROWS = 256
ROUTER_ROWS = 40
ROW_UNROLL = 8
VMEM_LIMIT = 56 * 1024 * 1024


def _cparams(sem, vmem=VMEM_LIMIT):
    return pltpu.CompilerParams(dimension_semantics=sem, vmem_limit_bytes=vmem)


def _rms(x, g):
    ms = jnp.mean(x * x, axis=-1, keepdims=True)
    return x * lax.rsqrt(ms + EPS) * g


def _norm_kernel(x_ref, g_ref, o_ref):
    o_ref[...] = _rms(x_ref[...], g_ref[...]).astype(o_ref.dtype)


def rmsnorm_cast(x, g, dtype, tm=512):
    t, d = x.shape
    return pl.pallas_call(
        _norm_kernel,
        grid=(t // tm,),
        in_specs=[pl.BlockSpec((tm, d), lambda i: (i, 0)),
                  pl.BlockSpec((1, d), lambda i: (0, 0))],
        out_specs=pl.BlockSpec((tm, d), lambda i: (i, 0)),
        out_shape=jax.ShapeDtypeStruct((t, d), dtype),
        compiler_params=_cparams(("parallel",)),
        name="rmsnorm_cast",
    )(x, g.reshape(1, d))


def _norm_matmul_kernel(x_ref, g_ref, w_ref, o_ref, h_ref):
    @pl.when(pl.program_id(1) == 0)
    def _():
        h_ref[...] = _rms(x_ref[...], g_ref[...]).astype(BF16)

    o_ref[...] = jnp.dot(h_ref[...], w_ref[...],
                         preferred_element_type=F32).astype(o_ref.dtype)


def norm_matmul(x, g, w, tm=1024, tn=1024):
    t, d = x.shape
    n = w.shape[1]
    tm = min(tm, t)
    return pl.pallas_call(
        _norm_matmul_kernel,
        grid=(t // tm, n // tn),
        in_specs=[pl.BlockSpec((tm, d), lambda i, j: (i, 0)),
                  pl.BlockSpec((1, d), lambda i, j: (0, 0)),
                  pl.BlockSpec((d, tn), lambda i, j: (0, j))],
        out_specs=pl.BlockSpec((tm, tn), lambda i, j: (i, j)),
        out_shape=jax.ShapeDtypeStruct((t, n), BF16),
        scratch_shapes=[pltpu.VMEM((tm, d), BF16)],
        compiler_params=_cparams(("parallel", "arbitrary")),
        name="norm_matmul",
    )(x, g.reshape(1, d), w)


def _split2(x):
    hi = x.astype(BF16)
    return hi, (x - hi.astype(F32)).astype(BF16)


def _hgrn2_kernel(q_ref, f_ref, v_ref, g_ref, lb_ref, ng_ref, o_ref, st_ref):
    c = HG_CHUNK
    hd = HG_HEAD_DIM
    heads = st_ref.shape[0]
    n_chunks = q_ref.shape[0] // c
    ng = ng_ref[...]
    row = lax.broadcasted_iota(jnp.int32, (c, c), 0)
    col = lax.broadcasted_iota(jnp.int32, (c, c), 1)
    causal = row >= col
    tril = causal.astype(BF16)
    mid = c // 2

    @pl.when(pl.program_id(1) == 0)
    def _():
        st_ref[...] = jnp.zeros_like(st_ref)

    nt_dims = (((1,), (1,)), ((), ()))
    hs = range(heads)
    cols = [slice(h * hd, (h + 1) * hd) for h in hs]

    def step(n, carry):
        rows = pl.ds(pl.multiple_of(n * c, c), c)
        q = [q_ref[rows, cl].astype(F32) for cl in cols]
        f = [lb_ref[:, cl] + (1.0 - lb_ref[:, cl]) * jax.nn.sigmoid(f_ref[rows, cl].astype(F32))
             for cl in cols]
        k = [1.0 - fh for fh in f]
        parts = [_split2(jnp.log(fh)) for fh in f]
        b = [sum(jnp.dot(tril, p, preferred_element_type=F32) for p in ph) for ph in parts]
        ref = [bh[mid - 1:mid, :] for bh in b]
        b_last = [bh[c - 1:c, :] for bh in b]
        st = [st_ref[h] for h in hs]
        qs = [(q[h] * jnp.exp(b[h] - ref[h])).astype(BF16) for h in hs]
        ks = [(k[h] * jnp.exp(ref[h] - b[h])).astype(BF16) for h in hs]
        scores = [lax.dot_general(qs[h], ks[h], nt_dims, preferred_element_type=F32) for h in hs]
        qe = [(q[h] * jnp.exp(b[h])).astype(BF16) for h in hs]
        o_inter = [lax.dot_general(qe[h], st[h].astype(BF16), nt_dims, preferred_element_type=F32)
                   for h in hs]
        kd = [(k[h] * jnp.exp(b_last[h] - b[h])).astype(BF16) for h in hs]
        upd = [lax.dot_general(v_ref[rows, cols[h]], kd[h], (((0,), (0,)), ((), ())),
                               preferred_element_type=F32) for h in hs]
        for h in hs:
            st_ref[h] = st[h] * jnp.exp(b_last[h]) + upd[h]
        sc = [jnp.where(causal, s, 0.0).astype(BF16) for s in scores]
        o = [o_inter[h] + jnp.dot(sc[h], v_ref[rows, cols[h]], preferred_element_type=F32)
             for h in hs]
        for h in hs:
            oh = o[h] * lax.rsqrt(jnp.mean(o[h] * o[h], axis=-1, keepdims=True) + EPS) * ng
            gate = g_ref[rows, cols[h]].astype(F32)
            o_ref[rows, cols[h]] = (oh * (gate * jax.nn.sigmoid(gate))).astype(o_ref.dtype)
        return carry

    lax.fori_loop(0, n_chunks, step, 0)


def hgrn2(z, lb, norm_g, bsz, seq, width, col0, ts=1024):
    hd = HG_HEAD_DIM
    heads = width // hd
    cb = col0 // width
    ts = min(ts, seq)
    nt = seq // ts

    def zspec(k):
        return pl.BlockSpec((ts, width), lambda b, s: (b * nt + s, cb + k))

    return pl.pallas_call(
        _hgrn2_kernel,
        grid=(bsz, nt),
        in_specs=[zspec(0), zspec(1), zspec(2), zspec(3),
                  pl.BlockSpec((1, width), lambda b, s: (0, 0)),
                  pl.BlockSpec((1, hd), lambda b, s: (0, 0))],
        out_specs=pl.BlockSpec((ts, width), lambda b, s: (b * nt + s, 0)),
        out_shape=jax.ShapeDtypeStruct((bsz * seq, width), BF16),
        scratch_shapes=[pltpu.VMEM((heads, hd, hd), F32)],
        compiler_params=_cparams(("parallel", "arbitrary")),
        name="hgrn2",
    )(z, z, z, z, lb.reshape(1, width), norm_g.reshape(1, hd))


def _outproj_kernel(seq_tiles, ab_ref, ac_ref, ah_ref, hc_ref, hh_ref, cw_ref,
                    ob_ref, x_ref, wa_ref, wb_ref, o_ref):
    i = pl.program_id(0)
    u = ac_ref[...].astype(F32) * ah_ref[...].astype(F32)
    halo = hc_ref[...].astype(F32) * hh_ref[...].astype(F32)
    halo = jnp.where(i % seq_tiles == 0, 0.0, halo)
    hr = halo.shape[0]
    row = lax.broadcasted_iota(jnp.int32, u.shape, 0)
    u1 = jnp.where(row == 0, halo[hr - 1:hr, :], pltpu.roll(u, 1, axis=0))
    u2 = jnp.where(row == 0, halo[hr - 2:hr - 1, :],
                   jnp.where(row == 1, halo[hr - 1:hr, :], pltpu.roll(u, 2, axis=0)))
    cw = cw_ref[...]
    conv = cw[2:3, :] * u + cw[1:2, :] * u1 + cw[0:1, :] * u2
    ya = (ab_ref[...].astype(F32) * conv).astype(BF16)
    mix = jnp.dot(ya, wa_ref[...], preferred_element_type=F32)
    mix = mix + jnp.dot(ob_ref[...], wb_ref[...], preferred_element_type=F32)
    o_ref[...] = x_ref[...] + mix


def outproj_even(z, conv_w, o_b, x, w_out, seq, tm=512, halo=16):
    t, d = x.shape
    tm = min(tm, seq)
    cw = conv_w.shape[1]
    hw = o_b.shape[1]
    hb = tm // halo
    kern = functools.partial(_outproj_kernel, seq // tm)
    return pl.pallas_call(
        kern,
        grid=(t // tm,),
        in_specs=[pl.BlockSpec((tm, cw), lambda i: (i, 0)),
                  pl.BlockSpec((tm, cw), lambda i: (i, 1)),
                  pl.BlockSpec((tm, cw), lambda i: (i, 2)),
                  pl.BlockSpec((halo, cw), lambda i: (jnp.maximum(i * hb - 1, 0), 1)),
                  pl.BlockSpec((halo, cw), lambda i: (jnp.maximum(i * hb - 1, 0), 2)),
                  pl.BlockSpec((CONV_K, cw), lambda i: (0, 0)),
                  pl.BlockSpec((tm, hw), lambda i: (i, 0)),
                  pl.BlockSpec((tm, d), lambda i: (i, 0)),
                  pl.BlockSpec((cw, d), lambda i: (0, 0)),
                  pl.BlockSpec((hw, d), lambda i: (1, 0))],
        out_specs=pl.BlockSpec((tm, d), lambda i: (i, 0)),
        out_shape=jax.ShapeDtypeStruct((t, d), F32),
        compiler_params=_cparams(("parallel",)),
        name="outproj_even",
    )(z, z, z, z, z, conv_w, o_b, x, w_out, w_out)


def _rep_rows(x, n):
    r, l = x.shape
    return jnp.broadcast_to(x[:, None, :], (r, n, l)).reshape(r * n, l)


def _tile_rows(x, n):
    r, l = x.shape
    return jnp.broadcast_to(x[None, :, :], (n, r, l)).reshape(n * r, l)


def _s5_group(n_chunks, prm, u, m_ref):
    lc = S5_CHUNK
    gc = S5_GROUP
    half = S5_STATE
    nt_dims = (((1,), (1,)), ((), ()))
    bqa, ce1, ce2, bwa, bwb, cga, cgb = (prm[i * gc:(i + 1) * gc, :] for i in range(7))
    alpha = prm[8 * gc:8 * gc + 1, :]
    beta = prm[8 * gc + 1:8 * gc + 2, :]

    def cpow(tt):
        mag = jnp.exp(alpha * tt)
        ang = beta * tt
        return mag * jnp.cos(ang), mag * jnp.sin(ang)

    t_col = lax.broadcasted_iota(jnp.int32, (lc, LANES), 0).astype(F32)
    p0r, p0i = cpow(t_col)
    prr, pri = cpow(float(lc - 1) - t_col)
    a1r, a1i = cpow(jnp.full((1, LANES), 1.0, F32))
    p1r, p1i = p0r * a1r - p0i * a1i, p0r * a1i + p0i * a1r

    ecat = _rep_rows(p0r, gc) * _tile_rows(ce1, lc) + _rep_rows(p0i, gc) * _tile_rows(ce2, lc)
    b_hi, b_lo = _split2(bqa)
    e_hi, e_lo = _split2(ecat)
    nt_dot = lambda a, b: lax.dot_general(a, b, nt_dims, preferred_element_type=F32)
    k_row = nt_dot(b_hi, e_hi) + nt_dot(b_hi, e_lo) + nt_dot(b_lo, e_hi)
    spv = LANES // gc
    width = lc * gc
    pos = lax.broadcasted_iota(jnp.int32, k_row.shape, 1)
    for r in range(spv):
        base = k_row if r == 0 else jnp.where(pos >= r * gc, pltpu.roll(k_row, r * gc, axis=1), 0.0)
        base = base.astype(BF16)
        for q in range(lc // spv):
            s = q * spv + r
            blk = base if q == 0 else jnp.concatenate(
                [jnp.zeros((gc, q * LANES), BF16), base[:, :width - q * LANES]], axis=1)
            m_ref[s * gc:(s + 1) * gc, :] = blk

    wcat = (_rep_rows(prr, gc) * _tile_rows(bwa, lc)
            + _rep_rows(pri, gc) * _tile_rows(bwb, lc)).astype(BF16)
    st = jnp.dot(u, wcat, preferred_element_type=F32)

    rows = st.shape[0]
    rpos = lax.broadcasted_iota(jnp.int32, (rows, LANES), 0) % n_chunks
    rlane = lax.broadcasted_iota(jnp.int32, (rows, LANES), 1)
    ar, ai = cpow(jnp.full((1, LANES), float(lc), F32))
    d = 1
    while d < n_chunks:
        sh = jnp.where(rpos >= d, pltpu.roll(st, d, axis=0), 0.0)
        a2 = jnp.where(rlane < half, -ai, ai)
        st = st + ar * sh + a2 * pltpu.roll(sh, half, axis=1)
        ar, ai = ar * ar - ai * ai, 2.0 * ar * ai
        d *= 2
    h0 = jnp.where(rpos >= 1, pltpu.roll(st, 1, axis=0), 0.0).astype(BF16)

    gcat = (_rep_rows(p1r, gc) * _tile_rows(cga, lc)
            + _rep_rows(p1i, gc) * _tile_rows(cgb, lc)).astype(BF16)
    y = jnp.dot(u, m_ref[...], preferred_element_type=F32)
    y = y + lax.dot_general(h0, gcat, nt_dims, preferred_element_type=F32)
    d_row = prm[8 * gc + 2:8 * gc + 3, :]
    return y + jnp.concatenate([d_row] * (lc * gc // LANES), axis=1) * u.astype(F32)


def _s5_kernel(n_chunks, h_ref, p_ref, perm_ref, y_ref, u_ref, m_ref):
    lc = S5_CHUNK
    gps = u_ref.shape[0]
    rows = u_ref.shape[1]
    spv = LANES // S5_GROUP
    lane_blk = lambda k: slice(k * LANES, (k + 1) * LANES)

    for j in range(lc // spv):
        xcat = jnp.concatenate(
            [h_ref[pl.ds(j * spv + s, rows, stride=lc), :] for s in range(spv)], axis=1)
        uall = jnp.dot(xcat.astype(BF16), perm_ref[...], preferred_element_type=F32).astype(BF16)
        for gi in range(gps):
            u_ref[gi, :, lane_blk(j)] = uall[:, lane_blk(gi)]

    def group(gi, carry):
        y = _s5_group(n_chunks, p_ref[gi], u_ref[gi], m_ref)
        u_ref[gi] = y.astype(BF16)
        return carry
    lax.fori_loop(0, gps, group, 0)

    for j in range(lc // spv):
        ycat = jnp.concatenate([u_ref[gi, :, lane_blk(j)] for gi in range(gps)], axis=1)
        z = jnp.dot(ycat, perm_ref[...], preferred_element_type=F32)
        for s in range(spv):
            y_ref[pl.ds(j * spv + s, rows, stride=lc), :] = z[:, lane_blk(s)]


def _s5_tables(lam_re, lam_im, log_dt, b_re, b_im, c_re, c_im, d_skip):
    dt = jnp.exp(log_dt.astype(F32))[:, None]
    lr = lam_re.astype(F32)
    li = lam_im.astype(F32)
    mag = jnp.exp(lr * dt)
    ar = mag * jnp.cos(li * dt)
    ai = mag * jnp.sin(li * dt)
    den = lr * lr + li * li
    cr = ((ar - 1.0) * lr + ai * li) / den
    ci = (ai * lr - (ar - 1.0) * li) / den
    bbr = (cr[..., None] * b_re - ci[..., None] * b_im).transpose(0, 2, 1)
    bbi = (cr[..., None] * b_im + ci[..., None] * b_re).transpose(0, 2, 1)
    cre = c_re.astype(F32)
    cim = c_im.astype(F32)
    cat = lambda a, b: jnp.concatenate([a, b], axis=-1)
    alpha = (lr * dt)[:, None, :]
    beta = (li * dt)[:, None, :]
    groups, states = lr.shape
    gc = b_re.shape[2]
    d_row = jnp.tile(d_skip.astype(F32).reshape(groups, 1, gc), (1, 1, 2 * states // gc))
    pad = jnp.zeros((groups, 5, 2 * states), F32)
    return jnp.concatenate([
        cat(bbr, -bbi), cat(cre, cim), cat(-cim, cre),
        cat(bbr, bbi), cat(-bbi, bbr),
        cat(cre, -cim), cat(-cim, -cre), jnp.zeros_like(cat(cre, cre)),
        cat(alpha, alpha), cat(beta, beta), d_row, pad], axis=1)


def s5_mix(h, tables, bsz, seq):
    t, w = h.shape
    lc, gc = S5_CHUNK, S5_GROUP
    gps = LANES // gc
    slabs = w // LANES
    n = seq // lc
    rows = bsz * n
    k = lc * gc
    idx = jnp.arange(k, dtype=jnp.int32)
    tgt = (idx % LANES) // gc * LANES + idx // LANES * gc + idx % gc
    perm = (tgt[:, None] == idx[None, :]).astype(BF16)
    tables = tables.reshape((slabs, gps) + tables.shape[1:])
    return pl.pallas_call(
        functools.partial(_s5_kernel, n),
        grid=(slabs,),
        in_specs=[pl.BlockSpec((t, LANES), lambda i: (0, i)),
                  pl.BlockSpec((None, gps) + tables.shape[2:], lambda i: (i, 0, 0, 0)),
                  pl.BlockSpec((k, k), lambda i: (0, 0))],
        out_specs=pl.BlockSpec((t, LANES), lambda i: (0, i)),
        out_shape=jax.ShapeDtypeStruct((t, w), F32),
        scratch_shapes=[pltpu.VMEM((gps, rows, k), BF16),
                        pltpu.VMEM((k, k), BF16)],
        compiler_params=_cparams(("parallel",)),
        name="s5_mix",
    )(h, tables, perm)


def _glu_kernel(tn, y_ref, x_ref, w_ref, o_ref):
    y = y_ref[...]
    cdf = 0.5 * (1.0 + jnp.tanh(math.sqrt(2.0 / math.pi) * (y + 0.044715 * (y * y * y))))
    act = (y * cdf).astype(BF16)
    d = o_ref.shape[1]
    for c in range(0, d, tn):
        za = jnp.dot(act, w_ref[:, c:c + tn], preferred_element_type=F32)
        zb = jnp.dot(act, w_ref[:, d + c:d + c + tn], preferred_element_type=F32)
        o_ref[:, c:c + tn] = x_ref[:, c:c + tn] + za * jax.nn.sigmoid(zb)


def glu_residual(y, x, w_glu, tm=512, tn=512):
    t, d = x.shape
    w = y.shape[1]
    tm = min(tm, t)
    return pl.pallas_call(
        functools.partial(_glu_kernel, tn),
        grid=(t // tm,),
        in_specs=[pl.BlockSpec((tm, w), lambda i: (i, 0)),
                  pl.BlockSpec((tm, d), lambda i: (i, 0)),
                  pl.BlockSpec((w, 2 * d), lambda i: (0, 0), pipeline_mode=pl.Buffered(1))],
        out_specs=pl.BlockSpec((tm, d), lambda i: (i, 0)),
        out_shape=jax.ShapeDtypeStruct((t, d), F32),
        compiler_params=_cparams(("parallel",)),
        name="glu_residual",
    )(y, x, w_glu)


def _router_kernel(x_ref, g_ref, w_ref, b_ref, e_ref, p_ref, r_ref, c_ref, cnt_ref):
    tm = x_ref.shape[0]

    @pl.when(pl.program_id(0) == 0)
    def _():
        cnt_ref[...] = jnp.zeros_like(cnt_ref)

    h = _rms(x_ref[...], g_ref[...])
    nt_dot = lambda a, b: lax.dot_general(a, b, (((1,), (1,)), ((), ())),
                                          preferred_element_type=F32)
    w_hi, w_lo = _split2(w_ref[...])
    h_hi, h_lo = _split2(h)
    logits = (nt_dot(w_hi, h_hi) + nt_dot(w_hi, h_lo) + nt_dot(w_lo, h_hi)
              + b_ref[...])
    lc = [logits[i:i + 1, :] for i in range(MOE_GROUPS)]
    m = functools.reduce(jnp.maximum, lc)
    grp = jnp.full(m.shape, MOE_GROUPS - 1, jnp.int32)
    for i in range(MOE_GROUPS - 2, -1, -1):
        grp = jnp.where(lc[i] == m, i, grp)
    den = functools.reduce(lambda a, b: a + b, [jnp.exp(l - m) for l in lc])
    p_top = 1.0 / den
    fine = [logits[MOE_GROUPS + j:MOE_GROUPS + j + 1, :] for j in range(MOE_EXPERTS)]
    sel = []
    for e in range(MOE_EPG):
        v = fine[(MOE_GROUPS - 1) * MOE_EPG + e]
        for i in range(MOE_GROUPS - 2, -1, -1):
            v = jnp.where(grp == i, fine[i * MOE_EPG + e], v)
        sel.append(v)

    def top1(vals):
        best = functools.reduce(jnp.maximum, vals)
        idx = jnp.full(best.shape, MOE_EPG - 1, jnp.int32)
        for e in range(MOE_EPG - 2, -1, -1):
            idx = jnp.where(vals[e] == best, e, idx)
        return best, idx

    v1, i1 = top1(sel)
    v2, i2 = top1([jnp.where(i1 == e, -jnp.inf, sel[e]) for e in range(MOE_EPG)])
    ex = jnp.exp(v2 - v1)
    s = 1.0 + ex
    e0 = grp * MOE_EPG + i1
    e1 = grp * MOE_EPG + i2
    e_ref[0:1, :] = e0
    e_ref[1:2, :] = e1
    p_ref[0:1, :] = p_top * (1.0 / s)
    p_ref[1:2, :] = p_top * (ex / s)

    eid = lax.broadcasted_iota(jnp.int32, (MOE_EXPERTS, tm), 0)
    tri = (lax.broadcasted_iota(jnp.int32, (tm, tm), 0)
           <= lax.broadcasted_iota(jnp.int32, (tm, tm), 1)).astype(BF16)
    base = cnt_ref[...]
    for k, ek in enumerate((e0, e1)):
        hot = eid == ek
        hot_f = hot.astype(F32)
        csum = jnp.dot(hot.astype(BF16), tri, preferred_element_type=F32)
        rank = jnp.sum(hot_f * (base + csum - 1.0), axis=0, keepdims=True)
        r_ref[k:k + 1, :] = rank.astype(jnp.int32)
        base = base + jnp.sum(hot_f, axis=1, keepdims=True)
    cnt_ref[...] = base
    c_ref[...] = base.astype(jnp.int32)


def moe_router(x, g, w_coarse, b_coarse, w_fine, b_fine, tm=512):
    t, d = x.shape
    pad = ROUTER_ROWS - MOE_GROUPS - MOE_EXPERTS
    w = jnp.concatenate([w_coarse.T, w_fine.T, jnp.zeros((pad, d), F32)], axis=0)
    b = jnp.concatenate([b_coarse, b_fine, jnp.zeros((pad,), F32)]).reshape(ROUTER_ROWS, 1)
    tok_spec = pl.BlockSpec((2, tm), lambda i: (0, i))
    return pl.pallas_call(
        _router_kernel,
        grid=(t // tm,),
        in_specs=[pl.BlockSpec((tm, d), lambda i: (i, 0)),
                  pl.BlockSpec((1, d), lambda i: (0, 0)),
                  pl.BlockSpec((ROUTER_ROWS, d), lambda i: (0, 0)),
                  pl.BlockSpec((ROUTER_ROWS, 1), lambda i: (0, 0))],
        out_specs=[tok_spec, tok_spec, tok_spec,
                   pl.BlockSpec((MOE_EXPERTS, 1), lambda i: (0, 0))],
        out_shape=[jax.ShapeDtypeStruct((2, t), jnp.int32),
                   jax.ShapeDtypeStruct((2, t), F32),
                   jax.ShapeDtypeStruct((2, t), jnp.int32),
                   jax.ShapeDtypeStruct((MOE_EXPERTS, 1), jnp.int32)],
        scratch_shapes=[pltpu.VMEM((MOE_EXPERTS, 1), F32)],
        compiler_params=_cparams(("arbitrary",)),
        name="moe_router",
    )(x, g.reshape(1, d), w, b)


def _dispatch_plan(experts, rank, counts):
    n_assign = experts.size
    counts = counts.reshape(-1)
    padded = (counts + MOE_ROWS - 1) // MOE_ROWS * MOE_ROWS
    pend = jnp.cumsum(padded)
    pstart = pend - padded
    eids = jnp.arange(MOE_EXPERTS, dtype=jnp.int32)
    dest = rank + jnp.sum(jnp.where(experts[..., None] == eids, pstart, 0), axis=-1)
    n_groups = -(-n_assign // MOE_ROWS) + MOE_EXPERTS
    g0 = jnp.arange(n_groups, dtype=jnp.int32) * MOE_ROWS
    grp_expert = jnp.minimum(jnp.sum(pend[None, :] <= g0[:, None], axis=1), MOE_EXPERTS - 1)
    n_active = (pend[-1] // MOE_ROWS).reshape(1)
    of_group = lambda v: jnp.sum(jnp.where(grp_expert[:, None] == eids, v, 0), axis=1)
    grp_pos = (g0 - of_group(pstart)) // MOE_ROWS
    grp_len = of_group(padded) // MOE_ROWS
    used = counts > 0
    e_slot = (jnp.cumsum(used) - 1) % 2
    later_used = jnp.logical_and(used[None, :], eids[None, :] > eids[:, None])
    e_next = jnp.min(jnp.where(later_used, eids[None, :], MOE_EXPERTS), axis=1)
    e_next = jnp.where(e_next == MOE_EXPERTS, -1, e_next)
    i32 = lambda a: a.astype(jnp.int32)
    runs = (i32(grp_expert), i32(n_active), i32(grp_pos), i32(grp_len), i32(e_slot), i32(e_next))
    return i32(dest).reshape(-1), runs, i32(pstart + counts), i32(padded - counts)


def _dispatch_kernel(dest_ref, ps_ref, pn_ref, na_ref, x_ref, g_ref, xs_hbm, hbuf, zbuf, sem,
                     zsem):
    i = pl.program_id(0)
    n = pl.num_programs(0)
    tm = x_ref.shape[0]
    t = n * tm
    slot = i % 2

    def wait_slot(s):
        for _ in range(2):
            pltpu.make_async_copy(hbuf.at[s], xs_hbm.at[pl.ds(0, tm)], sem.at[s]).wait()

    def zero_row_copy(row):
        return pltpu.make_async_copy(zbuf.at[pl.ds(0, 1)], xs_hbm.at[pl.ds(row, 1)], zsem.at[0])

    def for_each_pad_row(fn):
        def per_expert(e, c):
            def per_row(r, c2):
                fn(ps_ref[e] + r)
                return c2
            return lax.fori_loop(0, pn_ref[e], per_row, c)
        lax.fori_loop(0, ps_ref.shape[0], per_expert, 0)

    def tail_group_copy(grp):
        rows = pl.ds(pl.multiple_of(grp * tm, tm), tm)
        return pltpu.make_async_copy(hbuf.at[1], xs_hbm.at[rows], sem.at[1])

    def for_each_tail_group(fn):
        def body(grp, c):
            fn(grp)
            return c
        lax.fori_loop(na_ref[0], xs_hbm.shape[0] // tm, body, 0)

    @pl.when(i == 0)
    def _():
        zbuf[...] = jnp.zeros_like(zbuf)
        for_each_pad_row(lambda row: zero_row_copy(row).start())
        hbuf[1] = jnp.zeros(hbuf.shape[1:], hbuf.dtype)
        for_each_tail_group(lambda grp: tail_group_copy(grp).start())
        for_each_tail_group(lambda grp: tail_group_copy(grp).wait())

    @pl.when(i >= 2)
    def _():
        wait_slot(slot)

    hbuf[slot] = _rms(x_ref[...], g_ref[...])

    def body(rb, c):
        r0 = pl.multiple_of(rb * ROW_UNROLL, ROW_UNROLL)
        for u in range(ROW_UNROLL):
            for k in range(2):
                row = dest_ref[k * t + i * tm + r0 + u]
                pltpu.make_async_copy(hbuf.at[slot, pl.ds(r0 + u, 1)],
                                      xs_hbm.at[pl.ds(row, 1)], sem.at[slot]).start()
        return c
    lax.fori_loop(0, tm // ROW_UNROLL, body, 0)

    @pl.when(i == n - 1)
    def _():
        wait_slot(slot)

        @pl.when(n >= 2)
        def _():
            wait_slot(1 - slot)

        for_each_pad_row(lambda row: zero_row_copy(row).wait())


def moe_dispatch(x, g, dest, pad_start, pad_len, n_active, total):
    t, d = x.shape
    tm = MOE_ROWS
    grid_spec = pltpu.PrefetchScalarGridSpec(
        num_scalar_prefetch=4,
        grid=(t // tm,),
        in_specs=[pl.BlockSpec((tm, d), lambda i, *_: (i, 0)),
                  pl.BlockSpec((1, d), lambda i, *_: (0, 0))],
        out_specs=pl.BlockSpec(memory_space=pl.ANY),
        scratch_shapes=[pltpu.VMEM((2, tm, d), F32),
                        pltpu.VMEM((8, d), F32),
                        pltpu.SemaphoreType.DMA((2,)),
                        pltpu.SemaphoreType.DMA((1,))],
    )
    return pl.pallas_call(
        _dispatch_kernel,
        grid_spec=grid_spec,
        out_shape=jax.ShapeDtypeStruct((total, d), F32),
        compiler_params=_cparams(("arbitrary",)),
        name="moe_dispatch",
    )(dest, pad_start, pad_len, n_active, x, g.reshape(1, d))


def _experts_kernel(layer, ge_ref, na_ref, pos_ref, len_ref, slot_ref, next_ref,
                    xs_ref, wg_hbm, wu_hbm, wd_hbm, o_ref, wg16, wu16, wd16, stage, sem):
    g = pl.program_id(0)
    e = ge_ref[g]
    half = stage.shape[1]
    w_hbm = (wg_hbm, wu_hbm, wd_hbm)
    w16 = (wg16, wu16, wd16)

    def half_src(p, expert, j):
        w = w_hbm[p].at[layer, expert]
        rows = pl.ds(j * half, half)
        return w.at[:, rows] if p == 2 else w.at[rows, :]

    def copies(p, expert):
        return [pltpu.make_async_copy(half_src(p, expert, j), stage.at[j], sem.at[j])
                for j in range(2)]

    def issue(p, expert):
        for c in copies(p, expert):
            c.start()

    def consume(p, expert, slot):
        for j, c in enumerate(copies(p, expert)):
            c.wait()
            for r in range(0, half, 256):
                v = stage[j, r:r + 256, :].astype(BF16)
                if p == 2:
                    w16[p][slot, r:r + 256, j * half:(j + 1) * half] = v
                else:
                    w16[p][slot, j * half + r:j * half + r + 256, :] = v

    @pl.when(g < na_ref[0])
    def _():
        slot = slot_ref[e]
        k = pos_ref[g]
        last = k == len_ref[g] - 1
        nxt = next_ref[e]
        has_next = nxt >= 0

        @pl.when(g == 0)
        def _():
            for p in range(3):
                issue(p, e)
                consume(p, e, slot)

        for p in range(3):
            @pl.when(jnp.logical_and(has_next, k == p + 1))
            def _():
                consume(p, nxt, 1 - slot)

        for p in range(3):
            @pl.when(jnp.logical_and(has_next, jnp.logical_or(k == p, jnp.logical_and(last, k < p))))
            def _():
                issue(p, nxt)

            @pl.when(jnp.logical_and(has_next, jnp.logical_and(last, k <= p)))
            def _():
                consume(p, nxt, 1 - slot)

        xb = xs_ref[...].astype(BF16)
        a = jnp.dot(xb, wg16[slot], preferred_element_type=F32)
        b = jnp.dot(xb, wu16[slot], preferred_element_type=F32)
        hid = (a * jax.nn.sigmoid(a) * b).astype(BF16)
        o_ref[...] = jnp.dot(hid, wd16[slot], preferred_element_type=F32)

    @pl.when(g >= na_ref[0])
    def _():
        o_ref[...] = jnp.zeros_like(o_ref)


def moe_experts(xs, runs, w_gate, w_up, w_down, layer):
    total, d = xs.shape
    ff = w_gate.shape[3]
    assert d == 2 * ff, "half of every weight matrix must have the staging shape (ff, ff)"
    n_groups = total // MOE_ROWS

    def blk(g, ge, na, *_):
        return (jnp.maximum(jnp.minimum(g, na[0] - 1), 0), 0)

    return pl.pallas_call(
        functools.partial(_experts_kernel, layer),
        grid_spec=pltpu.PrefetchScalarGridSpec(
            num_scalar_prefetch=len(runs), grid=(n_groups,),
            in_specs=[pl.BlockSpec((MOE_ROWS, d), blk)] + [pl.BlockSpec(memory_space=pl.ANY)] * 3,
            out_specs=pl.BlockSpec((MOE_ROWS, d), lambda g, *_: (g, 0)),
            scratch_shapes=[pltpu.VMEM((2, d, ff), BF16),
                            pltpu.VMEM((2, d, ff), BF16),
                            pltpu.VMEM((2, ff, d), BF16),
                            pltpu.VMEM((2, ff, ff), F32),
                            pltpu.SemaphoreType.DMA((2,))]),
        out_shape=jax.ShapeDtypeStruct((total, d), F32),
        compiler_params=_cparams(("arbitrary",)),
        name="moe_experts",
    )(*runs, xs, w_gate, w_up, w_down)


def _combine_kernel(emit_x, emit_norm, pos_ref, yb_hbm, x_ref, p_ref, g_ref, *rest):
    out_refs, (ybuf, sem) = rest[:-2], rest[-2:]
    i = pl.program_id(0)
    n = pl.num_programs(0)
    tm = x_ref.shape[0]
    t = n * tm

    def issue(tile, slot):
        def body(rb, c):
            r0 = pl.multiple_of(rb * ROW_UNROLL, ROW_UNROLL)
            for u in range(ROW_UNROLL):
                for k in range(2):
                    row = pos_ref[k * t + tile * tm + r0 + u]
                    pltpu.make_async_copy(yb_hbm.at[pl.ds(row, 1)],
                                          ybuf.at[slot, k, pl.ds(r0 + u, 1)],
                                          sem.at[slot]).start()
            return c
        lax.fori_loop(0, tm // ROW_UNROLL, body, 0)

    @pl.when(i == 0)
    def _():
        issue(0, 0)

    @pl.when(i + 1 < n)
    def _():
        issue(i + 1, (i + 1) % 2)

    slot = i % 2
    for k in range(2):
        pltpu.make_async_copy(yb_hbm.at[pl.ds(0, tm)], ybuf.at[slot, k], sem.at[slot]).wait()
    p = p_ref[...]
    out = x_ref[...] + (p[:, 0:1] * ybuf[slot, 0] + p[:, 1:2] * ybuf[slot, 1])
    outs = ([out] if emit_x else []) + ([_rms(out, g_ref[...])] if emit_norm else [])
    for o_ref, val in zip(out_refs, outs):
        o_ref[...] = val


def moe_combine(yb, dest, gates_t, x, norm_g=None, keep_x=True, tm=256):
    t, d = x.shape
    emit_norm = norm_g is not None
    emit_x = keep_x or not emit_norm
    g = (norm_g if emit_norm else jnp.ones((d,), F32)).reshape(1, d)
    n_out = int(emit_x) + int(emit_norm)
    row_spec = pl.BlockSpec((tm, d), lambda i, pos: (i, 0))
    grid_spec = pltpu.PrefetchScalarGridSpec(
        num_scalar_prefetch=1,
        grid=(t // tm,),
        in_specs=[pl.BlockSpec(memory_space=pl.ANY),
                  row_spec,
                  pl.BlockSpec((tm, 2), lambda i, pos: (i, 0)),
                  pl.BlockSpec((1, d), lambda i, pos: (0, 0))],
        out_specs=[row_spec] * n_out,
        scratch_shapes=[pltpu.VMEM((2, 2, tm, d), F32),
                        pltpu.SemaphoreType.DMA((2,))],
    )
    outs = pl.pallas_call(
        functools.partial(_combine_kernel, emit_x, emit_norm),
        grid_spec=grid_spec,
        out_shape=[jax.ShapeDtypeStruct((t, d), F32)] * n_out,
        compiler_params=_cparams(("arbitrary",)),
        name="moe_combine",
    )(dest, yb, x, gates_t, g)
    return outs if n_out > 1 else outs[0]


def hier_moe_residual(x, g_ffn, w_coarse, b_coarse, w_fine, b_fine, w_gate, w_up, w_down,
                      layer, norm_g=None, keep_x=True):
    experts, gates, rank, counts = moe_router(x, g_ffn, w_coarse, b_coarse, w_fine, b_fine)
    dest, runs, pad_start, pad_len = _dispatch_plan(experts, rank, counts)
    total = runs[0].shape[0] * MOE_ROWS
    xs = moe_dispatch(x, g_ffn, dest, pad_start, pad_len, runs[1], total)
    yb = moe_experts(xs, runs, w_gate, w_up, w_down, layer)
    return moe_combine(yb, dest, gates.T, x, norm_g, keep_x)


def kernel(x, g_mix, g_ffn, g_final, w_in_even, conv_w, hg_lower_bound, hg_norm_g, w_out_even, s5_lambda_re, s5_lambda_im, s5_log_dt, s5_b_re, s5_b_im, s5_c_re, s5_c_im, s5_d, w_glu, moe_w_coarse, moe_b_coarse, moe_w_fine, moe_b_fine, moe_w_gate, moe_w_up, moe_w_down):
    bsz, seq, d = x.shape
    depth = g_mix.shape[0]
    conv_width = conv_w.shape[2]
    hg_width = hg_lower_bound.shape[1]
    lb_all = jnp.cumsum(jax.nn.softmax(hg_lower_bound.astype(F32), axis=0), axis=0)
    xf = x.reshape(bsz * seq, d)
    h = None
    for layer in range(depth):
        j = layer // 2
        last = layer == depth - 1
        if layer % 2 == 0:
            z = norm_matmul(xf, g_mix[layer], w_in_even[j].astype(BF16))
            o_b = hgrn2(z, lb_all[j], hg_norm_g[j], bsz, seq, hg_width, 3 * conv_width)
            xf = outproj_even(z, conv_w[j], o_b, xf, w_out_even[j].astype(BF16), seq)
        else:
            if h is None:
                h = rmsnorm_cast(xf, g_mix[layer], F32)
            tables = _s5_tables(s5_lambda_re[j], s5_lambda_im[j], s5_log_dt[j],
                                s5_b_re[j], s5_b_im[j], s5_c_re[j], s5_c_im[j], s5_d[j])
            y = s5_mix(h, tables, bsz, seq)
            xf = glu_residual(y, xf, w_glu[j].astype(BF16))
        next_s5 = not last and (layer + 1) % 2 == 1
        norm_g = g_final if last else (g_mix[layer + 1] if next_s5 else None)
        out = hier_moe_residual(
            xf, g_ffn[layer], moe_w_coarse[layer], moe_b_coarse[layer], moe_w_fine[layer],
            moe_b_fine[layer], moe_w_gate, moe_w_up, moe_w_down, layer, norm_g,
            keep_x=not last)
        xf, h = out if next_s5 else (out, None)
    return xf.reshape(bsz, seq, d)
```

```python
import functools
import math

import jax
import jax.numpy as jnp
from jax import lax
from jax.experimental import pallas as pl
from jax.experimental.pallas import tpu as pltpu

F32 = jnp.float32
BF16 = jnp.bfloat16
EPS = 1e-6

LANES = 128
HG_HEAD_DIM = 128
HG_CHUNK = 64
CONV_K = 3
S5_GROUP = 16
S5_STATE = 64
S5_CHUNK = 64
MOE_GROUPS = 4
MOE_EPG = 8
MOE_EXPERTS = MOE_GROUPS * MOE_EPG
MOE_ROWS = 256
ROUTER_ROWS = 40
ROW_UNROLL = 8
VMEM_LIMIT = 56 * 1024 * 1024


def _cparams(sem, vmem=VMEM_LIMIT):
    return pltpu.CompilerParams(dimension_semantics=sem, vmem_limit_bytes=vmem)


def _rms(x, g):
    ms = jnp.mean(x * x, axis=-1, keepdims=True)
    return x * lax.rsqrt(ms + EPS) * g


def _norm_kernel(x_ref, g_ref, o_ref):
    o_ref[...] = _rms(x_ref[...], g_ref[...]).astype(o_ref.dtype)


def rmsnorm_cast(x, g, dtype, tm=512):
    t, d = x.shape
    return pl.pallas_call(
        _norm_kernel,
        grid=(t // tm,),
        in_specs=[pl.BlockSpec((tm, d), lambda i: (i, 0)),
                  pl.BlockSpec((1, d), lambda i: (0, 0))],
        out_specs=pl.BlockSpec((tm, d), lambda i: (i, 0)),
        out_shape=jax.ShapeDtypeStruct((t, d), dtype),
        compiler_params=_cparams(("parallel",)),
        name="rmsnorm_cast",
    )(x, g.reshape(1, d))


def _norm_matmul_kernel(x_ref, g_ref, w_ref, o_ref, h_ref):
    @pl.when(pl.program_id(1) == 0)
    def _():
        h_ref[...] = _rms(x_ref[...], g_ref[...]).astype(BF16)

    o_ref[...] = jnp.dot(h_ref[...], w_ref[...],
                         preferred_element_type=F32).astype(o_ref.dtype)


def norm_matmul(x, g, w, tm=1024, tn=1024):
    t, d = x.shape
    n = w.shape[1]
    tm = min(tm, t)
    return pl.pallas_call(
        _norm_matmul_kernel,
        grid=(t // tm, n // tn),
        in_specs=[pl.BlockSpec((tm, d), lambda i, j: (i, 0)),
                  pl.BlockSpec((1, d), lambda i, j: (0, 0)),
                  pl.BlockSpec((d, tn), lambda i, j: (0, j))],
        out_specs=pl.BlockSpec((tm, tn), lambda i, j: (i, j)),
        out_shape=jax.ShapeDtypeStruct((t, n), BF16),
        scratch_shapes=[pltpu.VMEM((tm, d), BF16)],
        compiler_params=_cparams(("parallel", "arbitrary")),
        name="norm_matmul",
    )(x, g.reshape(1, d), w)


def _split2(x):
    hi = x.astype(BF16)
    return hi, (x - hi.astype(F32)).astype(BF16)


def _hgrn2_kernel(q_ref, f_ref, v_ref, g_ref, lb_ref, ng_ref, o_ref, st_ref):
    c = HG_CHUNK
    hd = HG_HEAD_DIM
    heads = st_ref.shape[0]
    n_chunks = q_ref.shape[0] // c
    ng = ng_ref[...]
    row = lax.broadcasted_iota(jnp.int32, (c, c), 0)
    col = lax.broadcasted_iota(jnp.int32, (c, c), 1)
    causal = row >= col
    tril = causal.astype(BF16)
    mid = c // 2

    @pl.when(pl.program_id(1) == 0)
    def _():
        st_ref[...] = jnp.zeros_like(st_ref)

    nt_dims = (((1,), (1,)), ((), ()))
    hs = range(heads)
    cols = [slice(h * hd, (h + 1) * hd) for h in hs]

    def step(n, carry):
        rows = pl.ds(pl.multiple_of(n * c, c), c)
        q = [q_ref[rows, cl].astype(F32) for cl in cols]
        f = [lb_ref[:, cl] + (1.0 - lb_ref[:, cl]) * jax.nn.sigmoid(f_ref[rows, cl].astype(F32))
             for cl in cols]
        k = [1.0 - fh for fh in f]
        parts = [_split2(jnp.log(fh)) for fh in f]
        b = [sum(jnp.dot(tril, p, preferred_element_type=F32) for p in ph) for ph in parts]
        ref = [bh[mid - 1:mid, :] for bh in b]
        b_last = [bh[c - 1:c, :] for bh in b]
        st = [st_ref[h] for h in hs]
        qs = [(q[h] * jnp.exp(b[h] - ref[h])).astype(BF16) for h in hs]
        ks = [(k[h] * jnp.exp(ref[h] - b[h])).astype(BF16) for h in hs]
        scores = [lax.dot_general(qs[h], ks[h], nt_dims, preferred_element_type=F32) for h in hs]
        qe = [(q[h] * jnp.exp(b[h])).astype(BF16) for h in hs]
        o_inter = [lax.dot_general(qe[h], st[h].astype(BF16), nt_dims, preferred_element_type=F32)
                   for h in hs]
        kd = [(k[h] * jnp.exp(b_last[h] - b[h])).astype(BF16) for h in hs]
        upd = [lax.dot_general(v_ref[rows, cols[h]], kd[h], (((0,), (0,)), ((), ())),
                               preferred_element_type=F32) for h in hs]
        for h in hs:
            st_ref[h] = st[h] * jnp.exp(b_last[h]) + upd[h]
        sc = [jnp.where(causal, s, 0.0).astype(BF16) for s in scores]
        o = [o_inter[h] + jnp.dot(sc[h], v_ref[rows, cols[h]], preferred_element_type=F32)
             for h in hs]
        for h in hs:
            oh = o[h] * lax.rsqrt(jnp.mean(o[h] * o[h], axis=-1, keepdims=True) + EPS) * ng
            gate = g_ref[rows, cols[h]].astype(F32)
            o_ref[rows, cols[h]] = (oh * (gate * jax.nn.sigmoid(gate))).astype(o_ref.dtype)
        return carry

    lax.fori_loop(0, n_chunks, step, 0, unroll=2)


def hgrn2(z, lb, norm_g, bsz, seq, width, col0, ts=1024):
    hd = HG_HEAD_DIM
    heads = width // hd
    cb = col0 // width
    ts = min(ts, seq)
    nt = seq // ts

    def zspec(k):
        return pl.BlockSpec((ts, width), lambda b, s: (b * nt + s, cb + k))

    return pl.pallas_call(
        _hgrn2_kernel,
        grid=(bsz, nt),
        in_specs=[zspec(0), zspec(1), zspec(2), zspec(3),
                  pl.BlockSpec((1, width), lambda b, s: (0, 0)),
                  pl.BlockSpec((1, hd), lambda b, s: (0, 0))],
        out_specs=pl.BlockSpec((ts, width), lambda b, s: (b * nt + s, 0)),
        out_shape=jax.ShapeDtypeStruct((bsz * seq, width), BF16),
        scratch_shapes=[pltpu.VMEM((heads, hd, hd), F32)],
        compiler_params=_cparams(("parallel", "arbitrary")),
        name="hgrn2",
    )(z, z, z, z, lb.reshape(1, width), norm_g.reshape(1, hd))


def _outproj_kernel(seq_tiles, ab_ref, ac_ref, ah_ref, hc_ref, hh_ref, cw_ref,
                    ob_ref, x_ref, wa_ref, wb_ref, o_ref):
    i = pl.program_id(0)
    u = ac_ref[...].astype(F32) * ah_ref[...].astype(F32)
    halo = hc_ref[...].astype(F32) * hh_ref[...].astype(F32)
    halo = jnp.where(i % seq_tiles == 0, 0.0, halo)
    hr = halo.shape[0]
    row = lax.broadcasted_iota(jnp.int32, u.shape, 0)
    u1 = jnp.where(row == 0, halo[hr - 1:hr, :], pltpu.roll(u, 1, axis=0))
    u2 = jnp.where(row == 0, halo[hr - 2:hr - 1, :],
                   jnp.where(row == 1, halo[hr - 1:hr, :], pltpu.roll(u, 2, axis=0)))
    cw = cw_ref[...]
    conv = cw[2:3, :] * u + cw[1:2, :] * u1 + cw[0:1, :] * u2
    ya = (ab_ref[...].astype(F32) * conv).astype(BF16)
    mix = jnp.dot(ya, wa_ref[...], preferred_element_type=F32)
    mix = mix + jnp.dot(ob_ref[...], wb_ref[...], preferred_element_type=F32)
    o_ref[...] = x_ref[...] + mix


def outproj_even(z, conv_w, o_b, x, w_out, seq, tm=512, halo=16):
    t, d = x.shape
    tm = min(tm, seq)
    cw = conv_w.shape[1]
    hw = o_b.shape[1]
    hb = tm // halo
    kern = functools.partial(_outproj_kernel, seq // tm)
    return pl.pallas_call(
        kern,
        grid=(t // tm,),
        in_specs=[pl.BlockSpec((tm, cw), lambda i: (i, 0)),
                  pl.BlockSpec((tm, cw), lambda i: (i, 1)),
                  pl.BlockSpec((tm, cw), lambda i: (i, 2)),
                  pl.BlockSpec((halo, cw), lambda i: (jnp.maximum(i * hb - 1, 0), 1)),
                  pl.BlockSpec((halo, cw), lambda i: (jnp.maximum(i * hb - 1, 0), 2)),
                  pl.BlockSpec((CONV_K, cw), lambda i: (0, 0)),
                  pl.BlockSpec((tm, hw), lambda i: (i, 0)),
                  pl.BlockSpec((tm, d), lambda i: (i, 0)),
                  pl.BlockSpec((cw, d), lambda i: (0, 0)),
                  pl.BlockSpec((hw, d), lambda i: (1, 0))],
        out_specs=pl.BlockSpec((tm, d), lambda i: (i, 0)),
        out_shape=jax.ShapeDtypeStruct((t, d), F32),
        compiler_params=_cparams(("parallel",)),
        name="outproj_even",
    )(z, z, z, z, z, conv_w, o_b, x, w_out, w_out)


def _rep_rows(x, n):
    r, l = x.shape
    return jnp.broadcast_to(x[:, None, :], (r, n, l)).reshape(r * n, l)


def _tile_rows(x, n):
    r, l = x.shape
    return jnp.broadcast_to(x[None, :, :], (n, r, l)).reshape(n * r, l)


def _s5_group(n_chunks, prm, u, m_ref):
    lc = S5_CHUNK
    gc = S5_GROUP
    half = S5_STATE
    nt_dims = (((1,), (1,)), ((), ()))
    bqa, ce1, ce2, bwa, bwb, cga, cgb = (prm[i * gc:(i + 1) * gc, :] for i in range(7))
    alpha = prm[8 * gc:8 * gc + 1, :]
    beta = prm[8 * gc + 1:8 * gc + 2, :]

    def cpow(tt):
        mag = jnp.exp(alpha * tt)
        ang = beta * tt
        return mag * jnp.cos(ang), mag * jnp.sin(ang)

    t_col = lax.broadcasted_iota(jnp.int32, (lc, LANES), 0).astype(F32)
    p0r, p0i = cpow(t_col)
    prr, pri = cpow(float(lc - 1) - t_col)
    a1r, a1i = cpow(jnp.full((1, LANES), 1.0, F32))
    p1r, p1i = p0r * a1r - p0i * a1i, p0r * a1i + p0i * a1r

    ecat = _rep_rows(p0r, gc) * _tile_rows(ce1, lc) + _rep_rows(p0i, gc) * _tile_rows(ce2, lc)
    b_hi, b_lo = _split2(bqa)
    e_hi, e_lo = _split2(ecat)
    nt_dot = lambda a, b: lax.dot_general(a, b, nt_dims, preferred_element_type=F32)
    k_row = nt_dot(b_hi, e_hi) + nt_dot(b_hi, e_lo) + nt_dot(b_lo, e_hi)
    spv = LANES // gc
    width = lc * gc
    pos = lax.broadcasted_iota(jnp.int32, k_row.shape, 1)
    for r in range(spv):
        base = k_row if r == 0 else jnp.where(pos >= r * gc, pltpu.roll(k_row, r * gc, axis=1), 0.0)
        base = base.astype(BF16)
        for q in range(lc // spv):
            s = q * spv + r
            blk = base if q == 0 else jnp.concatenate(
                [jnp.zeros((gc, q * LANES), BF16), base[:, :width - q * LANES]], axis=1)
            m_ref[s * gc:(s + 1) * gc, :] = blk

    wcat = (_rep_rows(prr, gc) * _tile_rows(bwa, lc)
            + _rep_rows(pri, gc) * _tile_rows(bwb, lc)).astype(BF16)
    st = jnp.dot(u, wcat, preferred_element_type=F32)

    rows = st.shape[0]
    rpos = lax.broadcasted_iota(jnp.int32, (rows, LANES), 0) % n_chunks
    rlane = lax.broadcasted_iota(jnp.int32, (rows, LANES), 1)
    ar, ai = cpow(jnp.full((1, LANES), float(lc), F32))
    d = 1
    while d < n_chunks:
        sh = jnp.where(rpos >= d, pltpu.roll(st, d, axis=0), 0.0)
        a2 = jnp.where(rlane < half, -ai, ai)
        st = st + ar * sh + a2 * pltpu.roll(sh, half, axis=1)
        ar, ai = ar * ar - ai * ai, 2.0 * ar * ai
        d *= 2
    h0 = jnp.where(rpos >= 1, pltpu.roll(st, 1, axis=0), 0.0).astype(BF16)

    gcat = (_rep_rows(p1r, gc) * _tile_rows(cga, lc)
            + _rep_rows(p1i, gc) * _tile_rows(cgb, lc)).astype(BF16)
    y = jnp.dot(u, m_ref[...], preferred_element_type=F32)
    y = y + lax.dot_general(h0, gcat, nt_dims, preferred_element_type=F32)
    d_row = prm[8 * gc + 2:8 * gc + 3, :]
    return y + jnp.concatenate([d_row] * (lc * gc // LANES), axis=1) * u.astype(F32)


def _s5_kernel(n_chunks, h_ref, p_ref, perm_ref, y_ref, u_ref, m_ref):
    lc = S5_CHUNK
    gps = u_ref.shape[0]
    rows = u_ref.shape[1]
    spv = LANES // S5_GROUP
    lane_blk = lambda k: slice(k * LANES, (k + 1) * LANES)

    for j in range(lc // spv):
        xcat = jnp.concatenate(
            [h_ref[pl.ds(j * spv + s, rows, stride=lc), :] for s in range(spv)], axis=1)
        uall = jnp.dot(xcat.astype(BF16), perm_ref[...], preferred_element_type=F32).astype(BF16)
        for gi in range(gps):
            u_ref[gi, :, lane_blk(j)] = uall[:, lane_blk(gi)]

    def group(gi, carry):
        y = _s5_group(n_chunks, p_ref[gi], u_ref[gi], m_ref)
        u_ref[gi] = y.astype(BF16)
        return carry
    lax.fori_loop(0, gps, group, 0)

    for j in range(lc // spv):
        ycat = jnp.concatenate([u_ref[gi, :, lane_blk(j)] for gi in range(gps)], axis=1)
        z = jnp.dot(ycat, perm_ref[...], preferred_element_type=F32)
        for s in range(spv):
            y_ref[pl.ds(j * spv + s, rows, stride=lc), :] = z[:, lane_blk(s)]


def _s5_tables(lam_re, lam_im, log_dt, b_re, b_im, c_re, c_im, d_skip):
    dt = jnp.exp(log_dt.astype(F32))[:, None]
    lr = lam_re.astype(F32)
    li = lam_im.astype(F32)
    mag = jnp.exp(lr * dt)
    ar = mag * jnp.cos(li * dt)
    ai = mag * jnp.sin(li * dt)
    den = lr * lr + li * li
    cr = ((ar - 1.0) * lr + ai * li) / den
    ci = (ai * lr - (ar - 1.0) * li) / den
    bbr = (cr[..., None] * b_re - ci[..., None] * b_im).transpose(0, 2, 1)
    bbi = (cr[..., None] * b_im + ci[..., None] * b_re).transpose(0, 2, 1)
    cre = c_re.astype(F32)
    cim = c_im.astype(F32)
    cat = lambda a, b: jnp.concatenate([a, b], axis=-1)
    alpha = (lr * dt)[:, None, :]
    beta = (li * dt)[:, None, :]
    groups, states = lr.shape
    gc = b_re.shape[2]
    d_row = jnp.tile(d_skip.astype(F32).reshape(groups, 1, gc), (1, 1, 2 * states // gc))
    pad = jnp.zeros((groups, 5, 2 * states), F32)
    return jnp.concatenate([
        cat(bbr, -bbi), cat(cre, cim), cat(-cim, cre),
        cat(bbr, bbi), cat(-bbi, bbr),
        cat(cre, -cim), cat(-cim, -cre), jnp.zeros_like(cat(cre, cre)),
        cat(alpha, alpha), cat(beta, beta), d_row, pad], axis=1)


def s5_mix(h, tables, bsz, seq):
    t, w = h.shape
    lc, gc = S5_CHUNK, S5_GROUP
    gps = LANES // gc
    slabs = w // LANES
    n = seq // lc
    rows = bsz * n
    k = lc * gc
    idx = jnp.arange(k, dtype=jnp.int32)
    tgt = (idx % LANES) // gc * LANES + idx // LANES * gc + idx % gc
    perm = (tgt[:, None] == idx[None, :]).astype(BF16)
    tables = tables.reshape((slabs, gps) + tables.shape[1:])
    return pl.pallas_call(
        functools.partial(_s5_kernel, n),
        grid=(slabs,),
        in_specs=[pl.BlockSpec((t, LANES), lambda i: (0, i)),
                  pl.BlockSpec((None, gps) + tables.shape[2:], lambda i: (i, 0, 0, 0)),
                  pl.BlockSpec((k, k), lambda i: (0, 0))],
        out_specs=pl.BlockSpec((t, LANES), lambda i: (0, i)),
        out_shape=jax.ShapeDtypeStruct((t, w), F32),
        scratch_shapes=[pltpu.VMEM((gps, rows, k), BF16),
                        pltpu.VMEM((k, k), BF16)],
        compiler_params=_cparams(("parallel",)),
        name="s5_mix",
    )(h, tables, perm)


def _glu_kernel(tn, y_ref, x_ref, w_ref, o_ref):
    y = y_ref[...]
    cdf = 0.5 * (1.0 + jnp.tanh(math.sqrt(2.0 / math.pi) * (y + 0.044715 * (y * y * y))))
    act = (y * cdf).astype(BF16)
    d = o_ref.shape[1]
    for c in range(0, d, tn):
        za = jnp.dot(act, w_ref[:, c:c + tn], preferred_element_type=F32)
        zb = jnp.dot(act, w_ref[:, d + c:d + c + tn], preferred_element_type=F32)
        o_ref[:, c:c + tn] = x_ref[:, c:c + tn] + za * jax.nn.sigmoid(zb)


def glu_residual(y, x, w_glu, tm=512, tn=512):
    t, d = x.shape
    w = y.shape[1]
    tm = min(tm, t)
    return pl.pallas_call(
        functools.partial(_glu_kernel, tn),
        grid=(t // tm,),
        in_specs=[pl.BlockSpec((tm, w), lambda i: (i, 0)),
                  pl.BlockSpec((tm, d), lambda i: (i, 0)),
                  pl.BlockSpec((w, 2 * d), lambda i: (0, 0), pipeline_mode=pl.Buffered(1))],
        out_specs=pl.BlockSpec((tm, d), lambda i: (i, 0)),
        out_shape=jax.ShapeDtypeStruct((t, d), F32),
        compiler_params=_cparams(("parallel",)),
        name="glu_residual",
    )(y, x, w_glu)


def _router_kernel(x_ref, g_ref, w_ref, b_ref, e_ref, p_ref, r_ref, c_ref, cnt_ref):
    tm = x_ref.shape[0]

    @pl.when(pl.program_id(0) == 0)
    def _():
        cnt_ref[...] = jnp.zeros_like(cnt_ref)

    h = _rms(x_ref[...], g_ref[...])
    nt_dot = lambda a, b: lax.dot_general(a, b, (((1,), (1,)), ((), ())),
                                          preferred_element_type=F32)
    w_hi, w_lo = _split2(w_ref[...])
    h_hi, h_lo = _split2(h)
    logits = (nt_dot(w_hi, h_hi) + nt_dot(w_hi, h_lo) + nt_dot(w_lo, h_hi)
              + b_ref[...])
    lc = [logits[i:i + 1, :] for i in range(MOE_GROUPS)]
    m = functools.reduce(jnp.maximum, lc)
    grp = jnp.full(m.shape, MOE_GROUPS - 1, jnp.int32)
    for i in range(MOE_GROUPS - 2, -1, -1):
        grp = jnp.where(lc[i] == m, i, grp)
    den = functools.reduce(lambda a, b: a + b, [jnp.exp(l - m) for l in lc])
    p_top = 1.0 / den
    fine = [logits[MOE_GROUPS + j:MOE_GROUPS + j + 1, :] for j in range(MOE_EXPERTS)]
    sel = []
    for e in range(MOE_EPG):
        v = fine[(MOE_GROUPS - 1) * MOE_EPG + e]
        for i in range(MOE_GROUPS - 2, -1, -1):
            v = jnp.where(grp == i, fine[i * MOE_EPG + e], v)
        sel.append(v)

    def top1(vals):
        best = functools.reduce(jnp.maximum, vals)
        idx = jnp.full(best.shape, MOE_EPG - 1, jnp.int32)
        for e in range(MOE_EPG - 2, -1, -1):
            idx = jnp.where(vals[e] == best, e, idx)
        return best, idx

    v1, i1 = top1(sel)
    v2, i2 = top1([jnp.where(i1 == e, -jnp.inf, sel[e]) for e in range(MOE_EPG)])
    ex = jnp.exp(v2 - v1)
    s = 1.0 + ex
    e0 = grp * MOE_EPG + i1
    e1 = grp * MOE_EPG + i2
    e_ref[0:1, :] = e0
    e_ref[1:2, :] = e1
    p_ref[0:1, :] = p_top * (1.0 / s)
    p_ref[1:2, :] = p_top * (ex / s)

    eid = lax.broadcasted_iota(jnp.int32, (MOE_EXPERTS, tm), 0)
    tri = (lax.broadcasted_iota(jnp.int32, (tm, tm), 0)
           <= lax.broadcasted_iota(jnp.int32, (tm, tm), 1)).astype(BF16)
    base = cnt_ref[...]
    for k, ek in enumerate((e0, e1)):
        hot = eid == ek
        hot_f = hot.astype(F32)
        csum = jnp.dot(hot.astype(BF16), tri, preferred_element_type=F32)
        rank = jnp.sum(hot_f * (base + csum - 1.0), axis=0, keepdims=True)
        r_ref[k:k + 1, :] = rank.astype(jnp.int32)
        base = base + jnp.sum(hot_f, axis=1, keepdims=True)
    cnt_ref[...] = base
    c_ref[...] = base.astype(jnp.int32)


def moe_router(x, g, w_coarse, b_coarse, w_fine, b_fine, tm=512):
    t, d = x.shape
    pad = ROUTER_ROWS - MOE_GROUPS - MOE_EXPERTS
    w = jnp.concatenate([w_coarse.T, w_fine.T, jnp.zeros((pad, d), F32)], axis=0)
    b = jnp.concatenate([b_coarse, b_fine, jnp.zeros((pad,), F32)]).reshape(ROUTER_ROWS, 1)
    tok_spec = pl.BlockSpec((2, tm), lambda i: (0, i))
    return pl.pallas_call(
        _router_kernel,
        grid=(t // tm,),
        in_specs=[pl.BlockSpec((tm, d), lambda i: (i, 0)),
                  pl.BlockSpec((1, d), lambda i: (0, 0)),
                  pl.BlockSpec((ROUTER_ROWS, d), lambda i: (0, 0)),
                  pl.BlockSpec((ROUTER_ROWS, 1), lambda i: (0, 0))],
        out_specs=[tok_spec, tok_spec, tok_spec,
                   pl.BlockSpec((MOE_EXPERTS, 1), lambda i: (0, 0))],
        out_shape=[jax.ShapeDtypeStruct((2, t), jnp.int32),
                   jax.ShapeDtypeStruct((2, t), F32),
                   jax.ShapeDtypeStruct((2, t), jnp.int32),
                   jax.ShapeDtypeStruct((MOE_EXPERTS, 1), jnp.int32)],
        scratch_shapes=[pltpu.VMEM((MOE_EXPERTS, 1), F32)],
        compiler_params=_cparams(("arbitrary",)),
        name="moe_router",
    )(x, g.reshape(1, d), w, b)


def _dispatch_plan(experts, rank, counts):
    n_assign = experts.size
    counts = counts.reshape(-1)
    padded = (counts + MOE_ROWS - 1) // MOE_ROWS * MOE_ROWS
    pend = jnp.cumsum(padded)
    pstart = pend - padded
    eids = jnp.arange(MOE_EXPERTS, dtype=jnp.int32)
    dest = rank + jnp.sum(jnp.where(experts[..., None] == eids, pstart, 0), axis=-1)
    n_groups = -(-n_assign // MOE_ROWS) + MOE_EXPERTS
    g0 = jnp.arange(n_groups, dtype=jnp.int32) * MOE_ROWS
    grp_expert = jnp.minimum(jnp.sum(pend[None, :] <= g0[:, None], axis=1), MOE_EXPERTS - 1)
    n_active = (pend[-1] // MOE_ROWS).reshape(1)
    of_group = lambda v: jnp.sum(jnp.where(grp_expert[:, None] == eids, v, 0), axis=1)
    grp_pos = (g0 - of_group(pstart)) // MOE_ROWS
    grp_len = of_group(padded) // MOE_ROWS
    used = counts > 0
    e_slot = (jnp.cumsum(used) - 1) % 2
    later_used = jnp.logical_and(used[None, :], eids[None, :] > eids[:, None])
    e_next = jnp.min(jnp.where(later_used, eids[None, :], MOE_EXPERTS), axis=1)
    e_next = jnp.where(e_next == MOE_EXPERTS, -1, e_next)
    i32 = lambda a: a.astype(jnp.int32)
    runs = (i32(grp_expert), i32(n_active), i32(grp_pos), i32(grp_len), i32(e_slot), i32(e_next))
    return i32(dest).reshape(-1), runs, i32(pstart + counts), i32(padded - counts)


def _dispatch_kernel(dest_ref, ps_ref, pn_ref, na_ref, x_ref, g_ref, xs_hbm, hbuf, zbuf, sem,
                     zsem):
    i = pl.program_id(0)
    n = pl.num_programs(0)
    tm = x_ref.shape[0]
    t = n * tm
    slot = i % 2

    def wait_slot(s):
        for _ in range(2):
            pltpu.make_async_copy(hbuf.at[s], xs_hbm.at[pl.ds(0, tm)], sem.at[s]).wait()

    def zero_row_copy(row):
        return pltpu.make_async_copy(zbuf.at[pl.ds(0, 1)], xs_hbm.at[pl.ds(row, 1)], zsem.at[0])

    def for_each_pad_row(fn):
        def per_expert(e, c):
            def per_row(r, c2):
                fn(ps_ref[e] + r)
                return c2
            return lax.fori_loop(0, pn_ref[e], per_row, c)
        lax.fori_loop(0, ps_ref.shape[0], per_expert, 0)

    def tail_group_copy(grp):
        rows = pl.ds(pl.multiple_of(grp * tm, tm), tm)
        return pltpu.make_async_copy(hbuf.at[1], xs_hbm.at[rows], sem.at[1])

    def for_each_tail_group(fn):
        def body(grp, c):
            fn(grp)
            return c
        lax.fori_loop(na_ref[0], xs_hbm.shape[0] // tm, body, 0)

    @pl.when(i == 0)
    def _():
        zbuf[...] = jnp.zeros_like(zbuf)
        for_each_pad_row(lambda row: zero_row_copy(row).start())
        hbuf[1] = jnp.zeros(hbuf.shape[1:], hbuf.dtype)
        for_each_tail_group(lambda grp: tail_group_copy(grp).start())
        for_each_tail_group(lambda grp: tail_group_copy(grp).wait())

    @pl.when(i >= 2)
    def _():
        wait_slot(slot)

    hbuf[slot] = _rms(x_ref[...], g_ref[...])

    def body(rb, c):
        r0 = pl.multiple_of(rb * ROW_UNROLL, ROW_UNROLL)
        for u in range(ROW_UNROLL):
            for k in range(2):
                row = dest_ref[k * t + i * tm + r0 + u]
                pltpu.make_async_copy(hbuf.at[slot, pl.ds(r0 + u, 1)],
                                      xs_hbm.at[pl.ds(row, 1)], sem.at[slot]).start()
        return c
    lax.fori_loop(0, tm // ROW_UNROLL, body, 0)

    @pl.when(i == n - 1)
    def _():
        wait_slot(slot)

        @pl.when(n >= 2)
        def _():
            wait_slot(1 - slot)

        for_each_pad_row(lambda row: zero_row_copy(row).wait())


def moe_dispatch(x, g, dest, pad_start, pad_len, n_active, total):
    t, d = x.shape
    tm = MOE_ROWS
    grid_spec = pltpu.PrefetchScalarGridSpec(
        num_scalar_prefetch=4,
        grid=(t // tm,),
        in_specs=[pl.BlockSpec((tm, d), lambda i, *_: (i, 0)),
                  pl.BlockSpec((1, d), lambda i, *_: (0, 0))],
        out_specs=pl.BlockSpec(memory_space=pl.ANY),
        scratch_shapes=[pltpu.VMEM((2, tm, d), F32),
                        pltpu.VMEM((8, d), F32),
                        pltpu.SemaphoreType.DMA((2,)),
                        pltpu.SemaphoreType.DMA((1,))],
    )
    return pl.pallas_call(
        _dispatch_kernel,
        grid_spec=grid_spec,
        out_shape=jax.ShapeDtypeStruct((total, d), F32),
        compiler_params=_cparams(("arbitrary",)),
        name="moe_dispatch",
    )(dest, pad_start, pad_len, n_active, x, g.reshape(1, d))


def _experts_kernel(layer, ge_ref, na_ref, pos_ref, len_ref, slot_ref, next_ref,
                    xs_ref, wg_hbm, wu_hbm, wd_hbm, o_ref, wg16, wu16, wd16, stage, sem):
    g = pl.program_id(0)
    e = ge_ref[g]
    half = stage.shape[1]
    w_hbm = (wg_hbm, wu_hbm, wd_hbm)
    w16 = (wg16, wu16, wd16)

    def half_src(p, expert, j):
        w = w_hbm[p].at[layer, expert]
        rows = pl.ds(j * half, half)
        return w.at[:, rows] if p == 2 else w.at[rows, :]

    def copies(p, expert):
        return [pltpu.make_async_copy(half_src(p, expert, j), stage.at[j], sem.at[j])
                for j in range(2)]

    def issue(p, expert):
        for c in copies(p, expert):
            c.start()

    def consume(p, expert, slot):
        for j, c in enumerate(copies(p, expert)):
            c.wait()
            for r in range(0, half, 256):
                v = stage[j, r:r + 256, :].astype(BF16)
                if p == 2:
                    w16[p][slot, r:r + 256, j * half:(j + 1) * half] = v
                else:
                    w16[p][slot, j * half + r:j * half + r + 256, :] = v

    @pl.when(g < na_ref[0])
    def _():
        slot = slot_ref[e]
        k = pos_ref[g]
        last = k == len_ref[g] - 1
        nxt = next_ref[e]
        has_next = nxt >= 0

        @pl.when(g == 0)
        def _():
            for p in range(3):
                issue(p, e)
                consume(p, e, slot)

        for p in range(3):
            @pl.when(jnp.logical_and(has_next, k == p + 1))
            def _():
                consume(p, nxt, 1 - slot)

        for p in range(3):
            @pl.when(jnp.logical_and(has_next, jnp.logical_or(k == p, jnp.logical_and(last, k < p))))
            def _():
                issue(p, nxt)

            @pl.when(jnp.logical_and(has_next, jnp.logical_and(last, k <= p)))
            def _():
                consume(p, nxt, 1 - slot)

        xb = xs_ref[...].astype(BF16)
        a = jnp.dot(xb, wg16[slot], preferred_element_type=F32)
        b = jnp.dot(xb, wu16[slot], preferred_element_type=F32)
        hid = (a * jax.nn.sigmoid(a) * b).astype(BF16)
        o_ref[...] = jnp.dot(hid, wd16[slot], preferred_element_type=F32)

    @pl.when(g >= na_ref[0])
    def _():
        o_ref[...] = jnp.zeros_like(o_ref)


def moe_experts(xs, runs, w_gate, w_up, w_down, layer):
    total, d = xs.shape
    ff = w_gate.shape[3]
    assert d == 2 * ff, "half of every weight matrix must have the staging shape (ff, ff)"
    n_groups = total // MOE_ROWS

    def blk(g, ge, na, *_):
        return (jnp.maximum(jnp.minimum(g, na[0] - 1), 0), 0)

    return pl.pallas_call(
        functools.partial(_experts_kernel, layer),
        grid_spec=pltpu.PrefetchScalarGridSpec(
            num_scalar_prefetch=len(runs), grid=(n_groups,),
            in_specs=[pl.BlockSpec((MOE_ROWS, d), blk)] + [pl.BlockSpec(memory_space=pl.ANY)] * 3,
            out_specs=pl.BlockSpec((MOE_ROWS, d), lambda g, *_: (g, 0)),
            scratch_shapes=[pltpu.VMEM((2, d, ff), BF16),
                            pltpu.VMEM((2, d, ff), BF16),
                            pltpu.VMEM((2, ff, d), BF16),
                            pltpu.VMEM((2, ff, ff), F32),
                            pltpu.SemaphoreType.DMA((2,))]),
        out_shape=jax.ShapeDtypeStruct((total, d), F32),
        compiler_params=_cparams(("arbitrary",)),
        name="moe_experts",
    )(*runs, xs, w_gate, w_up, w_down)


def _combine_kernel(emit_x, emit_norm, pos_ref, yb_hbm, x_ref, p_ref, g_ref, *rest):
    out_refs, (ybuf, sem) = rest[:-2], rest[-2:]
    i = pl.program_id(0)
    n = pl.num_programs(0)
    tm = x_ref.shape[0]
    t = n * tm

    def issue(tile, slot):
        def body(rb, c):
            r0 = pl.multiple_of(rb * ROW_UNROLL, ROW_UNROLL)
            for u in range(ROW_UNROLL):
                for k in range(2):
                    row = pos_ref[k * t + tile * tm + r0 + u]
                    pltpu.make_async_copy(yb_hbm.at[pl.ds(row, 1)],
                                          ybuf.at[slot, k, pl.ds(r0 + u, 1)],
                                          sem.at[slot]).start()
            return c
        lax.fori_loop(0, tm // ROW_UNROLL, body, 0)

    @pl.when(i == 0)
    def _():
        issue(0, 0)

    @pl.when(i + 1 < n)
    def _():
        issue(i + 1, (i + 1) % 2)

    slot = i % 2
    for k in range(2):
        pltpu.make_async_copy(yb_hbm.at[pl.ds(0, tm)], ybuf.at[slot, k], sem.at[slot]).wait()
    p = p_ref[...]
    out = x_ref[...] + (p[:, 0:1] * ybuf[slot, 0] + p[:, 1:2] * ybuf[slot, 1])
    outs = ([out] if emit_x else []) + ([_rms(out, g_ref[...])] if emit_norm else [])
    for o_ref, val in zip(out_refs, outs):
        o_ref[...] = val


def moe_combine(yb, dest, gates_t, x, norm_g=None, keep_x=True, tm=256):
    t, d = x.shape
    emit_norm = norm_g is not None
    emit_x = keep_x or not emit_norm
    g = (norm_g if emit_norm else jnp.ones((d,), F32)).reshape(1, d)
    n_out = int(emit_x) + int(emit_norm)
    row_spec = pl.BlockSpec((tm, d), lambda i, pos: (i, 0))
    grid_spec = pltpu.PrefetchScalarGridSpec(
        num_scalar_prefetch=1,
        grid=(t // tm,),
        in_specs=[pl.BlockSpec(memory_space=pl.ANY),
                  row_spec,
                  pl.BlockSpec((tm, 2), lambda i, pos: (i, 0)),
                  pl.BlockSpec((1, d), lambda i, pos: (0, 0))],
        out_specs=[row_spec] * n_out,
        scratch_shapes=[pltpu.VMEM((2, 2, tm, d), F32),
                        pltpu.SemaphoreType.DMA((2,))],
    )
    outs = pl.pallas_call(
        functools.partial(_combine_kernel, emit_x, emit_norm),
        grid_spec=grid_spec,
        out_shape=[jax.ShapeDtypeStruct((t, d), F32)] * n_out,
        compiler_params=_cparams(("arbitrary",)),
        name="moe_combine",
    )(dest, yb, x, gates_t, g)
    return outs if n_out > 1 else outs[0]


def hier_moe_residual(x, g_ffn, w_coarse, b_coarse, w_fine, b_fine, w_gate, w_up, w_down,
                      layer, norm_g=None, keep_x=True):
    experts, gates, rank, counts = moe_router(x, g_ffn, w_coarse, b_coarse, w_fine, b_fine)
    dest, runs, pad_start, pad_len = _dispatch_plan(experts, rank, counts)
    total = runs[0].shape[0] * MOE_ROWS
    xs = moe_dispatch(x, g_ffn, dest, pad_start, pad_len, runs[1], total)
    yb = moe_experts(xs, runs, w_gate, w_up, w_down, layer)
    return moe_combine(yb, dest, gates.T, x, norm_g, keep_x)


def kernel(x, g_mix, g_ffn, g_final, w_in_even, conv_w, hg_lower_bound, hg_norm_g, w_out_even, s5_lambda_re, s5_lambda_im, s5_log_dt, s5_b_re, s5_b_im, s5_c_re, s5_c_im, s5_d, w_glu, moe_w_coarse, moe_b_coarse, moe_w_fine, moe_b_fine, moe_w_gate, moe_w_up, moe_w_down):
    bsz, seq, d = x.shape
    depth = g_mix.shape[0]
    conv_width = conv_w.shape[2]
    hg_width = hg_lower_bound.shape[1]
    lb_all = jnp.cumsum(jax.nn.softmax(hg_lower_bound.astype(F32), axis=0), axis=0)
    xf = x.reshape(bsz * seq, d)
    h = None
    for layer in range(depth):
        j = layer // 2
        last = layer == depth - 1
        if layer % 2 == 0:
            z = norm_matmul(xf, g_mix[layer], w_in_even[j].astype(BF16))
            o_b = hgrn2(z, lb_all[j], hg_norm_g[j], bsz, seq, hg_width, 3 * conv_width)
            xf = outproj_even(z, conv_w[j], o_b, xf, w_out_even[j].astype(BF16), seq)
        else:
            if h is None:
                h = rmsnorm_cast(xf, g_mix[layer], F32)
            tables = _s5_tables(s5_lambda_re[j], s5_lambda_im[j], s5_log_dt[j],
                                s5_b_re[j], s5_b_im[j], s5_c_re[j], s5_c_im[j], s5_d[j])
            y = s5_mix(h, tables, bsz, seq)
            xf = glu_residual(y, xf, w_glu[j].astype(BF16))
        next_s5 = not last and (layer + 1) % 2 == 1
        norm_g = g_final if last else (g_mix[layer + 1] if next_s5 else None)
        out = hier_moe_residual(
            xf, g_ffn[layer], moe_w_coarse[layer], moe_b_coarse[layer], moe_w_fine[layer],
            moe_b_fine[layer], moe_w_gate, moe_w_up, moe_w_down, layer, norm_g,
            keep_x=not last)
        xf, h = out if next_s5 else (out, None)
    return xf.reshape(bsz, seq, d)
```

```python
import functools
import math

import jax
import jax.numpy as jnp
from jax import lax
from jax.experimental import pallas as pl
from jax.experimental.pallas import tpu as pltpu

F32 = jnp.float32
BF16 = jnp.bfloat16
EPS = 1e-6

LANES = 128
HG_HEAD_DIM = 128
HG_CHUNK = 64
CONV_K = 3
S5_GROUP = 16
S5_STATE = 64
S5_CHUNK = 64
MOE_GROUPS = 4
MOE_EPG = 8
MOE_EXPERTS = MOE_GROUPS * MOE_EPG
MOE_ROWS = 256
ROUTER_ROWS = 40
ROW_UNROLL = 8
VMEM_LIMIT = 56 * 1024 * 1024


def _cparams(sem, vmem=VMEM_LIMIT):
    return pltpu.CompilerParams(dimension_semantics=sem, vmem_limit_bytes=vmem)


def _rms(x, g):
    ms = jnp.mean(x * x, axis=-1, keepdims=True)
    return x * lax.rsqrt(ms + EPS) * g


def _norm_kernel(x_ref, g_ref, o_ref):
    o_ref[...] = _rms(x_ref[...], g_ref[...]).astype(o_ref.dtype)


def rmsnorm_cast(x, g, dtype, tm=512):
    t, d = x.shape
    return pl.pallas_call(
        _norm_kernel,
        grid=(t // tm,),
        in_specs=[pl.BlockSpec((tm, d), lambda i: (i, 0)),
                  pl.BlockSpec((1, d), lambda i: (0, 0))],
        out_specs=pl.BlockSpec((tm, d), lambda i: (i, 0)),
        out_shape=jax.ShapeDtypeStruct((t, d), dtype),
        compiler_params=_cparams(("parallel",)),
        name="rmsnorm_cast",
    )(x, g.reshape(1, d))


def _norm_matmul_kernel(x_ref, g_ref, w_ref, o_ref, h_ref):
    @pl.when(pl.program_id(1) == 0)
    def _():
        h_ref[...] = _rms(x_ref[...], g_ref[...]).astype(BF16)

    o_ref[...] = jnp.dot(h_ref[...], w_ref[...],
                         preferred_element_type=F32).astype(o_ref.dtype)


def norm_matmul(x, g, w, tm=1024, tn=1024):
    t, d = x.shape
    n = w.shape[1]
    tm = min(tm, t)
    return pl.pallas_call(
        _norm_matmul_kernel,
        grid=(t // tm, n // tn),
        in_specs=[pl.BlockSpec((tm, d), lambda i, j: (i, 0)),
                  pl.BlockSpec((1, d), lambda i, j: (0, 0)),
                  pl.BlockSpec((d, tn), lambda i, j: (0, j))],
        out_specs=pl.BlockSpec((tm, tn), lambda i, j: (i, j)),
        out_shape=jax.ShapeDtypeStruct((t, n), BF16),
        scratch_shapes=[pltpu.VMEM((tm, d), BF16)],
        compiler_params=_cparams(("parallel", "arbitrary")),
        name="norm_matmul",
    )(x, g.reshape(1, d), w)


def _split2(x):
    hi = x.astype(BF16)
    return hi, (x - hi.astype(F32)).astype(BF16)


def _hgrn2_kernel(q_ref, f_ref, v_ref, g_ref, lb_ref, ng_ref, o_ref, st_ref):
    c = HG_CHUNK
    hd = HG_HEAD_DIM
    heads = st_ref.shape[0]
    n_chunks = q_ref.shape[0] // c
    ng = ng_ref[...]
    row = lax.broadcasted_iota(jnp.int32, (c, c), 0)
    col = lax.broadcasted_iota(jnp.int32, (c, c), 1)
    causal = row >= col
    tril = causal.astype(BF16)
    mid = c // 2

    @pl.when(pl.program_id(1) == 0)
    def _():
        st_ref[...] = jnp.zeros_like(st_ref)

    nt_dims = (((1,), (1,)), ((), ()))
    hs = range(heads)
    cols = [slice(h * hd, (h + 1) * hd) for h in hs]

    def step(n, carry):
        rows = pl.ds(pl.multiple_of(n * c, c), c)
        q = [q_ref[rows, cl].astype(F32) for cl in cols]
        f = [lb_ref[:, cl] + (1.0 - lb_ref[:, cl]) * jax.nn.sigmoid(f_ref[rows, cl].astype(F32))
             for cl in cols]
        k = [1.0 - fh for fh in f]
        parts = [_split2(jnp.log(fh)) for fh in f]
        b = [sum(jnp.dot(tril, p, preferred_element_type=F32) for p in ph) for ph in parts]
        ref = [bh[mid - 1:mid, :] for bh in b]
        b_last = [bh[c - 1:c, :] for bh in b]
        st = [st_ref[h] for h in hs]
        qs = [(q[h] * jnp.exp(b[h] - ref[h])).astype(BF16) for h in hs]
        ks = [(k[h] * jnp.exp(ref[h] - b[h])).astype(BF16) for h in hs]
        scores = [lax.dot_general(qs[h], ks[h], nt_dims, preferred_element_type=F32) for h in hs]
        qe = [(q[h] * jnp.exp(b[h])).astype(BF16) for h in hs]
        o_inter = [lax.dot_general(qe[h], st[h].astype(BF16), nt_dims, preferred_element_type=F32)
                   for h in hs]
        kd = [(k[h] * jnp.exp(b_last[h] - b[h])).astype(BF16) for h in hs]
        upd = [lax.dot_general(v_ref[rows, cols[h]], kd[h], (((0,), (0,)), ((), ())),
                               preferred_element_type=F32) for h in hs]
        for h in hs:
            st_ref[h] = st[h] * jnp.exp(b_last[h]) + upd[h]
        sc = [jnp.where(causal, s, 0.0).astype(BF16) for s in scores]
        o = [o_inter[h] + jnp.dot(sc[h], v_ref[rows, cols[h]], preferred_element_type=F32)
             for h in hs]
        for h in hs:
            oh = o[h] * lax.rsqrt(jnp.mean(o[h] * o[h], axis=-1, keepdims=True) + EPS) * ng
            gate = g_ref[rows, cols[h]].astype(F32)
            o_ref[rows, cols[h]] = (oh * (gate * jax.nn.sigmoid(gate))).astype(o_ref.dtype)
        return carry

    lax.fori_loop(0, n_chunks, step, 0, unroll=4)


def hgrn2(z, lb, norm_g, bsz, seq, width, col0, ts=1024):
    hd = HG_HEAD_DIM
    heads = width // hd
    cb = col0 // width
    ts = min(ts, seq)
    nt = seq // ts

    def zspec(k):
        return pl.BlockSpec((ts, width), lambda b, s: (b * nt + s, cb + k))

    return pl.pallas_call(
        _hgrn2_kernel,
        grid=(bsz, nt),
        in_specs=[zspec(0), zspec(1), zspec(2), zspec(3),
                  pl.BlockSpec((1, width), lambda b, s: (0, 0)),
                  pl.BlockSpec((1, hd), lambda b, s: (0, 0))],
        out_specs=pl.BlockSpec((ts, width), lambda b, s: (b * nt + s, 0)),
        out_shape=jax.ShapeDtypeStruct((bsz * seq, width), BF16),
        scratch_shapes=[pltpu.VMEM((heads, hd, hd), F32)],
        compiler_params=_cparams(("parallel", "arbitrary")),
        name="hgrn2",
    )(z, z, z, z, lb.reshape(1, width), norm_g.reshape(1, hd))


def _outproj_kernel(seq_tiles, ab_ref, ac_ref, ah_ref, hc_ref, hh_ref, cw_ref,
                    ob_ref, x_ref, wa_ref, wb_ref, o_ref):
    i = pl.program_id(0)
    u = ac_ref[...].astype(F32) * ah_ref[...].astype(F32)
    halo = hc_ref[...].astype(F32) * hh_ref[...].astype(F32)
    halo = jnp.where(i % seq_tiles == 0, 0.0, halo)
    hr = halo.shape[0]
    row = lax.broadcasted_iota(jnp.int32, u.shape, 0)
    u1 = jnp.where(row == 0, halo[hr - 1:hr, :], pltpu.roll(u, 1, axis=0))
    u2 = jnp.where(row == 0, halo[hr - 2:hr - 1, :],
                   jnp.where(row == 1, halo[hr - 1:hr, :], pltpu.roll(u, 2, axis=0)))
    cw = cw_ref[...]
    conv = cw[2:3, :] * u + cw[1:2, :] * u1 + cw[0:1, :] * u2
    ya = (ab_ref[...].astype(F32) * conv).astype(BF16)
    mix = jnp.dot(ya, wa_ref[...], preferred_element_type=F32)
    mix = mix + jnp.dot(ob_ref[...], wb_ref[...], preferred_element_type=F32)
    o_ref[...] = x_ref[...] + mix


def outproj_even(z, conv_w, o_b, x, w_out, seq, tm=512, halo=16):
    t, d = x.shape
    tm = min(tm, seq)
    cw = conv_w.shape[1]
    hw = o_b.shape[1]
    hb = tm // halo
    kern = functools.partial(_outproj_kernel, seq // tm)
    return pl.pallas_call(
        kern,
        grid=(t // tm,),
        in_specs=[pl.BlockSpec((tm, cw), lambda i: (i, 0)),
                  pl.BlockSpec((tm, cw), lambda i: (i, 1)),
                  pl.BlockSpec((tm, cw), lambda i: (i, 2)),
                  pl.BlockSpec((halo, cw), lambda i: (jnp.maximum(i * hb - 1, 0), 1)),
                  pl.BlockSpec((halo, cw), lambda i: (jnp.maximum(i * hb - 1, 0), 2)),
                  pl.BlockSpec((CONV_K, cw), lambda i: (0, 0)),
                  pl.BlockSpec((tm, hw), lambda i: (i, 0)),
                  pl.BlockSpec((tm, d), lambda i: (i, 0)),
                  pl.BlockSpec((cw, d), lambda i: (0, 0)),
                  pl.BlockSpec((hw, d), lambda i: (1, 0))],
        out_specs=pl.BlockSpec((tm, d), lambda i: (i, 0)),
        out_shape=jax.ShapeDtypeStruct((t, d), F32),
        compiler_params=_cparams(("parallel",)),
        name="outproj_even",
    )(z, z, z, z, z, conv_w, o_b, x, w_out, w_out)


def _rep_rows(x, n):
    r, l = x.shape
    return jnp.broadcast_to(x[:, None, :], (r, n, l)).reshape(r * n, l)


def _tile_rows(x, n):
    r, l = x.shape
    return jnp.broadcast_to(x[None, :, :], (n, r, l)).reshape(n * r, l)


def _s5_group(n_chunks, prm, u, m_ref):
    lc = S5_CHUNK
    gc = S5_GROUP
    half = S5_STATE
    nt_dims = (((1,), (1,)), ((), ()))
    bqa, ce1, ce2, bwa, bwb, cga, cgb = (prm[i * gc:(i + 1) * gc, :] for i in range(7))
    alpha = prm[8 * gc:8 * gc + 1, :]
    beta = prm[8 * gc + 1:8 * gc + 2, :]

    def cpow(tt):
        mag = jnp.exp(alpha * tt)
        ang = beta * tt
        return mag * jnp.cos(ang), mag * jnp.sin(ang)

    t_col = lax.broadcasted_iota(jnp.int32, (lc, LANES), 0).astype(F32)
    p0r, p0i = cpow(t_col)
    prr, pri = cpow(float(lc - 1) - t_col)
    a1r, a1i = cpow(jnp.full((1, LANES), 1.0, F32))
    p1r, p1i = p0r * a1r - p0i * a1i, p0r * a1i + p0i * a1r

    ecat = _rep_rows(p0r, gc) * _tile_rows(ce1, lc) + _rep_rows(p0i, gc) * _tile_rows(ce2, lc)
    b_hi, b_lo = _split2(bqa)
    e_hi, e_lo = _split2(ecat)
    nt_dot = lambda a, b: lax.dot_general(a, b, nt_dims, preferred_element_type=F32)
    k_row = nt_dot(b_hi, e_hi) + nt_dot(b_hi, e_lo) + nt_dot(b_lo, e_hi)
    spv = LANES // gc
    width = lc * gc
    pos = lax.broadcasted_iota(jnp.int32, k_row.shape, 1)
    for r in range(spv):
        base = k_row if r == 0 else jnp.where(pos >= r * gc, pltpu.roll(k_row, r * gc, axis=1), 0.0)
        base = base.astype(BF16)
        for q in range(lc // spv):
            s = q * spv + r
            blk = base if q == 0 else jnp.concatenate(
                [jnp.zeros((gc, q * LANES), BF16), base[:, :width - q * LANES]], axis=1)
            m_ref[s * gc:(s + 1) * gc, :] = blk

    wcat = (_rep_rows(prr, gc) * _tile_rows(bwa, lc)
            + _rep_rows(pri, gc) * _tile_rows(bwb, lc)).astype(BF16)
    st = jnp.dot(u, wcat, preferred_element_type=F32)

    rows = st.shape[0]
    rpos = lax.broadcasted_iota(jnp.int32, (rows, LANES), 0) % n_chunks
    rlane = lax.broadcasted_iota(jnp.int32, (rows, LANES), 1)
    ar, ai = cpow(jnp.full((1, LANES), float(lc), F32))
    d = 1
    while d < n_chunks:
        sh = jnp.where(rpos >= d, pltpu.roll(st, d, axis=0), 0.0)
        a2 = jnp.where(rlane < half, -ai, ai)
        st = st + ar * sh + a2 * pltpu.roll(sh, half, axis=1)
        ar, ai = ar * ar - ai * ai, 2.0 * ar * ai
        d *= 2
    h0 = jnp.where(rpos >= 1, pltpu.roll(st, 1, axis=0), 0.0).astype(BF16)

    gcat = (_rep_rows(p1r, gc) * _tile_rows(cga, lc)
            + _rep_rows(p1i, gc) * _tile_rows(cgb, lc)).astype(BF16)
    y = jnp.dot(u, m_ref[...], preferred_element_type=F32)
    y = y + lax.dot_general(h0, gcat, nt_dims, preferred_element_type=F32)
    d_row = prm[8 * gc + 2:8 * gc + 3, :]
    return y + jnp.concatenate([d_row] * (lc * gc // LANES), axis=1) * u.astype(F32)


def _s5_kernel(n_chunks, h_ref, p_ref, perm_ref, y_ref, u_ref, m_ref):
    lc = S5_CHUNK
    gps = u_ref.shape[0]
    rows = u_ref.shape[1]
    spv = LANES // S5_GROUP
    lane_blk = lambda k: slice(k * LANES, (k + 1) * LANES)

    for j in range(lc // spv):
        xcat = jnp.concatenate(
            [h_ref[pl.ds(j * spv + s, rows, stride=lc), :] for s in range(spv)], axis=1)
        uall = jnp.dot(xcat.astype(BF16), perm_ref[...], preferred_element_type=F32).astype(BF16)
        for gi in range(gps):
            u_ref[gi, :, lane_blk(j)] = uall[:, lane_blk(gi)]

    n_m = m_ref.shape[0]

    def groups(i, carry):
        for b in range(n_m):
            gi = i * n_m + b
            y = _s5_group(n_chunks, p_ref[gi], u_ref[gi], m_ref.at[b])
            u_ref[gi] = y.astype(BF16)
        return carry
    lax.fori_loop(0, gps // n_m, groups, 0)

    for j in range(lc // spv):
        ycat = jnp.concatenate([u_ref[gi, :, lane_blk(j)] for gi in range(gps)], axis=1)
        z = jnp.dot(ycat, perm_ref[...], preferred_element_type=F32)
        for s in range(spv):
            y_ref[pl.ds(j * spv + s, rows, stride=lc), :] = z[:, lane_blk(s)]


def _s5_tables(lam_re, lam_im, log_dt, b_re, b_im, c_re, c_im, d_skip):
    dt = jnp.exp(log_dt.astype(F32))[:, None]
    lr = lam_re.astype(F32)
    li = lam_im.astype(F32)
    mag = jnp.exp(lr * dt)
    ar = mag * jnp.cos(li * dt)
    ai = mag * jnp.sin(li * dt)
    den = lr * lr + li * li
    cr = ((ar - 1.0) * lr + ai * li) / den
    ci = (ai * lr - (ar - 1.0) * li) / den
    bbr = (cr[..., None] * b_re - ci[..., None] * b_im).transpose(0, 2, 1)
    bbi = (cr[..., None] * b_im + ci[..., None] * b_re).transpose(0, 2, 1)
    cre = c_re.astype(F32)
    cim = c_im.astype(F32)
    cat = lambda a, b: jnp.concatenate([a, b], axis=-1)
    alpha = (lr * dt)[:, None, :]
    beta = (li * dt)[:, None, :]
    groups, states = lr.shape
    gc = b_re.shape[2]
    d_row = jnp.tile(d_skip.astype(F32).reshape(groups, 1, gc), (1, 1, 2 * states // gc))
    pad = jnp.zeros((groups, 5, 2 * states), F32)
    return jnp.concatenate([
        cat(bbr, -bbi), cat(cre, cim), cat(-cim, cre),
        cat(bbr, bbi), cat(-bbi, bbr),
        cat(cre, -cim), cat(-cim, -cre), jnp.zeros_like(cat(cre, cre)),
        cat(alpha, alpha), cat(beta, beta), d_row, pad], axis=1)


def s5_mix(h, tables, bsz, seq):
    t, w = h.shape
    lc, gc = S5_CHUNK, S5_GROUP
    gps = LANES // gc
    slabs = w // LANES
    n = seq // lc
    rows = bsz * n
    k = lc * gc
    idx = jnp.arange(k, dtype=jnp.int32)
    tgt = (idx % LANES) // gc * LANES + idx // LANES * gc + idx % gc
    perm = (tgt[:, None] == idx[None, :]).astype(BF16)
    tables = tables.reshape((slabs, gps) + tables.shape[1:])
    return pl.pallas_call(
        functools.partial(_s5_kernel, n),
        grid=(slabs,),
        in_specs=[pl.BlockSpec((t, LANES), lambda i: (0, i)),
                  pl.BlockSpec((None, gps) + tables.shape[2:], lambda i: (i, 0, 0, 0)),
                  pl.BlockSpec((k, k), lambda i: (0, 0))],
        out_specs=pl.BlockSpec((t, LANES), lambda i: (0, i)),
        out_shape=jax.ShapeDtypeStruct((t, w), F32),
        scratch_shapes=[pltpu.VMEM((gps, rows, k), BF16),
                        pltpu.VMEM((2, k, k), BF16)],
        compiler_params=_cparams(("parallel",)),
        name="s5_mix",
    )(h, tables, perm)


def _glu_kernel(tn, y_ref, x_ref, w_ref, o_ref):
    y = y_ref[...]
    cdf = 0.5 * (1.0 + jnp.tanh(math.sqrt(2.0 / math.pi) * (y + 0.044715 * (y * y * y))))
    act = (y * cdf).astype(BF16)
    d = o_ref.shape[1]
    for c in range(0, d, tn):
        za = jnp.dot(act, w_ref[:, c:c + tn], preferred_element_type=F32)
        zb = jnp.dot(act, w_ref[:, d + c:d + c + tn], preferred_element_type=F32)
        o_ref[:, c:c + tn] = x_ref[:, c:c + tn] + za * jax.nn.sigmoid(zb)


def glu_residual(y, x, w_glu, tm=512, tn=512):
    t, d = x.shape
    w = y.shape[1]
    tm = min(tm, t)
    return pl.pallas_call(
        functools.partial(_glu_kernel, tn),
        grid=(t // tm,),
        in_specs=[pl.BlockSpec((tm, w), lambda i: (i, 0)),
                  pl.BlockSpec((tm, d), lambda i: (i, 0)),
                  pl.BlockSpec((w, 2 * d), lambda i: (0, 0), pipeline_mode=pl.Buffered(1))],
        out_specs=pl.BlockSpec((tm, d), lambda i: (i, 0)),
        out_shape=jax.ShapeDtypeStruct((t, d), F32),
        compiler_params=_cparams(("parallel",)),
        name="glu_residual",
    )(y, x, w_glu)


def _router_kernel(x_ref, g_ref, w_ref, b_ref, e_ref, p_ref, r_ref, c_ref, cnt_ref):
    tm = x_ref.shape[0]

    @pl.when(pl.program_id(0) == 0)
    def _():
        cnt_ref[...] = jnp.zeros_like(cnt_ref)

    h = _rms(x_ref[...], g_ref[...])
    nt_dot = lambda a, b: lax.dot_general(a, b, (((1,), (1,)), ((), ())),
                                          preferred_element_type=F32)
    w_hi, w_lo = _split2(w_ref[...])
    h_hi, h_lo = _split2(h)
    logits = (nt_dot(w_hi, h_hi) + nt_dot(w_hi, h_lo) + nt_dot(w_lo, h_hi)
              + b_ref[...])
    lc = [logits[i:i + 1, :] for i in range(MOE_GROUPS)]
    m = functools.reduce(jnp.maximum, lc)
    grp = jnp.full(m.shape, MOE_GROUPS - 1, jnp.int32)
    for i in range(MOE_GROUPS - 2, -1, -1):
        grp = jnp.where(lc[i] == m, i, grp)
    den = functools.reduce(lambda a, b: a + b, [jnp.exp(l - m) for l in lc])
    p_top = 1.0 / den
    fine = [logits[MOE_GROUPS + j:MOE_GROUPS + j + 1, :] for j in range(MOE_EXPERTS)]
    sel = []
    for e in range(MOE_EPG):
        v = fine[(MOE_GROUPS - 1) * MOE_EPG + e]
        for i in range(MOE_GROUPS - 2, -1, -1):
            v = jnp.where(grp == i, fine[i * MOE_EPG + e], v)
        sel.append(v)

    def top1(vals):
        best = functools.reduce(jnp.maximum, vals)
        idx = jnp.full(best.shape, MOE_EPG - 1, jnp.int32)
        for e in range(MOE_EPG - 2, -1, -1):
            idx = jnp.where(vals[e] == best, e, idx)
        return best, idx

    v1, i1 = top1(sel)
    v2, i2 = top1([jnp.where(i1 == e, -jnp.inf, sel[e]) for e in range(MOE_EPG)])
    ex = jnp.exp(v2 - v1)
    s = 1.0 + ex
    e0 = grp * MOE_EPG + i1
    e1 = grp * MOE_EPG + i2
    e_ref[0:1, :] = e0
    e_ref[1:2, :] = e1
    p_ref[0:1, :] = p_top * (1.0 / s)
    p_ref[1:2, :] = p_top * (ex / s)

    eid = lax.broadcasted_iota(jnp.int32, (MOE_EXPERTS, tm), 0)
    tri = (lax.broadcasted_iota(jnp.int32, (tm, tm), 0)
           <= lax.broadcasted_iota(jnp.int32, (tm, tm), 1)).astype(BF16)
    base = cnt_ref[...]
    for k, ek in enumerate((e0, e1)):
        hot = eid == ek
        hot_f = hot.astype(F32)
        csum = jnp.dot(hot.astype(BF16), tri, preferred_element_type=F32)
        rank = jnp.sum(hot_f * (base + csum - 1.0), axis=0, keepdims=True)
        r_ref[k:k + 1, :] = rank.astype(jnp.int32)
        base = base + jnp.sum(hot_f, axis=1, keepdims=True)
    cnt_ref[...] = base
    c_ref[...] = base.astype(jnp.int32)


def moe_router(x, g, w_coarse, b_coarse, w_fine, b_fine, tm=512):
    t, d = x.shape
    pad = ROUTER_ROWS - MOE_GROUPS - MOE_EXPERTS
    w = jnp.concatenate([w_coarse.T, w_fine.T, jnp.zeros((pad, d), F32)], axis=0)
    b = jnp.concatenate([b_coarse, b_fine, jnp.zeros((pad,), F32)]).reshape(ROUTER_ROWS, 1)
    tok_spec = pl.BlockSpec((2, tm), lambda i: (0, i))
    return pl.pallas_call(
        _router_kernel,
        grid=(t // tm,),
        in_specs=[pl.BlockSpec((tm, d), lambda i: (i, 0)),
                  pl.BlockSpec((1, d), lambda i: (0, 0)),
                  pl.BlockSpec((ROUTER_ROWS, d), lambda i: (0, 0)),
                  pl.BlockSpec((ROUTER_ROWS, 1), lambda i: (0, 0))],
        out_specs=[tok_spec, tok_spec, tok_spec,
                   pl.BlockSpec((MOE_EXPERTS, 1), lambda i: (0, 0))],
        out_shape=[jax.ShapeDtypeStruct((2, t), jnp.int32),
                   jax.ShapeDtypeStruct((2, t), F32),
                   jax.ShapeDtypeStruct((2, t), jnp.int32),
                   jax.ShapeDtypeStruct((MOE_EXPERTS, 1), jnp.int32)],
        scratch_shapes=[pltpu.VMEM((MOE_EXPERTS, 1), F32)],
        compiler_params=_cparams(("arbitrary",)),
        name="moe_router",
    )(x, g.reshape(1, d), w, b)


def _dispatch_plan(experts, rank, counts):
    n_assign = experts.size
    counts = counts.reshape(-1)
    padded = (counts + MOE_ROWS - 1) // MOE_ROWS * MOE_ROWS
    pend = jnp.cumsum(padded)
    pstart = pend - padded
    eids = jnp.arange(MOE_EXPERTS, dtype=jnp.int32)
    dest = rank + jnp.sum(jnp.where(experts[..., None] == eids, pstart, 0), axis=-1)
    n_groups = -(-n_assign // MOE_ROWS) + MOE_EXPERTS
    g0 = jnp.arange(n_groups, dtype=jnp.int32) * MOE_ROWS
    grp_expert = jnp.minimum(jnp.sum(pend[None, :] <= g0[:, None], axis=1), MOE_EXPERTS - 1)
    n_active = (pend[-1] // MOE_ROWS).reshape(1)
    of_group = lambda v: jnp.sum(jnp.where(grp_expert[:, None] == eids, v, 0), axis=1)
    grp_pos = (g0 - of_group(pstart)) // MOE_ROWS
    grp_len = of_group(padded) // MOE_ROWS
    used = counts > 0
    e_slot = (jnp.cumsum(used) - 1) % 2
    later_used = jnp.logical_and(used[None, :], eids[None, :] > eids[:, None])
    e_next = jnp.min(jnp.where(later_used, eids[None, :], MOE_EXPERTS), axis=1)
    e_next = jnp.where(e_next == MOE_EXPERTS, -1, e_next)
    i32 = lambda a: a.astype(jnp.int32)
    runs = (i32(grp_expert), i32(n_active), i32(grp_pos), i32(grp_len), i32(e_slot), i32(e_next))
    return i32(dest).reshape(-1), runs, i32(pstart + counts), i32(padded - counts)


def _dispatch_kernel(dest_ref, ps_ref, pn_ref, na_ref, x_ref, g_ref, xs_hbm, hbuf, zbuf, sem,
                     zsem):
    i = pl.program_id(0)
    n = pl.num_programs(0)
    tm = x_ref.shape[0]
    t = n * tm
    slot = i % 2

    def wait_slot(s):
        for _ in range(2):
            pltpu.make_async_copy(hbuf.at[s], xs_hbm.at[pl.ds(0, tm)], sem.at[s]).wait()

    def zero_row_copy(row):
        return pltpu.make_async_copy(zbuf.at[pl.ds(0, 1)], xs_hbm.at[pl.ds(row, 1)], zsem.at[0])

    def for_each_pad_row(fn):
        def per_expert(e, c):
            def per_row(r, c2):
                fn(ps_ref[e] + r)
                return c2
            return lax.fori_loop(0, pn_ref[e], per_row, c)
        lax.fori_loop(0, ps_ref.shape[0], per_expert, 0)

    def tail_group_copy(grp):
        rows = pl.ds(pl.multiple_of(grp * tm, tm), tm)
        return pltpu.make_async_copy(hbuf.at[1], xs_hbm.at[rows], sem.at[1])

    def for_each_tail_group(fn):
        def body(grp, c):
            fn(grp)
            return c
        lax.fori_loop(na_ref[0], xs_hbm.shape[0] // tm, body, 0)

    @pl.when(i == 0)
    def _():
        zbuf[...] = jnp.zeros_like(zbuf)
        for_each_pad_row(lambda row: zero_row_copy(row).start())
        hbuf[1] = jnp.zeros(hbuf.shape[1:], hbuf.dtype)
        for_each_tail_group(lambda grp: tail_group_copy(grp).start())
        for_each_tail_group(lambda grp: tail_group_copy(grp).wait())

    @pl.when(i >= 2)
    def _():
        wait_slot(slot)

    hbuf[slot] = _rms(x_ref[...], g_ref[...])

    def body(rb, c):
        r0 = pl.multiple_of(rb * ROW_UNROLL, ROW_UNROLL)
        for u in range(ROW_UNROLL):
            for k in range(2):
                row = dest_ref[k * t + i * tm + r0 + u]
                pltpu.make_async_copy(hbuf.at[slot, pl.ds(r0 + u, 1)],
                                      xs_hbm.at[pl.ds(row, 1)], sem.at[slot]).start()
        return c
    lax.fori_loop(0, tm // ROW_UNROLL, body, 0)

    @pl.when(i == n - 1)
    def _():
        wait_slot(slot)

        @pl.when(n >= 2)
        def _():
            wait_slot(1 - slot)

        for_each_pad_row(lambda row: zero_row_copy(row).wait())


def moe_dispatch(x, g, dest, pad_start, pad_len, n_active, total):
    t, d = x.shape
    tm = MOE_ROWS
    grid_spec = pltpu.PrefetchScalarGridSpec(
        num_scalar_prefetch=4,
        grid=(t // tm,),
        in_specs=[pl.BlockSpec((tm, d), lambda i, *_: (i, 0)),
                  pl.BlockSpec((1, d), lambda i, *_: (0, 0))],
        out_specs=pl.BlockSpec(memory_space=pl.ANY),
        scratch_shapes=[pltpu.VMEM((2, tm, d), F32),
                        pltpu.VMEM((8, d), F32),
                        pltpu.SemaphoreType.DMA((2,)),
                        pltpu.SemaphoreType.DMA((1,))],
    )
    return pl.pallas_call(
        _dispatch_kernel,
        grid_spec=grid_spec,
        out_shape=jax.ShapeDtypeStruct((total, d), F32),
        compiler_params=_cparams(("arbitrary",)),
        name="moe_dispatch",
    )(dest, pad_start, pad_len, n_active, x, g.reshape(1, d))


def _experts_kernel(layer, ge_ref, na_ref, pos_ref, len_ref, slot_ref, next_ref,
                    xs_ref, wg_hbm, wu_hbm, wd_hbm, o_ref, wg16, wu16, wd16, stage, sem):
    g = pl.program_id(0)
    e = ge_ref[g]
    half = stage.shape[1]
    w_hbm = (wg_hbm, wu_hbm, wd_hbm)
    w16 = (wg16, wu16, wd16)

    def half_src(p, expert, j):
        w = w_hbm[p].at[layer, expert]
        rows = pl.ds(j * half, half)
        return w.at[:, rows] if p == 2 else w.at[rows, :]

    def copies(p, expert):
        return [pltpu.make_async_copy(half_src(p, expert, j), stage.at[j], sem.at[j])
                for j in range(2)]

    def issue(p, expert):
        for c in copies(p, expert):
            c.start()

    def consume(p, expert, slot):
        for j, c in enumerate(copies(p, expert)):
            c.wait()
            for r in range(0, half, 256):
                v = stage[j, r:r + 256, :].astype(BF16)
                if p == 2:
                    w16[p][slot, r:r + 256, j * half:(j + 1) * half] = v
                else:
                    w16[p][slot, j * half + r:j * half + r + 256, :] = v

    @pl.when(g < na_ref[0])
    def _():
        slot = slot_ref[e]
        k = pos_ref[g]
        last = k == len_ref[g] - 1
        nxt = next_ref[e]
        has_next = nxt >= 0

        @pl.when(g == 0)
        def _():
            for p in range(3):
                issue(p, e)
                consume(p, e, slot)

        for p in range(3):
            @pl.when(jnp.logical_and(has_next, k == p + 1))
            def _():
                consume(p, nxt, 1 - slot)

        for p in range(3):
            @pl.when(jnp.logical_and(has_next, jnp.logical_or(k == p, jnp.logical_and(last, k < p))))
            def _():
                issue(p, nxt)

            @pl.when(jnp.logical_and(has_next, jnp.logical_and(last, k <= p)))
            def _():
                consume(p, nxt, 1 - slot)

        xb = xs_ref[...].astype(BF16)
        a = jnp.dot(xb, wg16[slot], preferred_element_type=F32)
        b = jnp.dot(xb, wu16[slot], preferred_element_type=F32)
        hid = (a * jax.nn.sigmoid(a) * b).astype(BF16)
        o_ref[...] = jnp.dot(hid, wd16[slot], preferred_element_type=F32)

    @pl.when(g >= na_ref[0])
    def _():
        o_ref[...] = jnp.zeros_like(o_ref)


def moe_experts(xs, runs, w_gate, w_up, w_down, layer):
    total, d = xs.shape
    ff = w_gate.shape[3]
    assert d == 2 * ff, "half of every weight matrix must have the staging shape (ff, ff)"
    n_groups = total // MOE_ROWS

    def blk(g, ge, na, *_):
        return (jnp.maximum(jnp.minimum(g, na[0] - 1), 0), 0)

    return pl.pallas_call(
        functools.partial(_experts_kernel, layer),
        grid_spec=pltpu.PrefetchScalarGridSpec(
            num_scalar_prefetch=len(runs), grid=(n_groups,),
            in_specs=[pl.BlockSpec((MOE_ROWS, d), blk)] + [pl.BlockSpec(memory_space=pl.ANY)] * 3,
            out_specs=pl.BlockSpec((MOE_ROWS, d), lambda g, *_: (g, 0)),
            scratch_shapes=[pltpu.VMEM((2, d, ff), BF16),
                            pltpu.VMEM((2, d, ff), BF16),
                            pltpu.VMEM((2, ff, d), BF16),
                            pltpu.VMEM((2, ff, ff), F32),
                            pltpu.SemaphoreType.DMA((2,))]),
        out_shape=jax.ShapeDtypeStruct((total, d), F32),
        compiler_params=_cparams(("arbitrary",)),
        name="moe_experts",
    )(*runs, xs, w_gate, w_up, w_down)


def _combine_kernel(emit_x, emit_norm, pos_ref, yb_hbm, x_ref, p_ref, g_ref, *rest):
    out_refs, (ybuf, sem) = rest[:-2], rest[-2:]
    i = pl.program_id(0)
    n = pl.num_programs(0)
    tm = x_ref.shape[0]
    t = n * tm

    def issue(tile, slot):
        def body(rb, c):
            r0 = pl.multiple_of(rb * ROW_UNROLL, ROW_UNROLL)
            for u in range(ROW_UNROLL):
                for k in range(2):
                    row = pos_ref[k * t + tile * tm + r0 + u]
                    pltpu.make_async_copy(yb_hbm.at[pl.ds(row, 1)],
                                          ybuf.at[slot, k, pl.ds(r0 + u, 1)],
                                          sem.at[slot]).start()
            return c
        lax.fori_loop(0, tm // ROW_UNROLL, body, 0)

    @pl.when(i == 0)
    def _():
        issue(0, 0)

    @pl.when(i + 1 < n)
    def _():
        issue(i + 1, (i + 1) % 2)

    slot = i % 2
    for k in range(2):
        pltpu.make_async_copy(yb_hbm.at[pl.ds(0, tm)], ybuf.at[slot, k], sem.at[slot]).wait()
    p = p_ref[...]
    out = x_ref[...] + (p[:, 0:1] * ybuf[slot, 0] + p[:, 1:2] * ybuf[slot, 1])
    outs = ([out] if emit_x else []) + ([_rms(out, g_ref[...])] if emit_norm else [])
    for o_ref, val in zip(out_refs, outs):
        o_ref[...] = val


def moe_combine(yb, dest, gates_t, x, norm_g=None, keep_x=True, tm=256):
    t, d = x.shape
    emit_norm = norm_g is not None
    emit_x = keep_x or not emit_norm
    g = (norm_g if emit_norm else jnp.ones((d,), F32)).reshape(1, d)
    n_out = int(emit_x) + int(emit_norm)
    row_spec = pl.BlockSpec((tm, d), lambda i, pos: (i, 0))
    grid_spec = pltpu.PrefetchScalarGridSpec(
        num_scalar_prefetch=1,
        grid=(t // tm,),
        in_specs=[pl.BlockSpec(memory_space=pl.ANY),
                  row_spec,
                  pl.BlockSpec((tm, 2), lambda i, pos: (i, 0)),
                  pl.BlockSpec((1, d), lambda i, pos: (0, 0))],
        out_specs=[row_spec] * n_out,
        scratch_shapes=[pltpu.VMEM((2, 2, tm, d), F32),
                        pltpu.SemaphoreType.DMA((2,))],
    )
    outs = pl.pallas_call(
        functools.partial(_combine_kernel, emit_x, emit_norm),
        grid_spec=grid_spec,
        out_shape=[jax.ShapeDtypeStruct((t, d), F32)] * n_out,
        compiler_params=_cparams(("arbitrary",)),
        name="moe_combine",
    )(dest, yb, x, gates_t, g)
    return outs if n_out > 1 else outs[0]


def hier_moe_residual(x, g_ffn, w_coarse, b_coarse, w_fine, b_fine, w_gate, w_up, w_down,
                      layer, norm_g=None, keep_x=True):
    experts, gates, rank, counts = moe_router(x, g_ffn, w_coarse, b_coarse, w_fine, b_fine)
    dest, runs, pad_start, pad_len = _dispatch_plan(experts, rank, counts)
    total = runs[0].shape[0] * MOE_ROWS
    xs = moe_dispatch(x, g_ffn, dest, pad_start, pad_len, runs[1], total)
    yb = moe_experts(xs, runs, w_gate, w_up, w_down, layer)
    return moe_combine(yb, dest, gates.T, x, norm_g, keep_x)


def kernel(x, g_mix, g_ffn, g_final, w_in_even, conv_w, hg_lower_bound, hg_norm_g, w_out_even, s5_lambda_re, s5_lambda_im, s5_log_dt, s5_b_re, s5_b_im, s5_c_re, s5_c_im, s5_d, w_glu, moe_w_coarse, moe_b_coarse, moe_w_fine, moe_b_fine, moe_w_gate, moe_w_up, moe_w_down):
    bsz, seq, d = x.shape
    depth = g_mix.shape[0]
    conv_width = conv_w.shape[2]
    hg_width = hg_lower_bound.shape[1]
    lb_all = jnp.cumsum(jax.nn.softmax(hg_lower_bound.astype(F32), axis=0), axis=0)
    xf = x.reshape(bsz * seq, d)
    h = None
    for layer in range(depth):
        j = layer // 2
        last = layer == depth - 1
        if layer % 2 == 0:
            z = norm_matmul(xf, g_mix[layer], w_in_even[j].astype(BF16))
            o_b = hgrn2(z, lb_all[j], hg_norm_g[j], bsz, seq, hg_width, 3 * conv_width)
            xf = outproj_even(z, conv_w[j], o_b, xf, w_out_even[j].astype(BF16), seq)
        else:
            if h is None:
                h = rmsnorm_cast(xf, g_mix[layer], F32)
            tables = _s5_tables(s5_lambda_re[j], s5_lambda_im[j], s5_log_dt[j],
                                s5_b_re[j], s5_b_im[j], s5_c_re[j], s5_c_im[j], s5_d[j])
            y = s5_mix(h, tables, bsz, seq)
            xf = glu_residual(y, xf, w_glu[j].astype(BF16))
        next_s5 = not last and (layer + 1) % 2 == 1
        norm_g = g_final if last else (g_mix[layer + 1] if next_s5 else None)
        out = hier_moe_residual(
            xf, g_ffn[layer], moe_w_coarse[layer], moe_b_coarse[layer], moe_w_fine[layer],
            moe_b_fine[layer], moe_w_gate, moe_w_up, moe_w_down, layer, norm_g,
            keep_x=not last)
        xf, h = out if next_s5 else (out, None)
    return xf.reshape(bsz, seq, d)
```

```python
import functools
import math

import jax
import jax.numpy as jnp
from jax import lax
from jax.experimental import pallas as pl
from jax.experimental.pallas import tpu as pltpu

F32 = jnp.float32
BF16 = jnp.bfloat16
EPS = 1e-6

LANES = 128
HG_HEAD_DIM = 128
HG_CHUNK = 64
CONV_K = 3
S5_GROUP = 16
S5_STATE = 64
S5_CHUNK = 64
MOE_GROUPS = 4
MOE_EPG = 8
MOE_EXPERTS = MOE_GROUPS * MOE_EPG
MOE_ROWS = 256
ROUTER_ROWS = 40
ROW_UNROLL = 8
VMEM_LIMIT = 56 * 1024 * 1024


def _cparams(sem, vmem=VMEM_LIMIT):
    return pltpu.CompilerParams(dimension_semantics=sem, vmem_limit_bytes=vmem)


def _rms(x, g):
    ms = jnp.mean(x * x, axis=-1, keepdims=True)
    return x * lax.rsqrt(ms + EPS) * g


def _norm_kernel(x_ref, g_ref, o_ref):
    o_ref[...] = _rms(x_ref[...], g_ref[...]).astype(o_ref.dtype)


def rmsnorm_cast(x, g, dtype, tm=512):
    t, d = x.shape
    return pl.pallas_call(
        _norm_kernel,
        grid=(t // tm,),
        in_specs=[pl.BlockSpec((tm, d), lambda i: (i, 0)),
                  pl.BlockSpec((1, d), lambda i: (0, 0))],
        out_specs=pl.BlockSpec((tm, d), lambda i: (i, 0)),
        out_shape=jax.ShapeDtypeStruct((t, d), dtype),
        compiler_params=_cparams(("parallel",)),
        name="rmsnorm_cast",
    )(x, g.reshape(1, d))


def _norm_matmul_kernel(x_ref, g_ref, w_ref, o_ref, h_ref):
    @pl.when(pl.program_id(1) == 0)
    def _():
        h_ref[...] = _rms(x_ref[...], g_ref[...]).astype(BF16)

    o_ref[...] = jnp.dot(h_ref[...], w_ref[...],
                         preferred_element_type=F32).astype(o_ref.dtype)


def norm_matmul(x, g, w, tm=1024, tn=1024):
    t, d = x.shape
    n = w.shape[1]
    tm = min(tm, t)
    return pl.pallas_call(
        _norm_matmul_kernel,
        grid=(t // tm, n // tn),
        in_specs=[pl.BlockSpec((tm, d), lambda i, j: (i, 0)),
                  pl.BlockSpec((1, d), lambda i, j: (0, 0)),
                  pl.BlockSpec((d, tn), lambda i, j: (0, j))],
        out_specs=pl.BlockSpec((tm, tn), lambda i, j: (i, j)),
        out_shape=jax.ShapeDtypeStruct((t, n), BF16),
        scratch_shapes=[pltpu.VMEM((tm, d), BF16)],
        compiler_params=_cparams(("parallel", "arbitrary")),
        name="norm_matmul",
    )(x, g.reshape(1, d), w)


def _split2(x):
    hi = x.astype(BF16)
    return hi, (x - hi.astype(F32)).astype(BF16)


def _hgrn2_kernel(q_ref, f_ref, v_ref, g_ref, lb_ref, ng_ref, o_ref, st_ref):
    c = HG_CHUNK
    hd = HG_HEAD_DIM
    heads = st_ref.shape[0]
    n_chunks = q_ref.shape[0] // c
    ng = ng_ref[...]
    row = lax.broadcasted_iota(jnp.int32, (c, c), 0)
    col = lax.broadcasted_iota(jnp.int32, (c, c), 1)
    causal = row >= col
    tril = causal.astype(BF16)
    mid = c // 2

    @pl.when(pl.program_id(1) == 0)
    def _():
        st_ref[...] = jnp.zeros_like(st_ref)

    nt_dims = (((1,), (1,)), ((), ()))
    hs = range(heads)
    cols = [slice(h * hd, (h + 1) * hd) for h in hs]

    def step(n, carry):
        rows = pl.ds(pl.multiple_of(n * c, c), c)
        q = [q_ref[rows, cl].astype(F32) for cl in cols]
        f = [lb_ref[:, cl] + (1.0 - lb_ref[:, cl]) * jax.nn.sigmoid(f_ref[rows, cl].astype(F32))
             for cl in cols]
        k = [1.0 - fh for fh in f]
        parts = [_split2(jnp.log(fh)) for fh in f]
        b = [sum(jnp.dot(tril, p, preferred_element_type=F32) for p in ph) for ph in parts]
        ref = [bh[mid - 1:mid, :] for bh in b]
        b_last = [bh[c - 1:c, :] for bh in b]
        st = [st_ref[h] for h in hs]
        qs = [(q[h] * jnp.exp(b[h] - ref[h])).astype(BF16) for h in hs]
        ks = [(k[h] * jnp.exp(ref[h] - b[h])).astype(BF16) for h in hs]
        scores = [lax.dot_general(qs[h], ks[h], nt_dims, preferred_element_type=F32) for h in hs]
        qe = [(q[h] * jnp.exp(b[h])).astype(BF16) for h in hs]
        o_inter = [lax.dot_general(qe[h], st[h].astype(BF16), nt_dims, preferred_element_type=F32)
                   for h in hs]
        kd = [(k[h] * jnp.exp(b_last[h] - b[h])).astype(BF16) for h in hs]
        upd = [lax.dot_general(v_ref[rows, cols[h]], kd[h], (((0,), (0,)), ((), ())),
                               preferred_element_type=F32) for h in hs]
        for h in hs:
            st_ref[h] = st[h] * jnp.exp(b_last[h]) + upd[h]
        sc = [jnp.where(causal, s, 0.0).astype(BF16) for s in scores]
        o = [o_inter[h] + jnp.dot(sc[h], v_ref[rows, cols[h]], preferred_element_type=F32)
             for h in hs]
        for h in hs:
            oh = o[h] * lax.rsqrt(jnp.mean(o[h] * o[h], axis=-1, keepdims=True) + EPS) * ng
            gate = g_ref[rows, cols[h]].astype(F32)
            o_ref[rows, cols[h]] = (oh * (gate * jax.nn.sigmoid(gate))).astype(o_ref.dtype)
        return carry

    lax.fori_loop(0, n_chunks, step, 0, unroll=4)


def hgrn2(z, lb, norm_g, bsz, seq, width, col0, ts=1024):
    hd = HG_HEAD_DIM
    heads = width // hd
    cb = col0 // width
    ts = min(ts, seq)
    nt = seq // ts

    def zspec(k):
        return pl.BlockSpec((ts, width), lambda b, s: (b * nt + s, cb + k))

    return pl.pallas_call(
        _hgrn2_kernel,
        grid=(bsz, nt),
        in_specs=[zspec(0), zspec(1), zspec(2), zspec(3),
                  pl.BlockSpec((1, width), lambda b, s: (0, 0)),
                  pl.BlockSpec((1, hd), lambda b, s: (0, 0))],
        out_specs=pl.BlockSpec((ts, width), lambda b, s: (b * nt + s, 0)),
        out_shape=jax.ShapeDtypeStruct((bsz * seq, width), BF16),
        scratch_shapes=[pltpu.VMEM((heads, hd, hd), F32)],
        compiler_params=_cparams(("parallel", "arbitrary")),
        name="hgrn2",
    )(z, z, z, z, lb.reshape(1, width), norm_g.reshape(1, hd))


def _outproj_kernel(seq_tiles, ab_ref, ac_ref, ah_ref, hc_ref, hh_ref, cw_ref,
                    ob_ref, x_ref, wa_ref, wb_ref, o_ref):
    i = pl.program_id(0)
    u = ac_ref[...].astype(F32) * ah_ref[...].astype(F32)
    halo = hc_ref[...].astype(F32) * hh_ref[...].astype(F32)
    halo = jnp.where(i % seq_tiles == 0, 0.0, halo)
    hr = halo.shape[0]
    row = lax.broadcasted_iota(jnp.int32, u.shape, 0)
    u1 = jnp.where(row == 0, halo[hr - 1:hr, :], pltpu.roll(u, 1, axis=0))
    u2 = jnp.where(row == 0, halo[hr - 2:hr - 1, :],
                   jnp.where(row == 1, halo[hr - 1:hr, :], pltpu.roll(u, 2, axis=0)))
    cw = cw_ref[...]
    conv = cw[2:3, :] * u + cw[1:2, :] * u1 + cw[0:1, :] * u2
    ya = (ab_ref[...].astype(F32) * conv).astype(BF16)
    mix = jnp.dot(ya, wa_ref[...], preferred_element_type=F32)
    mix = mix + jnp.dot(ob_ref[...], wb_ref[...], preferred_element_type=F32)
    o_ref[...] = x_ref[...] + mix


def outproj_even(z, conv_w, o_b, x, w_out, seq, tm=512, halo=16):
    t, d = x.shape
    tm = min(tm, seq)
    cw = conv_w.shape[1]
    hw = o_b.shape[1]
    hb = tm // halo
    kern = functools.partial(_outproj_kernel, seq // tm)
    return pl.pallas_call(
        kern,
        grid=(t // tm,),
        in_specs=[pl.BlockSpec((tm, cw), lambda i: (i, 0)),
                  pl.BlockSpec((tm, cw), lambda i: (i, 1)),
                  pl.BlockSpec((tm, cw), lambda i: (i, 2)),
                  pl.BlockSpec((halo, cw), lambda i: (jnp.maximum(i * hb - 1, 0), 1)),
                  pl.BlockSpec((halo, cw), lambda i: (jnp.maximum(i * hb - 1, 0), 2)),
                  pl.BlockSpec((CONV_K, cw), lambda i: (0, 0)),
                  pl.BlockSpec((tm, hw), lambda i: (i, 0)),
                  pl.BlockSpec((tm, d), lambda i: (i, 0)),
                  pl.BlockSpec((cw, d), lambda i: (0, 0)),
                  pl.BlockSpec((hw, d), lambda i: (1, 0))],
        out_specs=pl.BlockSpec((tm, d), lambda i: (i, 0)),
        out_shape=jax.ShapeDtypeStruct((t, d), F32),
        compiler_params=_cparams(("parallel",)),
        name="outproj_even",
    )(z, z, z, z, z, conv_w, o_b, x, w_out, w_out)


def _rep_rows(x, n):
    r, l = x.shape
    return jnp.broadcast_to(x[:, None, :], (r, n, l)).reshape(r * n, l)


def _tile_rows(x, n):
    r, l = x.shape
    return jnp.broadcast_to(x[None, :, :], (n, r, l)).reshape(n * r, l)


def _s5_group(n_chunks, prm, u, m_ref):
    lc = S5_CHUNK
    gc = S5_GROUP
    half = S5_STATE
    nt_dims = (((1,), (1,)), ((), ()))
    bqa, ce1, ce2, bwa, bwb, cga, cgb = (prm[i * gc:(i + 1) * gc, :] for i in range(7))
    alpha = prm[8 * gc:8 * gc + 1, :]
    beta = prm[8 * gc + 1:8 * gc + 2, :]

    def cpow(tt):
        mag = jnp.exp(alpha * tt)
        ang = beta * tt
        return mag * jnp.cos(ang), mag * jnp.sin(ang)

    t_col = lax.broadcasted_iota(jnp.int32, (lc, LANES), 0).astype(F32)
    p0r, p0i = cpow(t_col)
    prr, pri = cpow(float(lc - 1) - t_col)
    a1r, a1i = cpow(jnp.full((1, LANES), 1.0, F32))
    p1r, p1i = p0r * a1r - p0i * a1i, p0r * a1i + p0i * a1r

    ecat = _rep_rows(p0r, gc) * _tile_rows(ce1, lc) + _rep_rows(p0i, gc) * _tile_rows(ce2, lc)
    b_hi, b_lo = _split2(bqa)
    e_hi, e_lo = _split2(ecat)
    nt_dot = lambda a, b: lax.dot_general(a, b, nt_dims, preferred_element_type=F32)
    k_row = nt_dot(b_hi, e_hi) + nt_dot(b_hi, e_lo) + nt_dot(b_lo, e_hi)
    spv = LANES // gc
    width = lc * gc
    pos = lax.broadcasted_iota(jnp.int32, k_row.shape, 1)
    for r in range(spv):
        base = k_row if r == 0 else jnp.where(pos >= r * gc, pltpu.roll(k_row, r * gc, axis=1), 0.0)
        base = base.astype(BF16)
        for q in range(lc // spv):
            s = q * spv + r
            blk = base if q == 0 else jnp.concatenate(
                [jnp.zeros((gc, q * LANES), BF16), base[:, :width - q * LANES]], axis=1)
            m_ref[s * gc:(s + 1) * gc, :] = blk

    wcat = (_rep_rows(prr, gc) * _tile_rows(bwa, lc)
            + _rep_rows(pri, gc) * _tile_rows(bwb, lc)).astype(BF16)
    st = jnp.dot(u, wcat, preferred_element_type=F32)

    rows = st.shape[0]
    rpos = lax.broadcasted_iota(jnp.int32, (rows, LANES), 0) % n_chunks
    rlane = lax.broadcasted_iota(jnp.int32, (rows, LANES), 1)
    ar, ai = cpow(jnp.full((1, LANES), float(lc), F32))
    d = 1
    while d < n_chunks:
        sh = jnp.where(rpos >= d, pltpu.roll(st, d, axis=0), 0.0)
        a2 = jnp.where(rlane < half, -ai, ai)
        st = st + ar * sh + a2 * pltpu.roll(sh, half, axis=1)
        ar, ai = ar * ar - ai * ai, 2.0 * ar * ai
        d *= 2
    h0 = jnp.where(rpos >= 1, pltpu.roll(st, 1, axis=0), 0.0).astype(BF16)

    gcat = (_rep_rows(p1r, gc) * _tile_rows(cga, lc)
            + _rep_rows(p1i, gc) * _tile_rows(cgb, lc)).astype(BF16)
    y = jnp.dot(u, m_ref[...], preferred_element_type=F32)
    y = y + lax.dot_general(h0, gcat, nt_dims, preferred_element_type=F32)
    d_row = prm[8 * gc + 2:8 * gc + 3, :]
    return y + jnp.concatenate([d_row] * (lc * gc // LANES), axis=1) * u.astype(F32)


def _s5_kernel(n_chunks, h_ref, p_ref, perm_ref, y_ref, u_ref, m_ref):
    lc = S5_CHUNK
    gps = u_ref.shape[0]
    rows = u_ref.shape[1]
    spv = LANES // S5_GROUP
    lane_blk = lambda k: slice(k * LANES, (k + 1) * LANES)

    for j in range(lc // spv):
        xcat = jnp.concatenate(
            [h_ref[pl.ds(j * spv + s, rows, stride=lc), :] for s in range(spv)], axis=1)
        uall = jnp.dot(xcat.astype(BF16), perm_ref[...], preferred_element_type=F32).astype(BF16)
        for gi in range(gps):
            u_ref[gi, :, lane_blk(j)] = uall[:, lane_blk(gi)]

    n_m = m_ref.shape[0]

    def groups(i, carry):
        for b in range(n_m):
            gi = i * n_m + b
            y = _s5_group(n_chunks, p_ref[gi], u_ref[gi], m_ref.at[b])
            u_ref[gi] = y.astype(BF16)
        return carry
    lax.fori_loop(0, gps // n_m, groups, 0)

    for j in range(lc // spv):
        ycat = jnp.concatenate([u_ref[gi, :, lane_blk(j)] for gi in range(gps)], axis=1)
        z = jnp.dot(ycat, perm_ref[...], preferred_element_type=F32)
        for s in range(spv):
            y_ref[pl.ds(j * spv + s, rows, stride=lc), :] = z[:, lane_blk(s)]


def _s5_tables(lam_re, lam_im, log_dt, b_re, b_im, c_re, c_im, d_skip):
    dt = jnp.exp(log_dt.astype(F32))[:, None]
    lr = lam_re.astype(F32)
    li = lam_im.astype(F32)
    mag = jnp.exp(lr * dt)
    ar = mag * jnp.cos(li * dt)
    ai = mag * jnp.sin(li * dt)
    den = lr * lr + li * li
    cr = ((ar - 1.0) * lr + ai * li) / den
    ci = (ai * lr - (ar - 1.0) * li) / den
    bbr = (cr[..., None] * b_re - ci[..., None] * b_im).transpose(0, 2, 1)
    bbi = (cr[..., None] * b_im + ci[..., None] * b_re).transpose(0, 2, 1)
    cre = c_re.astype(F32)
    cim = c_im.astype(F32)
    cat = lambda a, b: jnp.concatenate([a, b], axis=-1)
    alpha = (lr * dt)[:, None, :]
    beta = (li * dt)[:, None, :]
    groups, states = lr.shape
    gc = b_re.shape[2]
    d_row = jnp.tile(d_skip.astype(F32).reshape(groups, 1, gc), (1, 1, 2 * states // gc))
    pad = jnp.zeros((groups, 5, 2 * states), F32)
    return jnp.concatenate([
        cat(bbr, -bbi), cat(cre, cim), cat(-cim, cre),
        cat(bbr, bbi), cat(-bbi, bbr),
        cat(cre, -cim), cat(-cim, -cre), jnp.zeros_like(cat(cre, cre)),
        cat(alpha, alpha), cat(beta, beta), d_row, pad], axis=1)


def s5_mix(h, tables, bsz, seq):
    t, w = h.shape
    lc, gc = S5_CHUNK, S5_GROUP
    gps = LANES // gc
    slabs = w // LANES
    n = seq // lc
    rows = bsz * n
    k = lc * gc
    idx = jnp.arange(k, dtype=jnp.int32)
    tgt = (idx % LANES) // gc * LANES + idx // LANES * gc + idx % gc
    perm = (tgt[:, None] == idx[None, :]).astype(BF16)
    tables = tables.reshape((slabs, gps) + tables.shape[1:])
    return pl.pallas_call(
        functools.partial(_s5_kernel, n),
        grid=(slabs,),
        in_specs=[pl.BlockSpec((t, LANES), lambda i: (0, i)),
                  pl.BlockSpec((None, gps) + tables.shape[2:], lambda i: (i, 0, 0, 0)),
                  pl.BlockSpec((k, k), lambda i: (0, 0))],
        out_specs=pl.BlockSpec((t, LANES), lambda i: (0, i)),
        out_shape=jax.ShapeDtypeStruct((t, w), F32),
        scratch_shapes=[pltpu.VMEM((gps, rows, k), BF16),
                        pltpu.VMEM((2, k, k), BF16)],
        compiler_params=_cparams(("parallel",)),
        name="s5_mix",
    )(h, tables, perm)


def _glu_kernel(tn, y_ref, x_ref, w_ref, o_ref):
    y = y_ref[...]
    cdf = 0.5 * (1.0 + jnp.tanh(math.sqrt(2.0 / math.pi) * (y + 0.044715 * (y * y * y))))
    act = (y * cdf).astype(BF16)
    d = o_ref.shape[1]
    for c in range(0, d, tn):
        za = jnp.dot(act, w_ref[:, c:c + tn], preferred_element_type=F32)
        zb = jnp.dot(act, w_ref[:, d + c:d + c + tn], preferred_element_type=F32)
        o_ref[:, c:c + tn] = x_ref[:, c:c + tn] + za * jax.nn.sigmoid(zb)


def glu_residual(y, x, w_glu, tm=512, tn=512):
    t, d = x.shape
    w = y.shape[1]
    tm = min(tm, t)
    return pl.pallas_call(
        functools.partial(_glu_kernel, tn),
        grid=(t // tm,),
        in_specs=[pl.BlockSpec((tm, w), lambda i: (i, 0)),
                  pl.BlockSpec((tm, d), lambda i: (i, 0)),
                  pl.BlockSpec((w, 2 * d), lambda i: (0, 0), pipeline_mode=pl.Buffered(1))],
        out_specs=pl.BlockSpec((tm, d), lambda i: (i, 0)),
        out_shape=jax.ShapeDtypeStruct((t, d), F32),
        compiler_params=_cparams(("parallel",)),
        name="glu_residual",
    )(y, x, w_glu)


def _router_kernel(x_ref, g_ref, w_ref, b_ref, e_ref, p_ref, r_ref, c_ref, cnt_ref):
    tm = x_ref.shape[0]

    @pl.when(pl.program_id(0) == 0)
    def _():
        cnt_ref[...] = jnp.zeros_like(cnt_ref)

    h = _rms(x_ref[...], g_ref[...])
    nt_dot = lambda a, b: lax.dot_general(a, b, (((1,), (1,)), ((), ())),
                                          preferred_element_type=F32)
    w_hi, w_lo = _split2(w_ref[...])
    h_hi, h_lo = _split2(h)
    logits = (nt_dot(w_hi, h_hi) + nt_dot(w_hi, h_lo) + nt_dot(w_lo, h_hi)
              + b_ref[...])
    lc = [logits[i:i + 1, :] for i in range(MOE_GROUPS)]
    m = functools.reduce(jnp.maximum, lc)
    grp = jnp.full(m.shape, MOE_GROUPS - 1, jnp.int32)
    for i in range(MOE_GROUPS - 2, -1, -1):
        grp = jnp.where(lc[i] == m, i, grp)
    den = functools.reduce(lambda a, b: a + b, [jnp.exp(l - m) for l in lc])
    p_top = 1.0 / den
    fine = [logits[MOE_GROUPS + j:MOE_GROUPS + j + 1, :] for j in range(MOE_EXPERTS)]
    sel = []
    for e in range(MOE_EPG):
        v = fine[(MOE_GROUPS - 1) * MOE_EPG + e]
        for i in range(MOE_GROUPS - 2, -1, -1):
            v = jnp.where(grp == i, fine[i * MOE_EPG + e], v)
        sel.append(v)

    def top1(vals):
        best = functools.reduce(jnp.maximum, vals)
        idx = jnp.full(best.shape, MOE_EPG - 1, jnp.int32)
        for e in range(MOE_EPG - 2, -1, -1):
            idx = jnp.where(vals[e] == best, e, idx)
        return best, idx

    v1, i1 = top1(sel)
    v2, i2 = top1([jnp.where(i1 == e, -jnp.inf, sel[e]) for e in range(MOE_EPG)])
    ex = jnp.exp(v2 - v1)
    s = 1.0 + ex
    e0 = grp * MOE_EPG + i1
    e1 = grp * MOE_EPG + i2
    e_ref[0:1, :] = e0
    e_ref[1:2, :] = e1
    p_ref[0:1, :] = p_top * (1.0 / s)
    p_ref[1:2, :] = p_top * (ex / s)

    eid = lax.broadcasted_iota(jnp.int32, (MOE_EXPERTS, tm), 0)
    tri = (lax.broadcasted_iota(jnp.int32, (tm, tm), 0)
           <= lax.broadcasted_iota(jnp.int32, (tm, tm), 1)).astype(BF16)
    base = cnt_ref[...]
    for k, ek in enumerate((e0, e1)):
        hot = eid == ek
        hot_f = hot.astype(F32)
        csum = jnp.dot(hot.astype(BF16), tri, preferred_element_type=F32)
        rank = jnp.sum(hot_f * (base + csum - 1.0), axis=0, keepdims=True)
        r_ref[k:k + 1, :] = rank.astype(jnp.int32)
        base = base + jnp.sum(hot_f, axis=1, keepdims=True)
    cnt_ref[...] = base
    c_ref[...] = base.astype(jnp.int32)


def moe_router(x, g, w_coarse, b_coarse, w_fine, b_fine, tm=512):
    t, d = x.shape
    pad = ROUTER_ROWS - MOE_GROUPS - MOE_EXPERTS
    w = jnp.concatenate([w_coarse.T, w_fine.T, jnp.zeros((pad, d), F32)], axis=0)
    b = jnp.concatenate([b_coarse, b_fine, jnp.zeros((pad,), F32)]).reshape(ROUTER_ROWS, 1)
    tok_spec = pl.BlockSpec((2, tm), lambda i: (0, i))
    return pl.pallas_call(
        _router_kernel,
        grid=(t // tm,),
        in_specs=[pl.BlockSpec((tm, d), lambda i: (i, 0)),
                  pl.BlockSpec((1, d), lambda i: (0, 0)),
                  pl.BlockSpec((ROUTER_ROWS, d), lambda i: (0, 0)),
                  pl.BlockSpec((ROUTER_ROWS, 1), lambda i: (0, 0))],
        out_specs=[tok_spec, tok_spec, tok_spec,
                   pl.BlockSpec((MOE_EXPERTS, 1), lambda i: (0, 0))],
        out_shape=[jax.ShapeDtypeStruct((2, t), jnp.int32),
                   jax.ShapeDtypeStruct((2, t), F32),
                   jax.ShapeDtypeStruct((2, t), jnp.int32),
                   jax.ShapeDtypeStruct((MOE_EXPERTS, 1), jnp.int32)],
        scratch_shapes=[pltpu.VMEM((MOE_EXPERTS, 1), F32)],
        compiler_params=_cparams(("arbitrary",)),
        name="moe_router",
    )(x, g.reshape(1, d), w, b)


def _dispatch_plan(experts, rank, counts):
    n_assign = experts.size
    counts = counts.reshape(-1)
    padded = (counts + MOE_ROWS - 1) // MOE_ROWS * MOE_ROWS
    pend = jnp.cumsum(padded)
    pstart = pend - padded
    eids = jnp.arange(MOE_EXPERTS, dtype=jnp.int32)
    dest = rank + jnp.sum(jnp.where(experts[..., None] == eids, pstart, 0), axis=-1)
    n_groups = -(-n_assign // MOE_ROWS) + MOE_EXPERTS
    g0 = jnp.arange(n_groups, dtype=jnp.int32) * MOE_ROWS
    grp_expert = jnp.minimum(jnp.sum(pend[None, :] <= g0[:, None], axis=1), MOE_EXPERTS - 1)
    n_active = (pend[-1] // MOE_ROWS).reshape(1)
    of_group = lambda v: jnp.sum(jnp.where(grp_expert[:, None] == eids, v, 0), axis=1)
    grp_pos = (g0 - of_group(pstart)) // MOE_ROWS
    grp_len = of_group(padded) // MOE_ROWS
    used = counts > 0
    e_slot = (jnp.cumsum(used) - 1) % 2
    later_used = jnp.logical_and(used[None, :], eids[None, :] > eids[:, None])
    e_next = jnp.min(jnp.where(later_used, eids[None, :], MOE_EXPERTS), axis=1)
    e_next = jnp.where(e_next == MOE_EXPERTS, -1, e_next)
    i32 = lambda a: a.astype(jnp.int32)
    runs = (i32(grp_expert), i32(n_active), i32(grp_pos), i32(grp_len), i32(e_slot), i32(e_next))
    return i32(dest).reshape(-1), runs, i32(pstart + counts), i32(padded - counts)


def _dispatch_kernel(dest_ref, ps_ref, pn_ref, na_ref, x_ref, g_ref, xs_hbm, hbuf, zbuf, sem,
                     zsem):
    i = pl.program_id(0)
    n = pl.num_programs(0)
    tm = x_ref.shape[0]
    t = n * tm
    slot = i % 2

    def wait_slot(s):
        for _ in range(2):
            pltpu.make_async_copy(hbuf.at[s], xs_hbm.at[pl.ds(0, tm)], sem.at[s]).wait()

    def zero_row_copy(row):
        return pltpu.make_async_copy(zbuf.at[pl.ds(0, 1)], xs_hbm.at[pl.ds(row, 1)], zsem.at[0])

    def for_each_pad_row(fn):
        def per_expert(e, c):
            def per_row(r, c2):
                fn(ps_ref[e] + r)
                return c2
            return lax.fori_loop(0, pn_ref[e], per_row, c)
        lax.fori_loop(0, ps_ref.shape[0], per_expert, 0)

    def tail_group_copy(grp):
        rows = pl.ds(pl.multiple_of(grp * tm, tm), tm)
        return pltpu.make_async_copy(hbuf.at[1], xs_hbm.at[rows], sem.at[1])

    def for_each_tail_group(fn):
        def body(grp, c):
            fn(grp)
            return c
        lax.fori_loop(na_ref[0], xs_hbm.shape[0] // tm, body, 0)

    @pl.when(i == 0)
    def _():
        zbuf[...] = jnp.zeros_like(zbuf)
        for_each_pad_row(lambda row: zero_row_copy(row).start())
        hbuf[1] = jnp.zeros(hbuf.shape[1:], hbuf.dtype)
        for_each_tail_group(lambda grp: tail_group_copy(grp).start())
        for_each_tail_group(lambda grp: tail_group_copy(grp).wait())

    @pl.when(i >= 2)
    def _():
        wait_slot(slot)

    hbuf[slot] = _rms(x_ref[...], g_ref[...])

    def body(rb, c):
        r0 = pl.multiple_of(rb * ROW_UNROLL, ROW_UNROLL)
        for u in range(ROW_UNROLL):
            for k in range(2):
                row = dest_ref[k * t + i * tm + r0 + u]
                pltpu.make_async_copy(hbuf.at[slot, pl.ds(r0 + u, 1)],
                                      xs_hbm.at[pl.ds(row, 1)], sem.at[slot]).start()
        return c
    lax.fori_loop(0, tm // ROW_UNROLL, body, 0)

    @pl.when(i == n - 1)
    def _():
        wait_slot(slot)

        @pl.when(n >= 2)
        def _():
            wait_slot(1 - slot)

        for_each_pad_row(lambda row: zero_row_copy(row).wait())


def moe_dispatch(x, g, dest, pad_start, pad_len, n_active, total):
    t, d = x.shape
    tm = MOE_ROWS
    grid_spec = pltpu.PrefetchScalarGridSpec(
        num_scalar_prefetch=4,
        grid=(t // tm,),
        in_specs=[pl.BlockSpec((tm, d), lambda i, *_: (i, 0)),
                  pl.BlockSpec((1, d), lambda i, *_: (0, 0))],
        out_specs=pl.BlockSpec(memory_space=pl.ANY),
        scratch_shapes=[pltpu.VMEM((2, tm, d), F32),
                        pltpu.VMEM((8, d), F32),
                        pltpu.SemaphoreType.DMA((2,)),
                        pltpu.SemaphoreType.DMA((1,))],
    )
    return pl.pallas_call(
        _dispatch_kernel,
        grid_spec=grid_spec,
        out_shape=jax.ShapeDtypeStruct((total, d), F32),
        compiler_params=_cparams(("arbitrary",)),
        name="moe_dispatch",
    )(dest, pad_start, pad_len, n_active, x, g.reshape(1, d))


def _experts_kernel(layer, ge_ref, na_ref, pos_ref, len_ref, slot_ref, next_ref,
                    xs_ref, wg_hbm, wu_hbm, wd_hbm, o_ref, wg16, wu16, wd16, stage, sem):
    g = pl.program_id(0)
    e = ge_ref[g]
    half = stage.shape[1]
    w_hbm = (wg_hbm, wu_hbm, wd_hbm)
    w16 = (wg16, wu16, wd16)

    def half_src(p, expert, j):
        w = w_hbm[p].at[layer, expert]
        rows = pl.ds(j * half, half)
        return w.at[:, rows] if p == 2 else w.at[rows, :]

    def copies(p, expert):
        return [pltpu.make_async_copy(half_src(p, expert, j), stage.at[j], sem.at[j])
                for j in range(2)]

    def issue(p, expert):
        for c in copies(p, expert):
            c.start()

    def consume(p, expert, slot):
        for j, c in enumerate(copies(p, expert)):
            c.wait()
            for r in range(0, half, 256):
                v = stage[j, r:r + 256, :].astype(BF16)
                if p == 2:
                    w16[p][slot, r:r + 256, j * half:(j + 1) * half] = v
                else:
                    w16[p][slot, j * half + r:j * half + r + 256, :] = v

    @pl.when(g < na_ref[0])
    def _():
        slot = slot_ref[e]
        k = pos_ref[g]
        last = k == len_ref[g] - 1
        nxt = next_ref[e]
        has_next = nxt >= 0

        @pl.when(g == 0)
        def _():
            for p in range(3):
                issue(p, e)
                consume(p, e, slot)

        for p in range(3):
            @pl.when(jnp.logical_and(has_next, k == p + 1))
            def _():
                consume(p, nxt, 1 - slot)

        for p in range(3):
            @pl.when(jnp.logical_and(has_next, jnp.logical_or(k == p, jnp.logical_and(last, k < p))))
            def _():
                issue(p, nxt)

            @pl.when(jnp.logical_and(has_next, jnp.logical_and(last, k <= p)))
            def _():
                consume(p, nxt, 1 - slot)

        xb = xs_ref[...].astype(BF16)
        a = jnp.dot(xb, wg16[slot], preferred_element_type=F32)
        b = jnp.dot(xb, wu16[slot], preferred_element_type=F32)
        hid = (a * jax.nn.sigmoid(a) * b).astype(BF16)
        o_ref[...] = jnp.dot(hid, wd16[slot], preferred_element_type=F32)

    @pl.when(g >= na_ref[0])
    def _():
        o_ref[...] = jnp.zeros_like(o_ref)


def moe_experts(xs, runs, w_gate, w_up, w_down, layer):
    total, d = xs.shape
    ff = w_gate.shape[3]
    assert d == 2 * ff, "half of every weight matrix must have the staging shape (ff, ff)"
    n_groups = total // MOE_ROWS

    def blk(g, ge, na, *_):
        return (jnp.maximum(jnp.minimum(g, na[0] - 1), 0), 0)

    return pl.pallas_call(
        functools.partial(_experts_kernel, layer),
        grid_spec=pltpu.PrefetchScalarGridSpec(
            num_scalar_prefetch=len(runs), grid=(n_groups,),
            in_specs=[pl.BlockSpec((MOE_ROWS, d), blk)] + [pl.BlockSpec(memory_space=pl.ANY)] * 3,
            out_specs=pl.BlockSpec((MOE_ROWS, d), lambda g, *_: (g, 0)),
            scratch_shapes=[pltpu.VMEM((2, d, ff), BF16),
                            pltpu.VMEM((2, d, ff), BF16),
                            pltpu.VMEM((2, ff, d), BF16),
                            pltpu.VMEM((2, ff, ff), F32),
                            pltpu.SemaphoreType.DMA((2,))]),
        out_shape=jax.ShapeDtypeStruct((total, d), F32),
        compiler_params=_cparams(("arbitrary",)),
        name="moe_experts",
    )(*runs, xs, w_gate, w_up, w_down)


def _combine_kernel(emit_x, emit_norm, pos_ref, yb_hbm, x_ref, p_ref, g_ref, *rest):
    out_refs, (ybuf, sem) = rest[:-2], rest[-2:]
    i = pl.program_id(0)
    n = pl.num_programs(0)
    tm = x_ref.shape[0]
    t = n * tm

    def issue(tile, slot):
        def body(rb, c):
            r0 = pl.multiple_of(rb * ROW_UNROLL, ROW_UNROLL)
            for u in range(ROW_UNROLL):
                for k in range(2):
                    row = pos_ref[k * t + tile * tm + r0 + u]
                    pltpu.make_async_copy(yb_hbm.at[pl.ds(row, 1)],
                                          ybuf.at[slot, k, pl.ds(r0 + u, 1)],
                                          sem.at[slot]).start()
            return c
        lax.fori_loop(0, tm // ROW_UNROLL, body, 0)

    @pl.when(i == 0)
    def _():
        issue(0, 0)

    @pl.when(i + 1 < n)
    def _():
        issue(i + 1, (i + 1) % 2)

    slot = i % 2
    for k in range(2):
        pltpu.make_async_copy(yb_hbm.at[pl.ds(0, tm)], ybuf.at[slot, k], sem.at[slot]).wait()
    p = p_ref[...]
    out = x_ref[...] + (p[:, 0:1] * ybuf[slot, 0] + p[:, 1:2] * ybuf[slot, 1])
    outs = ([out] if emit_x else []) + ([_rms(out, g_ref[...])] if emit_norm else [])
    for o_ref, val in zip(out_refs, outs):
        o_ref[...] = val


def moe_combine(yb, dest, gates_t, x, norm_g=None, keep_x=True, tm=512):
    t, d = x.shape
    emit_norm = norm_g is not None
    emit_x = keep_x or not emit_norm
    g = (norm_g if emit_norm else jnp.ones((d,), F32)).reshape(1, d)
    n_out = int(emit_x) + int(emit_norm)
    row_spec = pl.BlockSpec((tm, d), lambda i, pos: (i, 0))
    grid_spec = pltpu.PrefetchScalarGridSpec(
        num_scalar_prefetch=1,
        grid=(t // tm,),
        in_specs=[pl.BlockSpec(memory_space=pl.ANY),
                  row_spec,
                  pl.BlockSpec((tm, 2), lambda i, pos: (i, 0)),
                  pl.BlockSpec((1, d), lambda i, pos: (0, 0))],
        out_specs=[row_spec] * n_out,
        scratch_shapes=[pltpu.VMEM((2, 2, tm, d), F32),
                        pltpu.SemaphoreType.DMA((2,))],
    )
    outs = pl.pallas_call(
        functools.partial(_combine_kernel, emit_x, emit_norm),
        grid_spec=grid_spec,
        out_shape=[jax.ShapeDtypeStruct((t, d), F32)] * n_out,
        compiler_params=_cparams(("arbitrary",)),
        name="moe_combine",
    )(dest, yb, x, gates_t, g)
    return outs if n_out > 1 else outs[0]


def hier_moe_residual(x, g_ffn, w_coarse, b_coarse, w_fine, b_fine, w_gate, w_up, w_down,
                      layer, norm_g=None, keep_x=True):
    experts, gates, rank, counts = moe_router(x, g_ffn, w_coarse, b_coarse, w_fine, b_fine)
    dest, runs, pad_start, pad_len = _dispatch_plan(experts, rank, counts)
    total = runs[0].shape[0] * MOE_ROWS
    xs = moe_dispatch(x, g_ffn, dest, pad_start, pad_len, runs[1], total)
    yb = moe_experts(xs, runs, w_gate, w_up, w_down, layer)
    return moe_combine(yb, dest, gates.T, x, norm_g, keep_x)


def kernel(x, g_mix, g_ffn, g_final, w_in_even, conv_w, hg_lower_bound, hg_norm_g, w_out_even, s5_lambda_re, s5_lambda_im, s5_log_dt, s5_b_re, s5_b_im, s5_c_re, s5_c_im, s5_d, w_glu, moe_w_coarse, moe_b_coarse, moe_w_fine, moe_b_fine, moe_w_gate, moe_w_up, moe_w_down):
    bsz, seq, d = x.shape
    depth = g_mix.shape[0]
    conv_width = conv_w.shape[2]
    hg_width = hg_lower_bound.shape[1]
    lb_all = jnp.cumsum(jax.nn.softmax(hg_lower_bound.astype(F32), axis=0), axis=0)
    xf = x.reshape(bsz * seq, d)
    h = None
    for layer in range(depth):
        j = layer // 2
        last = layer == depth - 1
        if layer % 2 == 0:
            z = norm_matmul(xf, g_mix[layer], w_in_even[j].astype(BF16))
            o_b = hgrn2(z, lb_all[j], hg_norm_g[j], bsz, seq, hg_width, 3 * conv_width)
            xf = outproj_even(z, conv_w[j], o_b, xf, w_out_even[j].astype(BF16), seq)
        else:
            if h is None:
                h = rmsnorm_cast(xf, g_mix[layer], F32)
            tables = _s5_tables(s5_lambda_re[j], s5_lambda_im[j], s5_log_dt[j],
                                s5_b_re[j], s5_b_im[j], s5_c_re[j], s5_c_im[j], s5_d[j])
            y = s5_mix(h, tables, bsz, seq)
            xf = glu_residual(y, xf, w_glu[j].astype(BF16))
        next_s5 = not last and (layer + 1) % 2 == 1
        norm_g = g_final if last else (g_mix[layer + 1] if next_s5 else None)
        out = hier_moe_residual(
            xf, g_ffn[layer], moe_w_coarse[layer], moe_b_coarse[layer], moe_w_fine[layer],
            moe_b_fine[layer], moe_w_gate, moe_w_up, moe_w_down, layer, norm_g,
            keep_x=not last)
        xf, h = out if next_s5 else (out, None)
    return xf.reshape(bsz, seq, d)
```

```python
import functools
import math

import jax
import jax.numpy as jnp
from jax import lax
from jax.experimental import pallas as pl
from jax.experimental.pallas import tpu as pltpu

F32 = jnp.float32
BF16 = jnp.bfloat16
EPS = 1e-6

LANES = 128
HG_HEAD_DIM = 128
HG_CHUNK = 64
CONV_K = 3
S5_GROUP = 16
S5_STATE = 64
S5_CHUNK = 64
MOE_GROUPS = 4
MOE_EPG = 8
MOE_EXPERTS = MOE_GROUPS * MOE_EPG
MOE_ROWS = 256
ROUTER_ROWS = 40
ROW_UNROLL = 8
VMEM_LIMIT = 56 * 1024 * 1024


def _cparams(sem, vmem=VMEM_LIMIT):
    return pltpu.CompilerParams(dimension_semantics=sem, vmem_limit_bytes=vmem)


def _rms(x, g):
    ms = jnp.mean(x * x, axis=-1, keepdims=True)
    return x * lax.rsqrt(ms + EPS) * g


def _norm_kernel(x_ref, g_ref, o_ref):
    o_ref[...] = _rms(x_ref[...], g_ref[...]).astype(o_ref.dtype)


def rmsnorm_cast(x, g, dtype, tm=512):
    t, d = x.shape
    return pl.pallas_call(
        _norm_kernel,
        grid=(t // tm,),
        in_specs=[pl.BlockSpec((tm, d), lambda i: (i, 0)),
                  pl.BlockSpec((1, d), lambda i: (0, 0))],
        out_specs=pl.BlockSpec((tm, d), lambda i: (i, 0)),
        out_shape=jax.ShapeDtypeStruct((t, d), dtype),
        compiler_params=_cparams(("parallel",)),
        name="rmsnorm_cast",
    )(x, g.reshape(1, d))


def _norm_matmul_kernel(x_ref, g_ref, w_ref, o_ref, h_ref):
    @pl.when(pl.program_id(1) == 0)
    def _():
        h_ref[...] = _rms(x_ref[...], g_ref[...]).astype(BF16)

    o_ref[...] = jnp.dot(h_ref[...], w_ref[...],
                         preferred_element_type=F32).astype(o_ref.dtype)


def norm_matmul(x, g, w, tm=1024, tn=1024):
    t, d = x.shape
    n = w.shape[1]
    tm = min(tm, t)
    return pl.pallas_call(
        _norm_matmul_kernel,
        grid=(t // tm, n // tn),
        in_specs=[pl.BlockSpec((tm, d), lambda i, j: (i, 0)),
                  pl.BlockSpec((1, d), lambda i, j: (0, 0)),
                  pl.BlockSpec((d, tn), lambda i, j: (0, j))],
        out_specs=pl.BlockSpec((tm, tn), lambda i, j: (i, j)),
        out_shape=jax.ShapeDtypeStruct((t, n), BF16),
        scratch_shapes=[pltpu.VMEM((tm, d), BF16)],
        compiler_params=_cparams(("parallel", "arbitrary")),
        name="norm_matmul",
    )(x, g.reshape(1, d), w)


def _split2(x):
    hi = x.astype(BF16)
    return hi, (x - hi.astype(F32)).astype(BF16)


def _hgrn2_kernel(q_ref, f_ref, v_ref, g_ref, lb_ref, ng_ref, o_ref, st_ref):
    c = HG_CHUNK
    hd = HG_HEAD_DIM
    heads = st_ref.shape[0]
    n_chunks = q_ref.shape[0] // c
    ng = ng_ref[...]
    row = lax.broadcasted_iota(jnp.int32, (c, c), 0)
    col = lax.broadcasted_iota(jnp.int32, (c, c), 1)
    causal = row >= col
    tril = causal.astype(BF16)
    mid = c // 2

    @pl.when(pl.program_id(1) == 0)
    def _():
        st_ref[...] = jnp.zeros_like(st_ref)

    nt_dims = (((1,), (1,)), ((), ()))
    hs = range(heads)
    cols = [slice(h * hd, (h + 1) * hd) for h in hs]

    def step(n, carry):
        rows = pl.ds(pl.multiple_of(n * c, c), c)
        q = [q_ref[rows, cl].astype(F32) for cl in cols]
        f = [lb_ref[:, cl] + (1.0 - lb_ref[:, cl]) * jax.nn.sigmoid(f_ref[rows, cl].astype(F32))
             for cl in cols]
        k = [1.0 - fh for fh in f]
        parts = [_split2(jnp.log(fh)) for fh in f]
        b = [sum(jnp.dot(tril, p, preferred_element_type=F32) for p in ph) for ph in parts]
        ref = [bh[mid - 1:mid, :] for bh in b]
        b_last = [bh[c - 1:c, :] for bh in b]
        st = [st_ref[h] for h in hs]
        qs = [(q[h] * jnp.exp(b[h] - ref[h])).astype(BF16) for h in hs]
        ks = [(k[h] * jnp.exp(ref[h] - b[h])).astype(BF16) for h in hs]
        scores = [lax.dot_general(qs[h], ks[h], nt_dims, preferred_element_type=F32) for h in hs]
        qe = [(q[h] * jnp.exp(b[h])).astype(BF16) for h in hs]
        o_inter = [lax.dot_general(qe[h], st[h].astype(BF16), nt_dims, preferred_element_type=F32)
                   for h in hs]
        kd = [(k[h] * jnp.exp(b_last[h] - b[h])).astype(BF16) for h in hs]
        upd = [lax.dot_general(v_ref[rows, cols[h]], kd[h], (((0,), (0,)), ((), ())),
                               preferred_element_type=F32) for h in hs]
        for h in hs:
            st_ref[h] = st[h] * jnp.exp(b_last[h]) + upd[h]
        sc = [jnp.where(causal, s, 0.0).astype(BF16) for s in scores]
        o = [o_inter[h] + jnp.dot(sc[h], v_ref[rows, cols[h]], preferred_element_type=F32)
             for h in hs]
        for h in hs:
            oh = o[h] * lax.rsqrt(jnp.mean(o[h] * o[h], axis=-1, keepdims=True) + EPS) * ng
            gate = g_ref[rows, cols[h]].astype(F32)
            o_ref[rows, cols[h]] = (oh * (gate * jax.nn.sigmoid(gate))).astype(o_ref.dtype)
        return carry

    lax.fori_loop(0, n_chunks, step, 0, unroll=4)


def hgrn2(z, lb, norm_g, bsz, seq, width, col0, ts=1024):
    hd = HG_HEAD_DIM
    heads = width // hd
    cb = col0 // width
    ts = min(ts, seq)
    nt = seq // ts

    def zspec(k):
        return pl.BlockSpec((ts, width), lambda b, s: (b * nt + s, cb + k))

    return pl.pallas_call(
        _hgrn2_kernel,
        grid=(bsz, nt),
        in_specs=[zspec(0), zspec(1), zspec(2), zspec(3),
                  pl.BlockSpec((1, width), lambda b, s: (0, 0)),
                  pl.BlockSpec((1, hd), lambda b, s: (0, 0))],
        out_specs=pl.BlockSpec((ts, width), lambda b, s: (b * nt + s, 0)),
        out_shape=jax.ShapeDtypeStruct((bsz * seq, width), BF16),
        scratch_shapes=[pltpu.VMEM((heads, hd, hd), F32)],
        compiler_params=_cparams(("parallel", "arbitrary")),
        name="hgrn2",
    )(z, z, z, z, lb.reshape(1, width), norm_g.reshape(1, hd))


def _outproj_kernel(seq_tiles, ab_ref, ac_ref, ah_ref, hc_ref, hh_ref, cw_ref,
                    ob_ref, x_ref, wa_ref, wb_ref, o_ref):
    i = pl.program_id(0)
    u = ac_ref[...].astype(F32) * ah_ref[...].astype(F32)
    halo = hc_ref[...].astype(F32) * hh_ref[...].astype(F32)
    halo = jnp.where(i % seq_tiles == 0, 0.0, halo)
    hr = halo.shape[0]
    row = lax.broadcasted_iota(jnp.int32, u.shape, 0)
    u1 = jnp.where(row == 0, halo[hr - 1:hr, :], pltpu.roll(u, 1, axis=0))
    u2 = jnp.where(row == 0, halo[hr - 2:hr - 1, :],
                   jnp.where(row == 1, halo[hr - 1:hr, :], pltpu.roll(u, 2, axis=0)))
    cw = cw_ref[...]
    conv = cw[2:3, :] * u + cw[1:2, :] * u1 + cw[0:1, :] * u2
    ya = (ab_ref[...].astype(F32) * conv).astype(BF16)
    mix = jnp.dot(ya, wa_ref[...], preferred_element_type=F32)
    mix = mix + jnp.dot(ob_ref[...], wb_ref[...], preferred_element_type=F32)
    o_ref[...] = x_ref[...] + mix


def outproj_even(z, conv_w, o_b, x, w_out, seq, tm=512, halo=16):
    t, d = x.shape
    tm = min(tm, seq)
    cw = conv_w.shape[1]
    hw = o_b.shape[1]
    hb = tm // halo
    kern = functools.partial(_outproj_kernel, seq // tm)
    return pl.pallas_call(
        kern,
        grid=(t // tm,),
        in_specs=[pl.BlockSpec((tm, cw), lambda i: (i, 0)),
                  pl.BlockSpec((tm, cw), lambda i: (i, 1)),
                  pl.BlockSpec((tm, cw), lambda i: (i, 2)),
                  pl.BlockSpec((halo, cw), lambda i: (jnp.maximum(i * hb - 1, 0), 1)),
                  pl.BlockSpec((halo, cw), lambda i: (jnp.maximum(i * hb - 1, 0), 2)),
                  pl.BlockSpec((CONV_K, cw), lambda i: (0, 0)),
                  pl.BlockSpec((tm, hw), lambda i: (i, 0)),
                  pl.BlockSpec((tm, d), lambda i: (i, 0)),
                  pl.BlockSpec((cw, d), lambda i: (0, 0)),
                  pl.BlockSpec((hw, d), lambda i: (1, 0))],
        out_specs=pl.BlockSpec((tm, d), lambda i: (i, 0)),
        out_shape=jax.ShapeDtypeStruct((t, d), F32),
        compiler_params=_cparams(("parallel",)),
        name="outproj_even",
    )(z, z, z, z, z, conv_w, o_b, x, w_out, w_out)


def _rep_rows(x, n):
    r, l = x.shape
    return jnp.broadcast_to(x[:, None, :], (r, n, l)).reshape(r * n, l)


def _tile_rows(x, n):
    r, l = x.shape
    return jnp.broadcast_to(x[None, :, :], (n, r, l)).reshape(n * r, l)


def _s5_group(n_chunks, prm, u, m_ref):
    lc = S5_CHUNK
    gc = S5_GROUP
    half = S5_STATE
    nt_dims = (((1,), (1,)), ((), ()))
    bqa, ce1, ce2, bwa, bwb, cga, cgb = (prm[i * gc:(i + 1) * gc, :] for i in range(7))
    alpha = prm[8 * gc:8 * gc + 1, :]
    beta = prm[8 * gc + 1:8 * gc + 2, :]

    def cpow(tt):
        mag = jnp.exp(alpha * tt)
        ang = beta * tt
        return mag * jnp.cos(ang), mag * jnp.sin(ang)

    t_col = lax.broadcasted_iota(jnp.int32, (lc, LANES), 0).astype(F32)
    p0r, p0i = cpow(t_col)
    prr, pri = cpow(float(lc - 1) - t_col)
    a1r, a1i = cpow(jnp.full((1, LANES), 1.0, F32))
    p1r, p1i = p0r * a1r - p0i * a1i, p0r * a1i + p0i * a1r

    ecat = _rep_rows(p0r, gc) * _tile_rows(ce1, lc) + _rep_rows(p0i, gc) * _tile_rows(ce2, lc)
    b_hi, b_lo = _split2(bqa)
    e_hi, e_lo = _split2(ecat)
    nt_dot = lambda a, b: lax.dot_general(a, b, nt_dims, preferred_element_type=F32)
    k_row = nt_dot(b_hi, e_hi) + nt_dot(b_hi, e_lo) + nt_dot(b_lo, e_hi)
    spv = LANES // gc
    width = lc * gc
    pos = lax.broadcasted_iota(jnp.int32, k_row.shape, 1)
    for r in range(spv):
        base = k_row if r == 0 else jnp.where(pos >= r * gc, pltpu.roll(k_row, r * gc, axis=1), 0.0)
        base = base.astype(BF16)
        for q in range(lc // spv):
            s = q * spv + r
            blk = base if q == 0 else jnp.concatenate(
                [jnp.zeros((gc, q * LANES), BF16), base[:, :width - q * LANES]], axis=1)
            m_ref[s * gc:(s + 1) * gc, :] = blk

    wcat = (_rep_rows(prr, gc) * _tile_rows(bwa, lc)
            + _rep_rows(pri, gc) * _tile_rows(bwb, lc)).astype(BF16)
    st = jnp.dot(u, wcat, preferred_element_type=F32)

    rows = st.shape[0]
    rpos = lax.broadcasted_iota(jnp.int32, (rows, LANES), 0) % n_chunks
    rlane = lax.broadcasted_iota(jnp.int32, (rows, LANES), 1)
    ar, ai = cpow(jnp.full((1, LANES), float(lc), F32))
    d = 1
    while d < n_chunks:
        sh = jnp.where(rpos >= d, pltpu.roll(st, d, axis=0), 0.0)
        a2 = jnp.where(rlane < half, -ai, ai)
        st = st + ar * sh + a2 * pltpu.roll(sh, half, axis=1)
        ar, ai = ar * ar - ai * ai, 2.0 * ar * ai
        d *= 2
    h0 = jnp.where(rpos >= 1, pltpu.roll(st, 1, axis=0), 0.0).astype(BF16)

    gcat = (_rep_rows(p1r, gc) * _tile_rows(cga, lc)
            + _rep_rows(p1i, gc) * _tile_rows(cgb, lc)).astype(BF16)
    y = jnp.dot(u, m_ref[...], preferred_element_type=F32)
    y = y + lax.dot_general(h0, gcat, nt_dims, preferred_element_type=F32)
    d_row = prm[8 * gc + 2:8 * gc + 3, :]
    return y + jnp.concatenate([d_row] * (lc * gc // LANES), axis=1) * u.astype(F32)


def _s5_kernel(n_chunks, h_ref, p_ref, perm_ref, y_ref, u_ref, m_ref):
    lc = S5_CHUNK
    gps = u_ref.shape[0]
    rows = u_ref.shape[1]
    spv = LANES // S5_GROUP
    lane_blk = lambda k: slice(k * LANES, (k + 1) * LANES)

    for j in range(lc // spv):
        xcat = jnp.concatenate(
            [h_ref[pl.ds(j * spv + s, rows, stride=lc), :] for s in range(spv)], axis=1)
        uall = jnp.dot(xcat.astype(BF16), perm_ref[...], preferred_element_type=F32).astype(BF16)
        for gi in range(gps):
            u_ref[gi, :, lane_blk(j)] = uall[:, lane_blk(gi)]

    n_m = m_ref.shape[0]

    def groups(i, carry):
        for b in range(n_m):
            gi = i * n_m + b
            y = _s5_group(n_chunks, p_ref[gi], u_ref[gi], m_ref.at[b])
            u_ref[gi] = y.astype(BF16)
        return carry
    lax.fori_loop(0, gps // n_m, groups, 0)

    for j in range(lc // spv):
        ycat = jnp.concatenate([u_ref[gi, :, lane_blk(j)] for gi in range(gps)], axis=1)
        z = jnp.dot(ycat, perm_ref[...], preferred_element_type=F32)
        for s in range(spv):
            y_ref[pl.ds(j * spv + s, rows, stride=lc), :] = z[:, lane_blk(s)]


def _s5_tables(lam_re, lam_im, log_dt, b_re, b_im, c_re, c_im, d_skip):
    dt = jnp.exp(log_dt.astype(F32))[:, None]
    lr = lam_re.astype(F32)
    li = lam_im.astype(F32)
    mag = jnp.exp(lr * dt)
    ar = mag * jnp.cos(li * dt)
    ai = mag * jnp.sin(li * dt)
    den = lr * lr + li * li
    cr = ((ar - 1.0) * lr + ai * li) / den
    ci = (ai * lr - (ar - 1.0) * li) / den
    bbr = (cr[..., None] * b_re - ci[..., None] * b_im).transpose(0, 2, 1)
    bbi = (cr[..., None] * b_im + ci[..., None] * b_re).transpose(0, 2, 1)
    cre = c_re.astype(F32)
    cim = c_im.astype(F32)
    cat = lambda a, b: jnp.concatenate([a, b], axis=-1)
    alpha = (lr * dt)[:, None, :]
    beta = (li * dt)[:, None, :]
    groups, states = lr.shape
    gc = b_re.shape[2]
    d_row = jnp.tile(d_skip.astype(F32).reshape(groups, 1, gc), (1, 1, 2 * states // gc))
    pad = jnp.zeros((groups, 5, 2 * states), F32)
    return jnp.concatenate([
        cat(bbr, -bbi), cat(cre, cim), cat(-cim, cre),
        cat(bbr, bbi), cat(-bbi, bbr),
        cat(cre, -cim), cat(-cim, -cre), jnp.zeros_like(cat(cre, cre)),
        cat(alpha, alpha), cat(beta, beta), d_row, pad], axis=1)


def s5_mix(h, tables, bsz, seq):
    t, w = h.shape
    lc, gc = S5_CHUNK, S5_GROUP
    gps = LANES // gc
    slabs = w // LANES
    n = seq // lc
    rows = bsz * n
    k = lc * gc
    idx = jnp.arange(k, dtype=jnp.int32)
    tgt = (idx % LANES) // gc * LANES + idx // LANES * gc + idx % gc
    perm = (tgt[:, None] == idx[None, :]).astype(BF16)
    tables = tables.reshape((slabs, gps) + tables.shape[1:])
    return pl.pallas_call(
        functools.partial(_s5_kernel, n),
        grid=(slabs,),
        in_specs=[pl.BlockSpec((t, LANES), lambda i: (0, i)),
                  pl.BlockSpec((None, gps) + tables.shape[2:], lambda i: (i, 0, 0, 0)),
                  pl.BlockSpec((k, k), lambda i: (0, 0))],
        out_specs=pl.BlockSpec((t, LANES), lambda i: (0, i)),
        out_shape=jax.ShapeDtypeStruct((t, w), F32),
        scratch_shapes=[pltpu.VMEM((gps, rows, k), BF16),
                        pltpu.VMEM((2, k, k), BF16)],
        compiler_params=_cparams(("parallel",)),
        name="s5_mix",
    )(h, tables, perm)


def _glu_kernel(tn, y_ref, x_ref, w_ref, o_ref):
    y = y_ref[...]
    cdf = 0.5 * (1.0 + jnp.tanh(math.sqrt(2.0 / math.pi) * (y + 0.044715 * (y * y * y))))
    act = (y * cdf).astype(BF16)
    d = o_ref.shape[1]
    for c in range(0, d, tn):
        za = jnp.dot(act, w_ref[:, c:c + tn], preferred_element_type=F32)
        zb = jnp.dot(act, w_ref[:, d + c:d + c + tn], preferred_element_type=F32)
        o_ref[:, c:c + tn] = x_ref[:, c:c + tn] + za * jax.nn.sigmoid(zb)


def glu_residual(y, x, w_glu, tm=512, tn=512):
    t, d = x.shape
    w = y.shape[1]
    tm = min(tm, t)
    return pl.pallas_call(
        functools.partial(_glu_kernel, tn),
        grid=(t // tm,),
        in_specs=[pl.BlockSpec((tm, w), lambda i: (i, 0)),
                  pl.BlockSpec((tm, d), lambda i: (i, 0)),
                  pl.BlockSpec((w, 2 * d), lambda i: (0, 0), pipeline_mode=pl.Buffered(1))],
        out_specs=pl.BlockSpec((tm, d), lambda i: (i, 0)),
        out_shape=jax.ShapeDtypeStruct((t, d), F32),
        compiler_params=_cparams(("parallel",)),
        name="glu_residual",
    )(y, x, w_glu)


def _router_kernel(x_ref, g_ref, w_ref, b_ref, e_ref, p_ref, r_ref, c_ref, cnt_ref):
    tm = x_ref.shape[0]

    @pl.when(pl.program_id(0) == 0)
    def _():
        cnt_ref[...] = jnp.zeros_like(cnt_ref)

    h = _rms(x_ref[...], g_ref[...])
    nt_dot = lambda a, b: lax.dot_general(a, b, (((1,), (1,)), ((), ())),
                                          preferred_element_type=F32)
    w_hi, w_lo = _split2(w_ref[...])
    h_hi, h_lo = _split2(h)
    logits = (nt_dot(w_hi, h_hi) + nt_dot(w_hi, h_lo) + nt_dot(w_lo, h_hi)
              + b_ref[...])
    lc = [logits[i:i + 1, :] for i in range(MOE_GROUPS)]
    m = functools.reduce(jnp.maximum, lc)
    grp = jnp.full(m.shape, MOE_GROUPS - 1, jnp.int32)
    for i in range(MOE_GROUPS - 2, -1, -1):
        grp = jnp.where(lc[i] == m, i, grp)
    den = functools.reduce(lambda a, b: a + b, [jnp.exp(l - m) for l in lc])
    p_top = 1.0 / den
    fine = [logits[MOE_GROUPS + j:MOE_GROUPS + j + 1, :] for j in range(MOE_EXPERTS)]
    sel = []
    for e in range(MOE_EPG):
        v = fine[(MOE_GROUPS - 1) * MOE_EPG + e]
        for i in range(MOE_GROUPS - 2, -1, -1):
            v = jnp.where(grp == i, fine[i * MOE_EPG + e], v)
        sel.append(v)

    def top1(vals):
        best = functools.reduce(jnp.maximum, vals)
        idx = jnp.full(best.shape, MOE_EPG - 1, jnp.int32)
        for e in range(MOE_EPG - 2, -1, -1):
            idx = jnp.where(vals[e] == best, e, idx)
        return best, idx

    v1, i1 = top1(sel)
    v2, i2 = top1([jnp.where(i1 == e, -jnp.inf, sel[e]) for e in range(MOE_EPG)])
    ex = jnp.exp(v2 - v1)
    s = 1.0 + ex
    e0 = grp * MOE_EPG + i1
    e1 = grp * MOE_EPG + i2
    e_ref[0:1, :] = e0
    e_ref[1:2, :] = e1
    p_ref[0:1, :] = p_top * (1.0 / s)
    p_ref[1:2, :] = p_top * (ex / s)

    eid = lax.broadcasted_iota(jnp.int32, (MOE_EXPERTS, tm), 0)
    tri = (lax.broadcasted_iota(jnp.int32, (tm, tm), 0)
           <= lax.broadcasted_iota(jnp.int32, (tm, tm), 1)).astype(BF16)
    base = cnt_ref[...]
    for k, ek in enumerate((e0, e1)):
        hot = eid == ek
        hot_f = hot.astype(F32)
        csum = jnp.dot(hot.astype(BF16), tri, preferred_element_type=F32)
        rank = jnp.sum(hot_f * (base + csum - 1.0), axis=0, keepdims=True)
        r_ref[k:k + 1, :] = rank.astype(jnp.int32)
        base = base + jnp.sum(hot_f, axis=1, keepdims=True)
    cnt_ref[...] = base
    c_ref[...] = base.astype(jnp.int32)


def moe_router(x, g, w_coarse, b_coarse, w_fine, b_fine, tm=512):
    t, d = x.shape
    pad = ROUTER_ROWS - MOE_GROUPS - MOE_EXPERTS
    w = jnp.concatenate([w_coarse.T, w_fine.T, jnp.zeros((pad, d), F32)], axis=0)
    b = jnp.concatenate([b_coarse, b_fine, jnp.zeros((pad,), F32)]).reshape(ROUTER_ROWS, 1)
    tok_spec = pl.BlockSpec((2, tm), lambda i: (0, i))
    return pl.pallas_call(
        _router_kernel,
        grid=(t // tm,),
        in_specs=[pl.BlockSpec((tm, d), lambda i: (i, 0)),
                  pl.BlockSpec((1, d), lambda i: (0, 0)),
                  pl.BlockSpec((ROUTER_ROWS, d), lambda i: (0, 0)),
                  pl.BlockSpec((ROUTER_ROWS, 1), lambda i: (0, 0))],
        out_specs=[tok_spec, tok_spec, tok_spec,
                   pl.BlockSpec((MOE_EXPERTS, 1), lambda i: (0, 0))],
        out_shape=[jax.ShapeDtypeStruct((2, t), jnp.int32),
                   jax.ShapeDtypeStruct((2, t), F32),
                   jax.ShapeDtypeStruct((2, t), jnp.int32),
                   jax.ShapeDtypeStruct((MOE_EXPERTS, 1), jnp.int32)],
        scratch_shapes=[pltpu.VMEM((MOE_EXPERTS, 1), F32)],
        compiler_params=_cparams(("arbitrary",)),
        name="moe_router",
    )(x, g.reshape(1, d), w, b)


def _dispatch_plan(experts, rank, counts):
    n_assign = experts.size
    counts = counts.reshape(-1)
    padded = (counts + MOE_ROWS - 1) // MOE_ROWS * MOE_ROWS
    pend = jnp.cumsum(padded)
    pstart = pend - padded
    eids = jnp.arange(MOE_EXPERTS, dtype=jnp.int32)
    dest = rank + jnp.sum(jnp.where(experts[..., None] == eids, pstart, 0), axis=-1)
    n_groups = -(-n_assign // MOE_ROWS) + MOE_EXPERTS
    g0 = jnp.arange(n_groups, dtype=jnp.int32) * MOE_ROWS
    grp_expert = jnp.minimum(jnp.sum(pend[None, :] <= g0[:, None], axis=1), MOE_EXPERTS - 1)
    n_active = (pend[-1] // MOE_ROWS).reshape(1)
    of_group = lambda v: jnp.sum(jnp.where(grp_expert[:, None] == eids, v, 0), axis=1)
    grp_pos = (g0 - of_group(pstart)) // MOE_ROWS
    grp_len = of_group(padded) // MOE_ROWS
    used = counts > 0
    e_slot = (jnp.cumsum(used) - 1) % 2
    later_used = jnp.logical_and(used[None, :], eids[None, :] > eids[:, None])
    e_next = jnp.min(jnp.where(later_used, eids[None, :], MOE_EXPERTS), axis=1)
    e_next = jnp.where(e_next == MOE_EXPERTS, -1, e_next)
    i32 = lambda a: a.astype(jnp.int32)
    runs = (i32(grp_expert), i32(n_active), i32(grp_pos), i32(grp_len), i32(e_slot), i32(e_next))
    return i32(dest).reshape(-1), runs, i32(pstart + counts), i32(padded - counts)


def _dispatch_kernel(dest_ref, ps_ref, pn_ref, na_ref, x_ref, g_ref, xs_hbm, hbuf, zbuf, sem,
                     zsem):
    i = pl.program_id(0)
    n = pl.num_programs(0)
    tm = x_ref.shape[0]
    t = n * tm
    slot = i % 2

    def wait_slot(s):
        for _ in range(2):
            pltpu.make_async_copy(hbuf.at[s], xs_hbm.at[pl.ds(0, tm)], sem.at[s]).wait()

    def zero_row_copy(row):
        return pltpu.make_async_copy(zbuf.at[pl.ds(0, 1)], xs_hbm.at[pl.ds(row, 1)], zsem.at[0])

    def for_each_pad_row(fn):
        def per_expert(e, c):
            def per_row(r, c2):
                fn(ps_ref[e] + r)
                return c2
            return lax.fori_loop(0, pn_ref[e], per_row, c)
        lax.fori_loop(0, ps_ref.shape[0], per_expert, 0)

    def tail_group_copy(grp):
        rows = pl.ds(pl.multiple_of(grp * tm, tm), tm)
        return pltpu.make_async_copy(hbuf.at[1], xs_hbm.at[rows], sem.at[1])

    def for_each_tail_group(fn):
        def body(grp, c):
            fn(grp)
            return c
        lax.fori_loop(na_ref[0], xs_hbm.shape[0] // tm, body, 0)

    @pl.when(i == 0)
    def _():
        zbuf[...] = jnp.zeros_like(zbuf)
        for_each_pad_row(lambda row: zero_row_copy(row).start())
        hbuf[1] = jnp.zeros(hbuf.shape[1:], hbuf.dtype)
        for_each_tail_group(lambda grp: tail_group_copy(grp).start())
        for_each_tail_group(lambda grp: tail_group_copy(grp).wait())

    @pl.when(i >= 2)
    def _():
        wait_slot(slot)

    hbuf[slot] = _rms(x_ref[...], g_ref[...])

    def body(rb, c):
        r0 = pl.multiple_of(rb * ROW_UNROLL, ROW_UNROLL)
        for u in range(ROW_UNROLL):
            for k in range(2):
                row = dest_ref[k * t + i * tm + r0 + u]
                pltpu.make_async_copy(hbuf.at[slot, pl.ds(r0 + u, 1)],
                                      xs_hbm.at[pl.ds(row, 1)],
                                      sem.at[slot]).start(priority=k)
        return c
    lax.fori_loop(0, tm // ROW_UNROLL, body, 0)

    @pl.when(i == n - 1)
    def _():
        wait_slot(slot)

        @pl.when(n >= 2)
        def _():
            wait_slot(1 - slot)

        for_each_pad_row(lambda row: zero_row_copy(row).wait())


def moe_dispatch(x, g, dest, pad_start, pad_len, n_active, total):
    t, d = x.shape
    tm = MOE_ROWS
    grid_spec = pltpu.PrefetchScalarGridSpec(
        num_scalar_prefetch=4,
        grid=(t // tm,),
        in_specs=[pl.BlockSpec((tm, d), lambda i, *_: (i, 0)),
                  pl.BlockSpec((1, d), lambda i, *_: (0, 0))],
        out_specs=pl.BlockSpec(memory_space=pl.ANY),
        scratch_shapes=[pltpu.VMEM((2, tm, d), F32),
                        pltpu.VMEM((8, d), F32),
                        pltpu.SemaphoreType.DMA((2,)),
                        pltpu.SemaphoreType.DMA((1,))],
    )
    return pl.pallas_call(
        _dispatch_kernel,
        grid_spec=grid_spec,
        out_shape=jax.ShapeDtypeStruct((total, d), F32),
        compiler_params=_cparams(("arbitrary",)),
        name="moe_dispatch",
    )(dest, pad_start, pad_len, n_active, x, g.reshape(1, d))


def _experts_kernel(layer, ge_ref, na_ref, pos_ref, len_ref, slot_ref, next_ref,
                    xs_ref, wg_hbm, wu_hbm, wd_hbm, o_ref, wg16, wu16, wd16, stage, sem):
    g = pl.program_id(0)
    e = ge_ref[g]
    half = stage.shape[1]
    w_hbm = (wg_hbm, wu_hbm, wd_hbm)
    w16 = (wg16, wu16, wd16)

    def half_src(p, expert, j):
        w = w_hbm[p].at[layer, expert]
        rows = pl.ds(j * half, half)
        return w.at[:, rows] if p == 2 else w.at[rows, :]

    def copies(p, expert):
        return [pltpu.make_async_copy(half_src(p, expert, j), stage.at[j], sem.at[j])
                for j in range(2)]

    def issue(p, expert):
        for c in copies(p, expert):
            c.start()

    def consume(p, expert, slot):
        for j, c in enumerate(copies(p, expert)):
            c.wait()
            for r in range(0, half, 256):
                v = stage[j, r:r + 256, :].astype(BF16)
                if p == 2:
                    w16[p][slot, r:r + 256, j * half:(j + 1) * half] = v
                else:
                    w16[p][slot, j * half + r:j * half + r + 256, :] = v

    @pl.when(g < na_ref[0])
    def _():
        slot = slot_ref[e]
        k = pos_ref[g]
        last = k == len_ref[g] - 1
        nxt = next_ref[e]
        has_next = nxt >= 0

        @pl.when(g == 0)
        def _():
            for p in range(3):
                issue(p, e)
                consume(p, e, slot)

        for p in range(3):
            @pl.when(jnp.logical_and(has_next, k == p + 1))
            def _():
                consume(p, nxt, 1 - slot)

        for p in range(3):
            @pl.when(jnp.logical_and(has_next, jnp.logical_or(k == p, jnp.logical_and(last, k < p))))
            def _():
                issue(p, nxt)

            @pl.when(jnp.logical_and(has_next, jnp.logical_and(last, k <= p)))
            def _():
                consume(p, nxt, 1 - slot)

        xb = xs_ref[...].astype(BF16)
        a = jnp.dot(xb, wg16[slot], preferred_element_type=F32)
        b = jnp.dot(xb, wu16[slot], preferred_element_type=F32)
        hid = (a * jax.nn.sigmoid(a) * b).astype(BF16)
        o_ref[...] = jnp.dot(hid, wd16[slot], preferred_element_type=F32)

    @pl.when(g >= na_ref[0])
    def _():
        o_ref[...] = jnp.zeros_like(o_ref)


def moe_experts(xs, runs, w_gate, w_up, w_down, layer):
    total, d = xs.shape
    ff = w_gate.shape[3]
    assert d == 2 * ff, "half of every weight matrix must have the staging shape (ff, ff)"
    n_groups = total // MOE_ROWS

    def blk(g, ge, na, *_):
        return (jnp.maximum(jnp.minimum(g, na[0] - 1), 0), 0)

    return pl.pallas_call(
        functools.partial(_experts_kernel, layer),
        grid_spec=pltpu.PrefetchScalarGridSpec(
            num_scalar_prefetch=len(runs), grid=(n_groups,),
            in_specs=[pl.BlockSpec((MOE_ROWS, d), blk)] + [pl.BlockSpec(memory_space=pl.ANY)] * 3,
            out_specs=pl.BlockSpec((MOE_ROWS, d), lambda g, *_: (g, 0)),
            scratch_shapes=[pltpu.VMEM((2, d, ff), BF16),
                            pltpu.VMEM((2, d, ff), BF16),
                            pltpu.VMEM((2, ff, d), BF16),
                            pltpu.VMEM((2, ff, ff), F32),
                            pltpu.SemaphoreType.DMA((2,))]),
        out_shape=jax.ShapeDtypeStruct((total, d), F32),
        compiler_params=_cparams(("arbitrary",)),
        name="moe_experts",
    )(*runs, xs, w_gate, w_up, w_down)


def _combine_kernel(emit_x, emit_norm, pos_ref, yb_hbm, x_ref, p_ref, g_ref, *rest):
    out_refs, (ybuf, sem) = rest[:-2], rest[-2:]
    i = pl.program_id(0)
    n = pl.num_programs(0)
    tm = x_ref.shape[0]
    t = n * tm

    def issue(tile, slot):
        def body(rb, c):
            r0 = pl.multiple_of(rb * ROW_UNROLL, ROW_UNROLL)
            for u in range(ROW_UNROLL):
                for k in range(2):
                    row = pos_ref[k * t + tile * tm + r0 + u]
                    pltpu.make_async_copy(yb_hbm.at[pl.ds(row, 1)],
                                          ybuf.at[slot, k, pl.ds(r0 + u, 1)],
                                          sem.at[slot]).start(priority=k)
            return c
        lax.fori_loop(0, tm // ROW_UNROLL, body, 0)

    @pl.when(i == 0)
    def _():
        issue(0, 0)

    @pl.when(i + 1 < n)
    def _():
        issue(i + 1, (i + 1) % 2)

    slot = i % 2
    for k in range(2):
        pltpu.make_async_copy(yb_hbm.at[pl.ds(0, tm)], ybuf.at[slot, k], sem.at[slot]).wait()
    p = p_ref[...]
    out = x_ref[...] + (p[:, 0:1] * ybuf[slot, 0] + p[:, 1:2] * ybuf[slot, 1])
    outs = ([out] if emit_x else []) + ([_rms(out, g_ref[...])] if emit_norm else [])
    for o_ref, val in zip(out_refs, outs):
        o_ref[...] = val


def moe_combine(yb, dest, gates_t, x, norm_g=None, keep_x=True, tm=256):
    t, d = x.shape
    emit_norm = norm_g is not None
    emit_x = keep_x or not emit_norm
    g = (norm_g if emit_norm else jnp.ones((d,), F32)).reshape(1, d)
    n_out = int(emit_x) + int(emit_norm)
    row_spec = pl.BlockSpec((tm, d), lambda i, pos: (i, 0))
    grid_spec = pltpu.PrefetchScalarGridSpec(
        num_scalar_prefetch=1,
        grid=(t // tm,),
        in_specs=[pl.BlockSpec(memory_space=pl.ANY),
                  row_spec,
                  pl.BlockSpec((tm, 2), lambda i, pos: (i, 0)),
                  pl.BlockSpec((1, d), lambda i, pos: (0, 0))],
        out_specs=[row_spec] * n_out,
        scratch_shapes=[pltpu.VMEM((2, 2, tm, d), F32),
                        pltpu.SemaphoreType.DMA((2,))],
    )
    outs = pl.pallas_call(
        functools.partial(_combine_kernel, emit_x, emit_norm),
        grid_spec=grid_spec,
        out_shape=[jax.ShapeDtypeStruct((t, d), F32)] * n_out,
        compiler_params=_cparams(("arbitrary",)),
        name="moe_combine",
    )(dest, yb, x, gates_t, g)
    return outs if n_out > 1 else outs[0]


def hier_moe_residual(x, g_ffn, w_coarse, b_coarse, w_fine, b_fine, w_gate, w_up, w_down,
                      layer, norm_g=None, keep_x=True):
    experts, gates, rank, counts = moe_router(x, g_ffn, w_coarse, b_coarse, w_fine, b_fine)
    dest, runs, pad_start, pad_len = _dispatch_plan(experts, rank, counts)
    total = runs[0].shape[0] * MOE_ROWS
    xs = moe_dispatch(x, g_ffn, dest, pad_start, pad_len, runs[1], total)
    yb = moe_experts(xs, runs, w_gate, w_up, w_down, layer)
    return moe_combine(yb, dest, gates.T, x, norm_g, keep_x)


def kernel(x, g_mix, g_ffn, g_final, w_in_even, conv_w, hg_lower_bound, hg_norm_g, w_out_even, s5_lambda_re, s5_lambda_im, s5_log_dt, s5_b_re, s5_b_im, s5_c_re, s5_c_im, s5_d, w_glu, moe_w_coarse, moe_b_coarse, moe_w_fine, moe_b_fine, moe_w_gate, moe_w_up, moe_w_down):
    bsz, seq, d = x.shape
    depth = g_mix.shape[0]
    conv_width = conv_w.shape[2]
    hg_width = hg_lower_bound.shape[1]
    lb_all = jnp.cumsum(jax.nn.softmax(hg_lower_bound.astype(F32), axis=0), axis=0)
    xf = x.reshape(bsz * seq, d)
    h = None
    for layer in range(depth):
        j = layer // 2
        last = layer == depth - 1
        if layer % 2 == 0:
            z = norm_matmul(xf, g_mix[layer], w_in_even[j].astype(BF16))
            o_b = hgrn2(z, lb_all[j], hg_norm_g[j], bsz, seq, hg_width, 3 * conv_width)
            xf = outproj_even(z, conv_w[j], o_b, xf, w_out_even[j].astype(BF16), seq)
        else:
            if h is None:
                h = rmsnorm_cast(xf, g_mix[layer], F32)
            tables = _s5_tables(s5_lambda_re[j], s5_lambda_im[j], s5_log_dt[j],
                                s5_b_re[j], s5_b_im[j], s5_c_re[j], s5_c_im[j], s5_d[j])
            y = s5_mix(h, tables, bsz, seq)
            xf = glu_residual(y, xf, w_glu[j].astype(BF16))
        next_s5 = not last and (layer + 1) % 2 == 1
        norm_g = g_final if last else (g_mix[layer + 1] if next_s5 else None)
        out = hier_moe_residual(
            xf, g_ffn[layer], moe_w_coarse[layer], moe_b_coarse[layer], moe_w_fine[layer],
            moe_b_fine[layer], moe_w_gate, moe_w_up, moe_w_down, layer, norm_g,
            keep_x=not last)
        xf, h = out if next_s5 else (out, None)
    return xf.reshape(bsz, seq, d)
```
